```python
import jax, jax.numpy as jnp
from jax import lax
import numpy as np

D_MODEL = 1024
BATCH = 16
SEQ = 256
DEPTH = 2
DEC_BATCH = 2
DEC_SEQ = 4096
PAST_LEN = 512

GRID_W = 64
HEAD_DIM = 64
N_HEADS_TOTAL = D_MODEL // HEAD_DIM
N_HEADS_A = N_HEADS_TOTAL // 2
N_KV_A = N_HEADS_A // 4
N_HEADS_B = N_HEADS_TOTAL - N_HEADS_A
N_KV_B = N_HEADS_B // 4
N_HEADS_C = N_HEADS_TOTAL
G_A = N_HEADS_A // N_KV_A
G_B = N_HEADS_B // N_KV_B
MIX_WIDTH = N_HEADS_TOTAL * HEAD_DIM
EVEN_IN = (N_HEADS_A + N_HEADS_B + 2 * N_KV_A + 2 * N_KV_B) * HEAD_DIM
ODD_IN = 3 * N_HEADS_C * HEAD_DIM
BLOCK = 128
WINDOW = 128
NA_ROWS = 8
NA_COLS = 16
ROPE_THETA = 10000.0
N_EXPERTS = 64
TOP_K = 8
D_EXPERT = 128
D_SHARED = 128
ROUTED_SCALE = 2.5
ALPHA = (2 * DEPTH) ** 0.25
BETA = (8 * DEPTH) ** -0.25
N_EVEN = (DEPTH + 1) // 2
N_ODD = DEPTH // 2
LN_EPS = 1e-6
RMS_EPS = 1e-6
NEG_BIG = -1e30

kernel_name = 'hybrid_dit_context_prefix_step'

f32 = jnp.float32


def layer_norm(x, g, b):
    xf = x.astype(f32)
    mu = jnp.mean(xf, axis=-1, keepdims=True)
    var = jnp.mean(jnp.square(xf - mu), axis=-1, keepdims=True)
    return ((xf - mu) * lax.rsqrt(var + LN_EPS) * g.astype(f32) + b.astype(f32)).astype(x.dtype)


def rms_norm(x, g):
    xf = x.astype(f32)
    y = xf * lax.rsqrt(jnp.mean(xf * xf, axis=-1, keepdims=True) + RMS_EPS)
    return (y * g.astype(f32)).astype(x.dtype)


def rope_2d(x):
    S = x.shape[1]
    t = jnp.arange(S)
    row = (t // GRID_W).astype(f32)
    col = (t % GRID_W).astype(f32)
    quarter = HEAD_DIM // 4
    inv = ROPE_THETA ** (-jnp.arange(quarter, dtype=f32) / quarter)
    ar = row[:, None] * inv
    ac = col[:, None] * inv
    ang = jnp.concatenate([ar, ar, ac, ac], axis=-1)[None, :, None, :]
    x1, x2, x3, x4 = jnp.split(x.astype(f32), 4, axis=-1)
    rot = jnp.concatenate([-x2, x1, -x4, x3], axis=-1)
    return (x.astype(f32) * jnp.cos(ang) + rot * jnp.sin(ang)).astype(x.dtype)


def softmax_with_sink(s, sink):
    if sink is None:
        return jax.nn.softmax(s, axis=-1)
    sk = jnp.broadcast_to(sink.astype(f32)[None, :, :, None, None], s.shape[:-1] + (1,))
    return jax.nn.softmax(jnp.concatenate([s, sk], axis=-1), axis=-1)[..., :-1]


def group_heads(q, n_kv):
    B, S, H, hd = q.shape
    return q.reshape(B, S, n_kv, H // n_kv, hd)


def dense_attention(q, k, v, sink=None):
    B, S, KV, G, hd = q.shape
    nb = S // BLOCK
    qb = q.reshape(B, nb, BLOCK, KV, G, hd).swapaxes(0, 1)
    scale = hd ** -0.5

    def blk(qi):
        s = jnp.einsum('bqkgd,btkd->bkgqt', qi, k, preferred_element_type=f32) * scale
        p = softmax_with_sink(s, sink)
        return jnp.einsum('bkgqt,btkd->bqkgd', p.astype(v.dtype), v)

    o = lax.map(blk, qb)
    return o.swapaxes(0, 1).reshape(B, S, KV * G * hd)


def window_attention(q, k, v, kc, vc, sink):
    B, S, KV, G, hd = q.shape
    nb = S // BLOCK
    span = BLOCK + 2 * WINDOW
    qb = q.reshape(B, nb, BLOCK, KV, G, hd).swapaxes(0, 1)
    pad = ((0, 0), (WINDOW, WINDOW), (0, 0), (0, 0))
    kp = jnp.pad(k, pad)
    vp = jnp.pad(v, pad)
    scale = hd ** -0.5

    def blk(args):
        i, qi = args
        start = i * BLOCK
        kw = lax.dynamic_slice_in_dim(kp, start, span, axis=1)
        vw = lax.dynamic_slice_in_dim(vp, start, span, axis=1)
        qpos = start + jnp.arange(BLOCK)
        kpos = start - WINDOW + jnp.arange(span)
        valid = (jnp.abs(qpos[:, None] - kpos[None, :]) <= WINDOW) & (kpos >= 0)[None, :] & (kpos < S)[None, :]
        s_loc = jnp.einsum('bqkgd,btkd->bkgqt', qi, kw, preferred_element_type=f32) * scale
        s_loc = jnp.where(valid, s_loc, NEG_BIG)
        s_ctx = jnp.einsum('bqkgd,btkd->bkgqt', qi, kc, preferred_element_type=f32) * scale
        p = softmax_with_sink(jnp.concatenate([s_loc, s_ctx], axis=-1), sink).astype(v.dtype)
        return (jnp.einsum('bkgqt,btkd->bqkgd', p[..., :span], vw)
                + jnp.einsum('bkgqt,btkd->bqkgd', p[..., span:], vc))

    o = lax.map(blk, (jnp.arange(nb), qb))
    return o.swapaxes(0, 1).reshape(B, S, KV * G * hd)


def neighbourhood_attention(q, k, v, kc, vc, rpb):
    B, S, H, hd = q.shape
    rows = S // GRID_W
    kh = min(NA_ROWS, rows)
    kw = NA_COLS
    n_nb = kh * kw
    kg = k.reshape(B, rows, GRID_W, H, hd)
    vg = v.reshape(B, rows, GRID_W, H, hd)
    qr = q.reshape(B, rows, GRID_W, H, hd).swapaxes(0, 1)
    cols = jnp.arange(GRID_W)
    cs = jnp.clip(cols - kw // 2, 0, GRID_W - kw)
    col_idx = cs[:, None] + jnp.arange(kw)[None, :]
    dc = col_idx - cols[:, None] + (NA_COLS - 1)
    bias_c = rpb[:, :, dc]
    scale = hd ** -0.5

    def row_blk(args):
        r, qi = args
        rs = jnp.clip(r - kh // 2, 0, rows - kh)
        krow = lax.dynamic_slice_in_dim(kg, rs, kh, axis=1)
        vrow = lax.dynamic_slice_in_dim(vg, rs, kh, axis=1)
        kn = krow[:, :, col_idx]
        vn = vrow[:, :, col_idx]
        dr = rs + jnp.arange(kh) - r + (NA_ROWS - 1)
        bias = jnp.take(bias_c, dr, axis=1).transpose(0, 2, 1, 3)
        s_nb = jnp.einsum('bqhd,biqjhd->bhqij', qi, kn, preferred_element_type=f32) * scale + bias[None].astype(f32)
        s_ctx = jnp.einsum('bqhd,bthd->bhqt', qi, kc, preferred_element_type=f32) * scale
        s = jnp.concatenate([s_nb.reshape(B, H, GRID_W, n_nb), s_ctx], axis=-1)
        p = jax.nn.softmax(s, axis=-1).astype(v.dtype)
        p_nb = p[..., :n_nb].reshape(B, H, GRID_W, kh, kw)
        return (jnp.einsum('bhqij,biqjhd->bqhd', p_nb, vn)
                + jnp.einsum('bhqt,bthd->bqhd', p[..., n_nb:], vc))

    o = lax.map(row_blk, (jnp.arange(rows), qr))
    return o.swapaxes(0, 1).reshape(B, S, H * hd)


def even_project(h, w_in, qn, kn):
    B, S, _ = h.shape
    dqa = N_HEADS_A * HEAD_DIM
    dqb = N_HEADS_B * HEAD_DIM
    dka = N_KV_A * HEAD_DIM
    dkb = N_KV_B * HEAD_DIM
    cuts = [dqa, dqa + dqb, dqa + dqb + dka, dqa + dqb + dka + dkb, dqa + dqb + 2 * dka + dkb]
    qa, qb, ka, kb, va, vb = jnp.split(jnp.matmul(h, w_in), cuts, axis=-1)
    qa = rms_norm(qa.reshape(B, S, N_HEADS_A, HEAD_DIM), qn)
    ka = rms_norm(ka.reshape(B, S, N_KV_A, HEAD_DIM), kn)
    qb = qb.reshape(B, S, N_HEADS_B, HEAD_DIM)
    kb = kb.reshape(B, S, N_KV_B, HEAD_DIM)
    va = va.reshape(B, S, N_KV_A, HEAD_DIM)
    vb = vb.reshape(B, S, N_KV_B, HEAD_DIM)
    return qa, qb, ka, kb, va, vb


def even_mixer_context(h, w_in, w_out, qn, kn, sink):
    qa, qb, ka, kb, va, vb = even_project(h, w_in, qn, kn)
    oa = dense_attention(group_heads(qa, N_KV_A), ka, va)
    ob = dense_attention(group_heads(qb, N_KV_B), kb, vb, sink.reshape(N_KV_B, G_B))
    out = jnp.matmul(jnp.concatenate([oa, ob], axis=-1), w_out)
    return out, (ka, va, kb, vb)


def even_mixer_latent(h, kca, vca, kcb, vcb, w_in, w_out, qn, kn, sink):
    qa, qb, ka, kb, va, vb = even_project(h, w_in, qn, kn)
    qa, ka, qb, kb = rope_2d(qa), rope_2d(ka), rope_2d(qb), rope_2d(kb)
    oa = dense_attention(group_heads(qa, N_KV_A),
                         jnp.concatenate([kca, ka], axis=1), jnp.concatenate([vca, va], axis=1))
    ob = window_attention(group_heads(qb, N_KV_B), kb, vb, kcb, vcb, sink.reshape(N_KV_B, G_B))
    return jnp.matmul(jnp.concatenate([oa, ob], axis=-1), w_out)


def odd_project(h, w_in):
    B, S, _ = h.shape
    q, k, v = jnp.split(jnp.matmul(h, w_in), 3, axis=-1)
    shp = (B, S, N_HEADS_C, HEAD_DIM)
    return q.reshape(shp), k.reshape(shp), v.reshape(shp)


def odd_mixer_context(h, w_in, w_out):
    q, k, v = odd_project(h, w_in)
    o = dense_attention(q[:, :, :, None, :], k, v)
    return jnp.matmul(o, w_out), (k, v)


def odd_mixer_latent(h, kc, vc, w_in, w_out, rpb):
    q, k, v = odd_project(h, w_in)
    return jnp.matmul(neighbourhood_attention(q, k, v, kc, vc, rpb), w_out)


def adaln(cond, w, b):
    m = jnp.matmul(jax.nn.silu(cond), w) + b
    return [t[:, None, :] for t in jnp.split(m, 6, axis=-1)]


def modulate(x, shift, scale):
    return x * (1 + scale) + shift


def moe(x, w_router, b_router, w_gate, w_up, w_down, ws_gate, ws_up, ws_down):
    scores = jax.nn.sigmoid(jnp.matmul(x, w_router, preferred_element_type=f32))
    _, idx = lax.top_k(scores + b_router.astype(f32), TOP_K)
    g = jnp.take_along_axis(scores, idx, axis=-1)
    g = g / jnp.sum(g, axis=-1, keepdims=True) * ROUTED_SCALE
    gates = jnp.einsum('tk,tke->te', g, jax.nn.one_hot(idx, N_EXPERTS, dtype=f32))

    def expert(acc, p):
        wg, wu, wd, ge = p
        a = jax.nn.silu(jnp.matmul(x, wg)) * jnp.matmul(x, wu)
        return acc + ge[:, None].astype(x.dtype) * jnp.matmul(a, wd), None

    routed, _ = lax.scan(expert, jnp.zeros_like(x), (w_gate, w_up, w_down, gates.T))
    shared = jnp.matmul(jax.nn.silu(jnp.matmul(x, ws_gate)) * jnp.matmul(x, ws_up), ws_down)
    return routed + shared


def setup_inputs(seed: int = 0) -> dict:
    key = jax.random.key(seed)
    ks = jax.random.split(key, 32)
    D = D_MODEL

    def nrm(k, shape, s):
        return jax.random.normal(k, shape, f32) * s

    return {
        'x_prompt': nrm(ks[0], (BATCH, SEQ, D), 1.0),
        'x_sample': nrm(ks[1], (DEC_BATCH, DEC_SEQ, D), 1.0),
        'cache_ka': nrm(ks[2], (DEC_BATCH, N_EVEN, PAST_LEN, N_KV_A, HEAD_DIM), 1.0),
        'cache_va': nrm(ks[3], (DEC_BATCH, N_EVEN, PAST_LEN, N_KV_A, HEAD_DIM), 1.0),
        'cache_kb': nrm(ks[4], (DEC_BATCH, N_EVEN, PAST_LEN, N_KV_B, HEAD_DIM), 1.0),
        'cache_vb': nrm(ks[5], (DEC_BATCH, N_EVEN, PAST_LEN, N_KV_B, HEAD_DIM), 1.0),
        'cache_kc': nrm(ks[6], (DEC_BATCH, N_ODD, PAST_LEN, N_HEADS_C, HEAD_DIM), 1.0),
        'cache_vc': nrm(ks[7], (DEC_BATCH, N_ODD, PAST_LEN, N_HEADS_C, HEAD_DIM), 1.0),
        'c': nrm(ks[8], (DEC_BATCH, D), 1.0),
        'c_ctx': nrm(ks[9], (D,), 1.0),
        'w_ada': nrm(ks[10], (DEPTH, D, 6 * D), D ** -0.5),
        'b_ada': nrm(ks[11], (DEPTH, 6 * D), 0.01),
        'ln_g': 1.0 + nrm(ks[12], (DEPTH, 2, D), 0.01),
        'ln_b': nrm(ks[13], (DEPTH, 2, D), 0.01),
        'w_in_even': nrm(ks[14], (N_EVEN, D, EVEN_IN), D ** -0.5),
        'w_out_even': nrm(ks[15], (N_EVEN, MIX_WIDTH, D), MIX_WIDTH ** -0.5 * BETA),
        'qnorm_a': 1.0 + nrm(ks[16], (N_EVEN, HEAD_DIM), 0.01),
        'knorm_a': 1.0 + nrm(ks[17], (N_EVEN, HEAD_DIM), 0.01),
        'sink_b': nrm(ks[18], (N_EVEN, N_HEADS_B), 0.5),
        'w_in_odd': nrm(ks[19], (N_ODD, D, ODD_IN), D ** -0.5),
        'w_out_odd': nrm(ks[20], (N_ODD, MIX_WIDTH, D), MIX_WIDTH ** -0.5 * BETA),
        'rpb_c': nrm(ks[21], (N_ODD, N_HEADS_C, 2 * NA_ROWS - 1, 2 * NA_COLS - 1), 0.1),
        'w_router': nrm(ks[22], (DEPTH, D, N_EXPERTS), D ** -0.5),
        'b_router': nrm(ks[23], (DEPTH, N_EXPERTS), 0.01),
        'w_gate': nrm(ks[24], (DEPTH, N_EXPERTS, D, D_EXPERT), D ** -0.5),
        'w_up': nrm(ks[25], (DEPTH, N_EXPERTS, D, D_EXPERT), D ** -0.5),
        'w_down': nrm(ks[26], (DEPTH, N_EXPERTS, D_EXPERT, D), D_EXPERT ** -0.5 * BETA),
        'ws_gate': nrm(ks[27], (DEPTH, D, D_SHARED), D ** -0.5),
        'ws_up': nrm(ks[28], (DEPTH, D, D_SHARED), D ** -0.5),
        'ws_down': nrm(ks[29], (DEPTH, D_SHARED, D), D_SHARED ** -0.5 * BETA),
    }


def reference(x_prompt, x_sample, cache_ka, cache_va, cache_kb, cache_vb, cache_kc, cache_vc,
              c, c_ctx, w_ada, b_ada, ln_g, ln_b, w_in_even, w_out_even, qnorm_a, knorm_a, sink_b,
              w_in_odd, w_out_odd, rpb_c, w_router, b_router, w_gate, w_up, w_down,
              ws_gate, ws_up, ws_down):
    D = D_MODEL
    y_p, y_s = x_prompt, x_sample
    cond_ctx = c_ctx[None, :]
    new_ka, new_va, new_kb, new_vb, new_kc, new_vc = [], [], [], [], [], []
    for l in range(DEPTH):
        mp = adaln(cond_ctx, w_ada[l], b_ada[l])
        ms = adaln(c, w_ada[l], b_ada[l])
        hp = modulate(y_p, mp[0], mp[1])
        hs = modulate(y_s, ms[0], ms[1])
        if l % 2 == 0:
            e = l // 2
            out_p, (ka, va, kb, vb) = even_mixer_context(hp, w_in_even[e], w_out_even[e], qnorm_a[e], knorm_a[e], sink_b[e])
            out_s = even_mixer_latent(hs, cache_ka[:, e], cache_va[:, e], cache_kb[:, e], cache_vb[:, e],
                                      w_in_even[e], w_out_even[e], qnorm_a[e], knorm_a[e], sink_b[e])
            new_ka.append(ka)
            new_va.append(va)
            new_kb.append(kb)
            new_vb.append(vb)
        else:
            o = l // 2
            out_p, (kc, vc) = odd_mixer_context(hp, w_in_odd[o], w_out_odd[o])
            out_s = odd_mixer_latent(hs, cache_kc[:, o], cache_vc[:, o], w_in_odd[o], w_out_odd[o], rpb_c[o])
            new_kc.append(kc)
            new_vc.append(vc)
        y_p = layer_norm(ALPHA * y_p + mp[2] * out_p, ln_g[l, 0], ln_b[l, 0])
        y_s = layer_norm(ALPHA * y_s + ms[2] * out_s, ln_g[l, 0], ln_b[l, 0])
        hp = modulate(y_p, mp[3], mp[4])
        hs = modulate(y_s, ms[3], ms[4])
        n_p = hp.shape[0] * hp.shape[1]
        f = moe(jnp.concatenate([hp.reshape(-1, D), hs.reshape(-1, D)], axis=0),
                w_router[l], b_router[l], w_gate[l], w_up[l], w_down[l], ws_gate[l], ws_up[l], ws_down[l])
        f_p = f[:n_p].reshape(y_p.shape)
        f_s = f[n_p:].reshape(y_s.shape)
        y_p = layer_norm(ALPHA * y_p + mp[5] * f_p, ln_g[l, 1], ln_b[l, 1])
        y_s = layer_norm(ALPHA * y_s + ms[5] * f_s, ln_g[l, 1], ln_b[l, 1])
    return (y_p, y_s,
            jnp.stack(new_ka, axis=1), jnp.stack(new_va, axis=1),
            jnp.stack(new_kb, axis=1), jnp.stack(new_vb, axis=1),
            jnp.stack(new_kc, axis=1), jnp.stack(new_vc, axis=1))
```

```python
import functools

import numpy as np
import jax
import jax.numpy as jnp
from jax import lax
from jax.experimental import pallas as pl
from jax.experimental.pallas import tpu as pltpu

F32 = jnp.float32
BF16 = jnp.bfloat16

D_MODEL = 1024
BATCH = 16
SEQ = 256
DEPTH = 2
DEC_BATCH = 2
DEC_SEQ = 4096
PAST_LEN = 512
GRID_W = 64
HEAD_DIM = 64
N_HEADS_A = 8
N_KV_A = 2
N_HEADS_B = 8
N_KV_B = 2
N_HEADS_C = 16
EVEN_IN = 1536
ODD_IN = 3072
WINDOW = 128
NA_ROWS = 8
NA_COLS = 16
ROPE_THETA = 10000.0
N_EXPERTS = 64
TOP_K = 8
D_EXPERT = 128
ROUTED_SCALE = 2.5
ALPHA = (2 * DEPTH) ** 0.25
LN_EPS = 1e-6
RMS_EPS = 1e-6
NEG_BIG = -1e30

LANES = 128
P_TOK = BATCH * SEQ
S_TOK = DEC_BATCH * DEC_SEQ
TM = 512
TM_MOE = 1024
EXPERTS_PER_STEP = 4
NA_QROWS = 4
NA_KROWS = 12
VMEM_LIMIT = 56 * 1024 * 1024

_NT = (((1,), (1,)), ((), ()))


def _cparams(sem):
    return pltpu.CompilerParams(dimension_semantics=sem, vmem_limit_bytes=VMEM_LIMIT)


def _half_masks(dtype):
    lane = lax.broadcasted_iota(jnp.int32, (1, LANES), 1)
    lo = jnp.where(lane < HEAD_DIM, 1.0, 0.0).astype(dtype)
    hi = jnp.where(lane < HEAD_DIM, 0.0, 1.0).astype(dtype)
    return lo, hi


def _ada_kernel(c_ref, w_ref, b_ref, o_ref):
    c = c_ref[...]
    a = c / (1.0 + jnp.exp(-c))
    w = w_ref[0]
    a_hi = a.astype(BF16)
    a_lo = (a - a_hi.astype(F32)).astype(BF16)
    w_hi = w.astype(BF16)
    w_lo = (w - w_hi.astype(F32)).astype(BF16)
    acc = jnp.dot(a_hi, w_hi, preferred_element_type=F32)
    acc += jnp.dot(a_lo, w_hi, preferred_element_type=F32)
    acc += jnp.dot(a_hi, w_lo, preferred_element_type=F32)
    o_ref[0] = acc + b_ref[0]


def _adaln(cond8, w_ada, b_ada):
    tn = 1536
    return pl.pallas_call(
        _ada_kernel,
        grid=(DEPTH, 6 * D_MODEL // tn),
        in_specs=[
            pl.BlockSpec((8, D_MODEL), lambda l, j: (0, 0)),
            pl.BlockSpec((1, D_MODEL, tn), lambda l, j: (l, 0, j)),
            pl.BlockSpec((1, 1, tn), lambda l, j: (l, 0, j)),
        ],
        out_specs=pl.BlockSpec((1, 8, tn), lambda l, j: (l, 0, j)),
        out_shape=jax.ShapeDtypeStruct((DEPTH, 8, 6 * D_MODEL), F32),
        compiler_params=_cparams(("parallel", "parallel")),
        name="adaln",
    )(cond8, w_ada, b_ada.reshape(DEPTH, 1, 6 * D_MODEL))


def _group_sum_matrix():
    r = lax.broadcasted_iota(jnp.int32, (LANES, LANES), 0) // HEAD_DIM
    c = lax.broadcasted_iota(jnp.int32, (LANES, LANES), 1) // HEAD_DIM
    return jnp.where(r == c, 1.0, 0.0).astype(BF16)


def _rms_slab(t, g, gmat):
    sq = t * t
    hi = sq.astype(BF16)
    lo = (sq - hi.astype(F32)).astype(BF16)
    ss = jnp.dot(hi, gmat, preferred_element_type=F32) + jnp.dot(lo, gmat, preferred_element_type=F32)
    return t * lax.rsqrt(ss * (1.0 / HEAD_DIM) + RMS_EPS) * g


def _rope_slab(t, cos, sin_signed, first):
    r = jnp.where(first, pltpu.roll(t, LANES - 16, 1), pltpu.roll(t, 16, 1))
    return t * cos + r * sin_signed


def _proj_kernel(x_ref, mod_ref, w_ref, qn_ref, kn_ref, cos_ref, sin_ref, q_ref, k_ref, v_ref,
                 *, dq, dk, q_norm, k_norm, rope):
    x = x_ref[...]
    h = (x * (1.0 + mod_ref[1]) + mod_ref[0]).astype(BF16)
    y = jnp.dot(h, w_ref[...], preferred_element_type=F32)
    tm = x.shape[0]
    gmat = _group_sum_matrix() if (q_norm or k_norm) else None
    if rope:
        cos = cos_ref[...]
        sin_signed = sin_ref[...]
        first = (lax.broadcasted_iota(jnp.int32, (tm, LANES), 1) % 32) < 16
    for s in range(dq // LANES):
        t = y[:, s * LANES:(s + 1) * LANES]
        if s * LANES < q_norm:
            t = _rms_slab(t, qn_ref[...], gmat)
        if rope:
            t = _rope_slab(t, cos, sin_signed, first)
        q_ref[:, s * LANES:(s + 1) * LANES] = (t * (HEAD_DIM ** -0.5)).astype(q_ref.dtype)
    for s in range(dk // LANES):
        t = y[:, dq + s * LANES:dq + (s + 1) * LANES]
        if s * LANES < k_norm:
            t = _rms_slab(t, kn_ref[...], gmat)
        if rope:
            t = _rope_slab(t, cos, sin_signed, first)
        k_ref[:, s * LANES:(s + 1) * LANES] = t.astype(k_ref.dtype)
    v_ref[...] = y[:, dq + dk:].astype(v_ref.dtype)


def _project(x2d, mod, w_bf16, qn2, kn2, cos, sin_signed, *, dq, dk, dv, q_norm, k_norm, rope, latent):
    n = x2d.shape[0]
    per_batch = DEC_SEQ // TM
    if latent:
        mod_map = lambda i: (0, 1 + i // per_batch, 0, 0)
        pos_map = lambda i: (i % per_batch, 0)
        kv_dtype = BF16
    else:
        mod_map = lambda i: (0, 0, 0, 0)
        pos_map = lambda i: (0, 0)
        kv_dtype = F32
    kern = functools.partial(_proj_kernel, dq=dq, dk=dk, q_norm=q_norm, k_norm=k_norm, rope=rope)
    return pl.pallas_call(
        kern,
        grid=(n // TM,),
        in_specs=[
            pl.BlockSpec((TM, D_MODEL), lambda i: (i, 0)),
            pl.BlockSpec((6, None, 1, D_MODEL), mod_map),
            pl.BlockSpec((D_MODEL, dq + dk + dv), lambda i: (0, 0)),
            pl.BlockSpec((1, LANES), lambda i: (0, 0)),
            pl.BlockSpec((1, LANES), lambda i: (0, 0)),
            pl.BlockSpec((TM, LANES), pos_map),
            pl.BlockSpec((TM, LANES), pos_map),
        ],
        out_specs=[
            pl.BlockSpec((TM, dq), lambda i: (i, 0)),
            pl.BlockSpec((TM, dk), lambda i: (i, 0)),
            pl.BlockSpec((TM, dv), lambda i: (i, 0)),
        ],
        out_shape=[
            jax.ShapeDtypeStruct((n, dq), BF16),
            jax.ShapeDtypeStruct((n, dk), kv_dtype),
            jax.ShapeDtypeStruct((n, dv), kv_dtype),
        ],
        compiler_params=_cparams(("parallel",)),
        name="in_proj",
    )(x2d, mod, w_bf16, qn2, kn2, cos, sin_signed)


def _softmax_pv(parts, sink):
    m = parts[0][0].max(axis=1, keepdims=True)
    for s, _ in parts[1:]:
        m = jnp.maximum(m, s.max(axis=1, keepdims=True))
    if sink is not None:
        m = jnp.maximum(m, sink)
    l = None
    o = None
    for s, v in parts:
        p = jnp.exp(s - m)
        ps = p.sum(axis=1, keepdims=True)
        pv = jnp.dot(p.astype(BF16), v, preferred_element_type=F32)
        l = ps if l is None else l + ps
        o = pv if o is None else o + pv
    if sink is not None:
        l = l + jnp.exp(sink - m)
    return o / l


def _ctx_attn_kernel(sink_ref, q_ref, k_ref, v_ref, o_ref, *, kslab, sink_heads):
    lo, hi = _half_masks(BF16)
    lane_lo = lax.broadcasted_iota(jnp.int32, (q_ref.shape[0], LANES), 1) < HEAD_DIM
    for j in range(q_ref.shape[1] // LANES):
        qs = q_ref[:, j * LANES:(j + 1) * LANES]
        ks = kslab[j]
        kk = k_ref[:, ks * LANES:(ks + 1) * LANES].astype(BF16)
        vv = v_ref[:, ks * LANES:(ks + 1) * LANES].astype(BF16)
        outs = []
        for half, msk in enumerate((lo, hi)):
            s = lax.dot_general(qs * msk, kk, _NT, preferred_element_type=F32)
            sh = sink_heads[j][half]
            sink = None if sh is None else sink_ref[sh]
            outs.append(_softmax_pv([(s, vv)], sink))
        o_ref[:, j * LANES:(j + 1) * LANES] = jnp.where(lane_lo, outs[0], outs[1]).astype(o_ref.dtype)


def _ctx_attention(q, k, v, sink, *, kslab, sink_heads):
    kw = k.shape[1]
    kern = functools.partial(_ctx_attn_kernel, kslab=kslab, sink_heads=sink_heads)
    return pl.pallas_call(
        kern,
        grid=(BATCH,),
        in_specs=[
            pl.BlockSpec(memory_space=pltpu.SMEM),
            pl.BlockSpec((SEQ, D_MODEL), lambda b: (b, 0)),
            pl.BlockSpec((SEQ, kw), lambda b: (b, 0)),
            pl.BlockSpec((SEQ, kw), lambda b: (b, 0)),
        ],
        out_specs=pl.BlockSpec((SEQ, D_MODEL), lambda b: (b, 0)),
        out_shape=jax.ShapeDtypeStruct((P_TOK, D_MODEL), BF16),
        compiler_params=_cparams(("parallel",)),
        name="ctx_attn",
    )(sink, q, k, v)


def _global_attn_kernel(q_ref, k_ref, v_ref, o_ref, *, tk):
    tq = q_ref.shape[0]
    nk = k_ref.shape[0] // tk
    lo, hi = _half_masks(BF16)
    lane_lo = lax.broadcasted_iota(jnp.int32, (tq, LANES), 1) < HEAD_DIM
    for j in range(q_ref.shape[1] // LANES):
        qs = q_ref[:, j * LANES:(j + 1) * LANES]
        q_lo = qs * lo
        q_hi = qs * hi

        def body(c, carry):
            off = pl.multiple_of(c * tk, tk)
            kk = k_ref[pl.ds(off, tk), :]
            vv = v_ref[pl.ds(off, tk), :]
            new = []
            for qm, (m, l, acc) in zip((q_lo, q_hi), (carry[:3], carry[3:])):
                s = lax.dot_general(qm, kk, _NT, preferred_element_type=F32)
                m_new = jnp.maximum(m, s.max(axis=1, keepdims=True))
                alpha = jnp.exp(m - m_new)
                p = jnp.exp(s - m_new)
                l = alpha * l + p.sum(axis=1, keepdims=True)
                acc = alpha * acc + jnp.dot(p.astype(BF16), vv, preferred_element_type=F32)
                new += [m_new, l, acc]
            return tuple(new)

        init = (jnp.full((tq, 1), NEG_BIG, F32), jnp.zeros((tq, 1), F32), jnp.zeros((tq, LANES), F32)) * 2
        m0, l0, a0, m1, l1, a1 = lax.fori_loop(0, nk, body, init)
        o_ref[:, j * LANES:(j + 1) * LANES] = jnp.where(lane_lo, a0 / l0, a1 / l1).astype(o_ref.dtype)


def _global_attention(q, kcat, vcat):
    tq, tk = 256, 512
    t = kcat.shape[1]
    per_batch = DEC_SEQ // tq
    return pl.pallas_call(
        functools.partial(_global_attn_kernel, tk=tk),
        grid=(DEC_BATCH, per_batch),
        in_specs=[
            pl.BlockSpec((tq, 4 * LANES), lambda b, i: (b * per_batch + i, 0)),
            pl.BlockSpec((None, t, LANES), lambda b, i: (b, 0, 0)),
            pl.BlockSpec((None, t, LANES), lambda b, i: (b, 0, 0)),
        ],
        out_specs=pl.BlockSpec((tq, 4 * LANES), lambda b, i: (b * per_batch + i, 0)),
        out_shape=jax.ShapeDtypeStruct((S_TOK, 4 * LANES), BF16),
        compiler_params=_cparams(("parallel", "parallel")),
        name="global_attn",
    )(q, kcat, vcat)


def _window_attn_kernel(sink_ref, q_ref, k_ref, v_ref, o_ref):
    i = pl.program_id(1)
    tq = q_ref.shape[0]
    span = tq + 2 * WINDOW
    lo, hi = _half_masks(BF16)
    lane_lo = lax.broadcasted_iota(jnp.int32, (tq, LANES), 1) < HEAD_DIM
    off = pl.multiple_of(PAST_LEN - WINDOW + i * tq, LANES)
    k_loc = k_ref[pl.ds(off, span), :]
    v_loc = v_ref[pl.ds(off, span), :]
    k_ctx = k_ref[0:PAST_LEN, :]
    v_ctx = v_ref[0:PAST_LEN, :]
    row = lax.broadcasted_iota(jnp.int32, (tq, span), 0)
    col = lax.broadcasted_iota(jnp.int32, (tq, span), 1)
    kpos = i * tq - WINDOW + col
    valid = (col >= row) & (col <= row + 2 * WINDOW) & (kpos >= 0) & (kpos < DEC_SEQ)
    for j in range(q_ref.shape[1] // LANES):
        qs = q_ref[:, j * LANES:(j + 1) * LANES]
        outs = []
        for half, msk in enumerate((lo, hi)):
            qm = qs * msk
            s_loc = lax.dot_general(qm, k_loc, _NT, preferred_element_type=F32)
            s_loc = jnp.where(valid, s_loc, NEG_BIG)
            s_ctx = lax.dot_general(qm, k_ctx, _NT, preferred_element_type=F32)
            sink = sink_ref[j + 4 * half]
            outs.append(_softmax_pv([(s_loc, v_loc), (s_ctx, v_ctx)], sink))
        o_ref[:, j * LANES:(j + 1) * LANES] = jnp.where(lane_lo, outs[0], outs[1]).astype(o_ref.dtype)


def _window_attention(q, kcat, vcat, sink):
    tq = 128
    t = kcat.shape[1]
    per_batch = DEC_SEQ // tq
    return pl.pallas_call(
        _window_attn_kernel,
        grid=(DEC_BATCH, per_batch),
        in_specs=[
            pl.BlockSpec(memory_space=pltpu.SMEM),
            pl.BlockSpec((tq, 4 * LANES), lambda b, i: (b * per_batch + i, 1)),
            pl.BlockSpec((None, t, LANES), lambda b, i: (b, 0, 0)),
            pl.BlockSpec((None, t, LANES), lambda b, i: (b, 0, 0)),
        ],
        out_specs=pl.BlockSpec((tq, 4 * LANES), lambda b, i: (b * per_batch + i, 0)),
        out_shape=jax.ShapeDtypeStruct((S_TOK, 4 * LANES), BF16),
        compiler_params=_cparams(("parallel", "parallel")),
        name="window_attn",
    )(sink, q, kcat, vcat)


def _na_start_row(g):
    return jnp.clip(NA_QROWS * g - NA_ROWS // 2, 0, DEC_SEQ // GRID_W - NA_KROWS)


def _na_attn_kernel(q_ref, k_ref, v_ref, kc_ref, vc_ref, bias_ref, o_ref):
    g = pl.program_id(2)
    n_groups = pl.num_programs(2)
    tq = q_ref.shape[0]
    lo, hi = _half_masks(BF16)
    lane_lo = lax.broadcasted_iota(jnp.int32, (tq, LANES), 1) < HEAD_DIM
    off = pl.multiple_of(_na_start_row(g) * GRID_W, GRID_W)
    k_loc = k_ref[pl.ds(off, NA_KROWS * GRID_W), :]
    v_loc = v_ref[pl.ds(off, NA_KROWS * GRID_W), :]
    k_ctx = kc_ref[...]
    v_ctx = vc_ref[...]
    variant = jnp.where(g == 0, 0, jnp.where(g == n_groups - 1, 2, 1))
    qs = q_ref[...]
    outs = []
    for half, msk in enumerate((lo, hi)):
        qm = qs * msk
        s_loc = lax.dot_general(qm, k_loc, _NT, preferred_element_type=F32) + bias_ref[variant, half]
        s_ctx = lax.dot_general(qm, k_ctx, _NT, preferred_element_type=F32)
        outs.append(_softmax_pv([(s_loc, v_loc), (s_ctx, v_ctx)], None))
    o_ref[...] = jnp.where(lane_lo, outs[0], outs[1]).astype(o_ref.dtype)


def _na_bias_table(rpb):
    rows = DEC_SEQ // GRID_W
    n_groups = rows // NA_QROWS
    dr = np.zeros((3, NA_QROWS, NA_KROWS), np.int32)
    vr = np.zeros((3, NA_QROWS, NA_KROWS), bool)
    for vi, g in enumerate((0, 2, n_groups - 1)):
        start = int(np.clip(NA_QROWS * g - NA_ROWS // 2, 0, rows - NA_KROWS))
        for qr in range(NA_QROWS):
            r = NA_QROWS * g + qr
            rs = int(np.clip(r - NA_ROWS // 2, 0, rows - NA_ROWS))
            for kr in range(NA_KROWS):
                krow = start + kr
                vr[vi, qr, kr] = rs <= krow < rs + NA_ROWS
                dr[vi, qr, kr] = int(np.clip(krow - r + NA_ROWS - 1, 0, 2 * NA_ROWS - 2))
    cols = np.arange(GRID_W)
    cs = np.clip(cols - NA_COLS // 2, 0, GRID_W - NA_COLS)
    dc = cols[None, :] - cols[:, None] + NA_COLS - 1
    vc = (cols[None, :] >= cs[:, None]) & (cols[None, :] < cs[:, None] + NA_COLS)
    dc = np.clip(dc, 0, 2 * NA_COLS - 2)
    shape = (3, NA_QROWS, GRID_W, NA_KROWS, GRID_W)
    dr_full = np.broadcast_to(dr[:, :, None, :, None], shape).reshape(3, NA_QROWS * GRID_W, NA_KROWS * GRID_W)
    dc_full = np.broadcast_to(dc[None, None, :, None, :], shape).reshape(3, NA_QROWS * GRID_W, NA_KROWS * GRID_W)
    valid = np.broadcast_to(vr[:, :, None, :, None] & vc[None, None, :, None, :], shape).reshape(dr_full.shape)
    table = rpb.astype(F32)[:, dr_full, dc_full]
    table = jnp.where(valid[None], table, NEG_BIG)
    return jnp.transpose(table, (1, 0, 2, 3))


def _na_attention(q, k, v, kc, vc, bias):
    tq = NA_QROWS * GRID_W
    n_groups = DEC_SEQ // tq
    n_slabs = D_MODEL // LANES
    k3 = k.reshape(DEC_BATCH, DEC_SEQ, D_MODEL)
    v3 = v.reshape(DEC_BATCH, DEC_SEQ, D_MODEL)
    return pl.pallas_call(
        _na_attn_kernel,
        grid=(n_slabs, DEC_BATCH, n_groups),
        in_specs=[
            pl.BlockSpec((tq, LANES), lambda s, b, g: (b * n_groups + g, s)),
            pl.BlockSpec((None, DEC_SEQ, LANES), lambda s, b, g: (b, 0, s)),
            pl.BlockSpec((None, DEC_SEQ, LANES), lambda s, b, g: (b, 0, s)),
            pl.BlockSpec((None, PAST_LEN, LANES), lambda s, b, g: (b, 0, s)),
            pl.BlockSpec((None, PAST_LEN, LANES), lambda s, b, g: (b, 0, s)),
            pl.BlockSpec((3, 2, tq, NA_KROWS * GRID_W), lambda s, b, g: (0, s, 0, 0)),
        ],
        out_specs=pl.BlockSpec((tq, LANES), lambda s, b, g: (b * n_groups + g, s)),
        out_shape=jax.ShapeDtypeStruct((S_TOK, D_MODEL), BF16),
        compiler_params=_cparams(("parallel", "parallel", "parallel")),
        name="na_attn",
    )(q, k3, v3, kc, vc, bias)


def _layer_norm(z, g, b):
    mu = jnp.mean(z, axis=-1, keepdims=True)
    zc = z - mu
    var = jnp.mean(zc * zc, axis=-1, keepdims=True)
    return zc * lax.rsqrt(var + LN_EPS) * g + b


def _post_kernel(*refs, n_parts):
    o_refs = refs[:n_parts]
    (y_ref, mod_ref, w_ref, lng_ref, lnb_ref, wr_hi_ref, wr_lo_ref, br_ref,
     y_out_ref, h_out_ref, gates_ref) = refs[n_parts:]
    pw = D_MODEL // n_parts
    mix = None
    for p, o_ref in enumerate(o_refs):
        part = jnp.dot(o_ref[...], w_ref[p * pw:(p + 1) * pw, :], preferred_element_type=F32)
        mix = part if mix is None else mix + part
    y = _layer_norm(ALPHA * y_ref[...] + mod_ref[2] * mix, lng_ref[...], lnb_ref[...])
    y_out_ref[...] = y
    h = y * (1.0 + mod_ref[4]) + mod_ref[3]
    h_hi = h.astype(BF16)
    h_out_ref[...] = h_hi
    h_lo = (h - h_hi.astype(F32)).astype(BF16)
    logits = jnp.dot(h_hi, wr_hi_ref[...], preferred_element_type=F32)
    logits += jnp.dot(h_lo, wr_hi_ref[...], preferred_element_type=F32)
    logits += jnp.dot(h_hi, wr_lo_ref[...], preferred_element_type=F32)
    scores = 1.0 / (1.0 + jnp.exp(-logits))
    work = scores + br_ref[...]
    lane = lax.broadcasted_iota(jnp.int32, work.shape, 1).astype(F32)
    chosen = jnp.zeros(work.shape, F32)
    for _ in range(TOP_K):
        mx = work.max(axis=1, keepdims=True)
        first = jnp.where(work == mx, lane, float(N_EXPERTS)).min(axis=1, keepdims=True)
        pick = lane == first
        chosen = jnp.where(pick, scores, chosen)
        work = jnp.where(pick, NEG_BIG, work)
    gates_ref[...] = chosen / chosen.sum(axis=1, keepdims=True) * ROUTED_SCALE


def _post_mixer(o_parts, y2d, mod, w_out_bf16, ln_g, ln_b, wr_hi, wr_lo, b_router, *, latent):
    n = y2d.shape[0]
    n_parts = len(o_parts)
    pw = D_MODEL // n_parts
    per_batch = DEC_SEQ // TM
    mod_map = (lambda i: (0, 1 + i // per_batch, 0, 0)) if latent else (lambda i: (0, 0, 0, 0))
    o_specs = [pl.BlockSpec((TM, pw), functools.partial(lambda i, c: (i, c), c=col)) for _, col in o_parts]
    full = lambda i: (0, 0)
    return pl.pallas_call(
        functools.partial(_post_kernel, n_parts=n_parts),
        grid=(n // TM,),
        in_specs=o_specs + [
            pl.BlockSpec((TM, D_MODEL), lambda i: (i, 0)),
            pl.BlockSpec((6, None, 1, D_MODEL), mod_map),
            pl.BlockSpec((D_MODEL, D_MODEL), full),
            pl.BlockSpec((1, D_MODEL), full),
            pl.BlockSpec((1, D_MODEL), full),
            pl.BlockSpec((D_MODEL, N_EXPERTS), full),
            pl.BlockSpec((D_MODEL, N_EXPERTS), full),
            pl.BlockSpec((1, N_EXPERTS), full),
        ],
        out_specs=[
            pl.BlockSpec((TM, D_MODEL), lambda i: (i, 0)),
            pl.BlockSpec((TM, D_MODEL), lambda i: (i, 0)),
            pl.BlockSpec((TM, N_EXPERTS), lambda i: (i, 0)),
        ],
        out_shape=[
            jax.ShapeDtypeStruct((n, D_MODEL), F32),
            jax.ShapeDtypeStruct((n, D_MODEL), BF16),
            jax.ShapeDtypeStruct((n, N_EXPERTS), F32),
        ],
        compiler_params=_cparams(("parallel",)),
        name="post_mixer",
    )(*[a for a, _ in o_parts], y2d, mod, w_out_bf16, ln_g, ln_b, wr_hi, wr_lo, b_router)


def _silu(x):
    return x / (1.0 + jnp.exp(-x))


def _moe_kernel(x_ref, gates_ref, y_ref, mod_ref, wg_ref, wu_ref, wd_ref, sg_ref, su_ref, sd_ref,
                lng_ref, lnb_ref, o_ref, acc_ref):
    e = pl.program_id(1)
    x = x_ref[...]

    @pl.when(e == 0)
    def _():
        a = _silu(jnp.dot(x, sg_ref[...].astype(BF16), preferred_element_type=F32))
        a = a * jnp.dot(x, su_ref[...].astype(BF16), preferred_element_type=F32)
        acc_ref[...] = jnp.dot(a.astype(BF16), sd_ref[...].astype(BF16), preferred_element_type=F32)

    gates = gates_ref[...]
    lane = lax.broadcasted_iota(jnp.int32, gates.shape, 1)
    for pair in range(EXPERTS_PER_STEP // 2):
        acts = []
        for j in (2 * pair, 2 * pair + 1):
            w1 = jnp.concatenate([wg_ref[j].astype(BF16), wu_ref[j].astype(BF16)], axis=1)
            hcat = jnp.dot(x, w1, preferred_element_type=F32)
            ge = jnp.sum(jnp.where(lane == e * EXPERTS_PER_STEP + j, gates, 0.0), axis=1, keepdims=True)
            acts.append((_silu(hcat[:, :D_EXPERT]) * hcat[:, D_EXPERT:] * ge).astype(BF16))
        w2 = jnp.concatenate([wd_ref[2 * pair].astype(BF16), wd_ref[2 * pair + 1].astype(BF16)], axis=0)
        acc_ref[...] += jnp.dot(jnp.concatenate(acts, axis=1), w2, preferred_element_type=F32)

    @pl.when(e == pl.num_programs(1) - 1)
    def _():
        o_ref[...] = _layer_norm(ALPHA * y_ref[...] + mod_ref[5] * acc_ref[...], lng_ref[...], lnb_ref[...])


def _moe(x_bf16, gates, y2d, mod, w_gate, w_up, w_down, ws_gate, ws_up, ws_down, ln_g, ln_b, *, latent):
    n = x_bf16.shape[0]
    per_batch = DEC_SEQ // TM_MOE
    mod_map = (lambda i, e: (0, 1 + i // per_batch, 0, 0)) if latent else (lambda i, e: (0, 0, 0, 0))
    ep = EXPERTS_PER_STEP
    tok = lambda i, e: (i, 0)
    full = lambda i, e: (0, 0)
    return pl.pallas_call(
        _moe_kernel,
        grid=(n // TM_MOE, N_EXPERTS // ep),
        in_specs=[
            pl.BlockSpec((TM_MOE, D_MODEL), tok),
            pl.BlockSpec((TM_MOE, N_EXPERTS), tok),
            pl.BlockSpec((TM_MOE, D_MODEL), tok),
            pl.BlockSpec((6, None, 1, D_MODEL), mod_map),
            pl.BlockSpec((ep, D_MODEL, D_EXPERT), lambda i, e: (e, 0, 0)),
            pl.BlockSpec((ep, D_MODEL, D_EXPERT), lambda i, e: (e, 0, 0)),
            pl.BlockSpec((ep, D_EXPERT, D_MODEL), lambda i, e: (e, 0, 0)),
            pl.BlockSpec((D_MODEL, D_EXPERT), full),
            pl.BlockSpec((D_MODEL, D_EXPERT), full),
            pl.BlockSpec((D_EXPERT, D_MODEL), full),
            pl.BlockSpec((1, D_MODEL), full),
            pl.BlockSpec((1, D_MODEL), full),
        ],
        out_specs=pl.BlockSpec((TM_MOE, D_MODEL), tok),
        out_shape=jax.ShapeDtypeStruct((n, D_MODEL), F32),
        scratch_shapes=[pltpu.VMEM((TM_MOE, D_MODEL), F32)],
        compiler_params=_cparams(("parallel", "arbitrary")),
        name="moe",
    )(x_bf16, gates, y2d, mod, w_gate, w_up, w_down, ws_gate, ws_up, ws_down, ln_g, ln_b)


def _slab_perm():
    idx = []
    for j in range(4):
        for half in range(2):
            head = j + 4 * half
            idx.extend(range(head * HEAD_DIM, (head + 1) * HEAD_DIM))
    return np.asarray(idx, np.int32)


def _rope_tables():
    t = np.arange(DEC_SEQ)
    quarter = HEAD_DIM // 4
    inv = jnp.asarray(ROPE_THETA, F32) ** (-jnp.arange(quarter, dtype=F32) / quarter)
    ar = jnp.asarray(t // GRID_W, F32)[:, None] * inv
    ac = jnp.asarray(t % GRID_W, F32)[:, None] * inv
    ang = jnp.concatenate([ar, ar, ac, ac] * 2, axis=-1)
    sign = np.where((np.arange(LANES) % 32) < 16, -1.0, 1.0).astype(np.float32)
    return jnp.cos(ang), jnp.sin(ang) * sign


def kernel(x_prompt, x_sample, cache_ka, cache_va, cache_kb, cache_vb, cache_kc, cache_vc, c, c_ctx, w_ada, b_ada, ln_g, ln_b, w_in_even, w_out_even, qnorm_a, knorm_a, sink_b, w_in_odd, w_out_odd, rpb_c, w_router, b_router, w_gate, w_up, w_down, ws_gate, ws_up, ws_down):
    D = D_MODEL
    y_p = x_prompt.reshape(P_TOK, D)
    y_s = x_sample.reshape(S_TOK, D)

    cond8 = jnp.concatenate([c_ctx[None, :], c, jnp.zeros((8 - 1 - DEC_BATCH, D), F32)], axis=0)
    ada = _adaln(cond8, w_ada, b_ada)
    mods = ada.reshape(DEPTH, 8, 6, 1, D).transpose(0, 2, 1, 3, 4)

    cos, sin_signed = _rope_tables()
    perm = _slab_perm()
    in_perm = np.concatenate([perm, 512 + perm, np.arange(1024, EVEN_IN)])
    out_perm = np.concatenate([perm, 512 + perm])
    ones_row = jnp.ones((1, LANES), F32)
    no_sink = jnp.zeros((N_HEADS_B,), F32)

    new = {}
    for l in range(DEPTH):
        mod = mods[l]
        if l % 2 == 0:
            e = l // 2
            w_in = w_in_even[e][:, in_perm].astype(BF16)
            w_out = w_out_even[e][out_perm, :].astype(BF16)
            qn2 = jnp.tile(qnorm_a[e], 2)[None, :]
            kn2 = jnp.tile(knorm_a[e], 2)[None, :]
            proj = functools.partial(_project, w_bf16=w_in, qn2=qn2, kn2=kn2, cos=cos, sin_signed=sin_signed,
                                     dq=1024, dk=256, dv=256, q_norm=512, k_norm=128)
            q_p, k_p, v_p = proj(y_p, mod, rope=False, latent=False)
            q_s, k_s, v_s = proj(y_s, mod, rope=True, latent=True)
            new["ka"], new["kb"] = k_p[:, :LANES], k_p[:, LANES:]
            new["va"], new["vb"] = v_p[:, :LANES], v_p[:, LANES:]
            sink = sink_b[e].astype(F32)
            sink_heads = tuple((None, None) for _ in range(4)) + tuple((j, j + 4) for j in range(4))
            o_p = _ctx_attention(q_p, k_p, v_p, sink, kslab=(0,) * 4 + (1,) * 4, sink_heads=sink_heads)

            k_s3 = k_s.reshape(DEC_BATCH, DEC_SEQ, 2 * LANES)
            v_s3 = v_s.reshape(DEC_BATCH, DEC_SEQ, 2 * LANES)
            cka = cache_ka[:, e].reshape(DEC_BATCH, PAST_LEN, LANES).astype(BF16)
            cva = cache_va[:, e].reshape(DEC_BATCH, PAST_LEN, LANES).astype(BF16)
            ckb = cache_kb[:, e].reshape(DEC_BATCH, PAST_LEN, LANES).astype(BF16)
            cvb = cache_vb[:, e].reshape(DEC_BATCH, PAST_LEN, LANES).astype(BF16)
            pad = jnp.zeros((DEC_BATCH, WINDOW, LANES), BF16)
            o_a = _global_attention(q_s, jnp.concatenate([cka, k_s3[:, :, :LANES]], axis=1),
                                    jnp.concatenate([cva, v_s3[:, :, :LANES]], axis=1))
            o_b = _window_attention(q_s, jnp.concatenate([ckb, k_s3[:, :, LANES:], pad], axis=1),
                                    jnp.concatenate([cvb, v_s3[:, :, LANES:], pad], axis=1), sink)
            parts_p = [(o_p, 0), (o_p, 1)]
            parts_s = [(o_a, 0), (o_b, 0)]
        else:
            o = l // 2
            w_in = w_in_odd[o].astype(BF16)
            w_out = w_out_odd[o].astype(BF16)
            proj = functools.partial(_project, w_bf16=w_in, qn2=ones_row, kn2=ones_row, cos=cos,
                                     sin_signed=sin_signed, dq=1024, dk=1024, dv=1024, q_norm=0, k_norm=0,
                                     rope=False)
            q_p, k_p, v_p = proj(y_p, mod, latent=False)
            q_s, k_s, v_s = proj(y_s, mod, latent=True)
            new["kc"], new["vc"] = k_p, v_p
            o_p = _ctx_attention(q_p, k_p, v_p, no_sink, kslab=tuple(range(8)),
                                 sink_heads=tuple((None, None) for _ in range(8)))
            kc = cache_kc[:, o].reshape(DEC_BATCH, PAST_LEN, D).astype(BF16)
            vc = cache_vc[:, o].reshape(DEC_BATCH, PAST_LEN, D).astype(BF16)
            o_s = _na_attention(q_s, k_s, v_s, kc, vc, _na_bias_table(rpb_c[o]))
            parts_p = [(o_p, 0)]
            parts_s = [(o_s, 0)]

        wr = w_router[l]
        wr_hi = wr.astype(BF16)
        wr_lo = (wr - wr_hi.astype(F32)).astype(BF16)
        post = functools.partial(_post_mixer, mod=mod, w_out_bf16=w_out, ln_g=ln_g[l, 0][None, :],
                                 ln_b=ln_b[l, 0][None, :], wr_hi=wr_hi, wr_lo=wr_lo, b_router=b_router[l][None, :])
        y_p, h_p, g_p = post(parts_p, y_p, latent=False)
        y_s, h_s, g_s = post(parts_s, y_s, latent=True)
        moe = functools.partial(_moe, mod=mod, w_gate=w_gate[l], w_up=w_up[l], w_down=w_down[l],
                                ws_gate=ws_gate[l], ws_up=ws_up[l], ws_down=ws_down[l],
                                ln_g=ln_g[l, 1][None, :], ln_b=ln_b[l, 1][None, :])
        y_p = moe(h_p, g_p, y_p, latent=False)
        y_s = moe(h_s, g_s, y_s, latent=True)

    kv_a = (BATCH, 1, SEQ, N_KV_A, HEAD_DIM)
    kv_c = (BATCH, 1, SEQ, N_HEADS_C, HEAD_DIM)
    return (y_p.reshape(BATCH, SEQ, D), y_s.reshape(DEC_BATCH, DEC_SEQ, D),
            new["ka"].reshape(kv_a), new["va"].reshape(kv_a), new["kb"].reshape(kv_a), new["vb"].reshape(kv_a),
            new["kc"].reshape(kv_c), new["vc"].reshape(kv_c))
```

```python
import functools
import math

import numpy as np
import jax
import jax.numpy as jnp
from jax import lax
from jax.experimental import pallas as pl
from jax.experimental.pallas import tpu as pltpu

F32 = jnp.float32
BF16 = jnp.bfloat16

D_MODEL = 1024
BATCH = 16
SEQ = 256
DEPTH = 2
DEC_BATCH = 2
DEC_SEQ = 4096
PAST_LEN = 512
GRID_W = 64
HEAD_DIM = 64
N_HEADS_A = 8
N_KV_A = 2
N_HEADS_B = 8
N_KV_B = 2
N_HEADS_C = 16
EVEN_IN = 1536
ODD_IN = 3072
WINDOW = 128
NA_ROWS = 8
NA_COLS = 16
ROPE_THETA = 10000.0
N_EXPERTS = 64
TOP_K = 8
D_EXPERT = 128
ROUTED_SCALE = 2.5
ALPHA = (2 * DEPTH) ** 0.25
LN_EPS = 1e-6
RMS_EPS = 1e-6
NEG_BIG = -1e30
LOG2E = math.log2(math.e)

LANES = 128
P_TOK = BATCH * SEQ
S_TOK = DEC_BATCH * DEC_SEQ
TM = 512
TM_MOE = 1024
EXPERTS_PER_STEP = 4
NA_QROWS = 4
NA_KROWS = 12
VMEM_LIMIT = 56 * 1024 * 1024

_NT = (((1,), (1,)), ((), ()))


def _cparams(sem):
    return pltpu.CompilerParams(dimension_semantics=sem, vmem_limit_bytes=VMEM_LIMIT)


def _half_masks(dtype):
    lane = lax.broadcasted_iota(jnp.int32, (1, LANES), 1)
    lo = jnp.where(lane < HEAD_DIM, 1.0, 0.0).astype(dtype)
    hi = jnp.where(lane < HEAD_DIM, 0.0, 1.0).astype(dtype)
    return lo, hi


def _ada_kernel(c_ref, w_ref, b_ref, o_ref):
    c = c_ref[...]
    a = c / (1.0 + jnp.exp(-c))
    w = w_ref[0]
    a_hi = a.astype(BF16)
    a_lo = (a - a_hi.astype(F32)).astype(BF16)
    w_hi = w.astype(BF16)
    w_lo = (w - w_hi.astype(F32)).astype(BF16)
    acc = jnp.dot(a_hi, w_hi, preferred_element_type=F32)
    acc += jnp.dot(a_lo, w_hi, preferred_element_type=F32)
    acc += jnp.dot(a_hi, w_lo, preferred_element_type=F32)
    o_ref[0] = acc + b_ref[0]


def _adaln(cond8, w_ada, b_ada):
    tn = 1536
    return pl.pallas_call(
        _ada_kernel,
        grid=(DEPTH, 6 * D_MODEL // tn),
        in_specs=[
            pl.BlockSpec((8, D_MODEL), lambda l, j: (0, 0)),
            pl.BlockSpec((1, D_MODEL, tn), lambda l, j: (l, 0, j)),
            pl.BlockSpec((1, 1, tn), lambda l, j: (l, 0, j)),
        ],
        out_specs=pl.BlockSpec((1, 8, tn), lambda l, j: (l, 0, j)),
        out_shape=jax.ShapeDtypeStruct((DEPTH, 8, 6 * D_MODEL), F32),
        compiler_params=_cparams(("parallel", "parallel")),
        name="adaln",
    )(cond8, w_ada, b_ada.reshape(DEPTH, 1, 6 * D_MODEL))


def _group_sum_matrix():
    r = lax.broadcasted_iota(jnp.int32, (LANES, LANES), 0) // HEAD_DIM
    c = lax.broadcasted_iota(jnp.int32, (LANES, LANES), 1) // HEAD_DIM
    return jnp.where(r == c, 1.0, 0.0).astype(BF16)


def _rms_slab(t, g, gmat):
    sq = t * t
    hi = sq.astype(BF16)
    lo = (sq - hi.astype(F32)).astype(BF16)
    ss = jnp.dot(hi, gmat, preferred_element_type=F32) + jnp.dot(lo, gmat, preferred_element_type=F32)
    return t * lax.rsqrt(ss * (1.0 / HEAD_DIM) + RMS_EPS) * g


def _rope_slab(t, cos, sin_signed, first):
    r = jnp.where(first, pltpu.roll(t, LANES - 16, 1), pltpu.roll(t, 16, 1))
    return t * cos + r * sin_signed


def _proj_kernel(x_ref, mod_ref, w_ref, qn_ref, kn_ref, cos_ref, sin_ref, q_ref, k_ref, v_ref,
                 *, dq, dk, q_norm, k_norm, rope):
    x = x_ref[...]
    h = (x * (1.0 + mod_ref[1]) + mod_ref[0]).astype(BF16)
    y = jnp.dot(h, w_ref[...], preferred_element_type=F32)
    tm = x.shape[0]
    gmat = _group_sum_matrix() if (q_norm or k_norm) else None
    if rope:
        cos = cos_ref[...]
        sin_signed = sin_ref[...]
        first = (lax.broadcasted_iota(jnp.int32, (tm, LANES), 1) % 32) < 16
    for s in range(dq // LANES):
        t = y[:, s * LANES:(s + 1) * LANES]
        if s * LANES < q_norm:
            t = _rms_slab(t, qn_ref[...], gmat)
        if rope:
            t = _rope_slab(t, cos, sin_signed, first)
        q_ref[:, s * LANES:(s + 1) * LANES] = (t * (HEAD_DIM ** -0.5 * LOG2E)).astype(q_ref.dtype)
    for s in range(dk // LANES):
        t = y[:, dq + s * LANES:dq + (s + 1) * LANES]
        if s * LANES < k_norm:
            t = _rms_slab(t, kn_ref[...], gmat)
        if rope:
            t = _rope_slab(t, cos, sin_signed, first)
        k_ref[:, s * LANES:(s + 1) * LANES] = t.astype(k_ref.dtype)
    v_ref[...] = y[:, dq + dk:].astype(v_ref.dtype)


def _project(x2d, mod, w_bf16, qn2, kn2, cos, sin_signed, *, dq, dk, dv, q_norm, k_norm, rope, latent):
    n = x2d.shape[0]
    per_batch = DEC_SEQ // TM
    if latent:
        mod_map = lambda i: (0, 1 + i // per_batch, 0, 0)
        pos_map = lambda i: (i % per_batch, 0)
        kv_dtype = BF16
    else:
        mod_map = lambda i: (0, 0, 0, 0)
        pos_map = lambda i: (0, 0)
        kv_dtype = F32
    kern = functools.partial(_proj_kernel, dq=dq, dk=dk, q_norm=q_norm, k_norm=k_norm, rope=rope)
    return pl.pallas_call(
        kern,
        grid=(n // TM,),
        in_specs=[
            pl.BlockSpec((TM, D_MODEL), lambda i: (i, 0)),
            pl.BlockSpec((6, None, 1, D_MODEL), mod_map),
            pl.BlockSpec((D_MODEL, dq + dk + dv), lambda i: (0, 0)),
            pl.BlockSpec((1, LANES), lambda i: (0, 0)),
            pl.BlockSpec((1, LANES), lambda i: (0, 0)),
            pl.BlockSpec((TM, LANES), pos_map),
            pl.BlockSpec((TM, LANES), pos_map),
        ],
        out_specs=[
            pl.BlockSpec((TM, dq), lambda i: (i, 0)),
            pl.BlockSpec((TM, dk), lambda i: (i, 0)),
            pl.BlockSpec((TM, dv), lambda i: (i, 0)),
        ],
        out_shape=[
            jax.ShapeDtypeStruct((n, dq), BF16),
            jax.ShapeDtypeStruct((n, dk), kv_dtype),
            jax.ShapeDtypeStruct((n, dv), kv_dtype),
        ],
        compiler_params=_cparams(("parallel",)),
        name="in_proj",
    )(x2d, mod, w_bf16, qn2, kn2, cos, sin_signed)


def _stack_heads(q_ref, slabs, lo, hi):
    blocks = []
    for j in slabs:
        qs = q_ref[:, j * LANES:(j + 1) * LANES]
        blocks += [qs * lo, qs * hi]
    return jnp.concatenate(blocks, axis=0)


def _unstack_heads(o, o_ref, slabs, tq, lane_lo):
    for n, j in enumerate(slabs):
        merged = jnp.where(lane_lo, o[2 * n * tq:(2 * n + 1) * tq], o[(2 * n + 1) * tq:(2 * n + 2) * tq])
        o_ref[:, j * LANES:(j + 1) * LANES] = merged.astype(o_ref.dtype)


def _sink_column(sink_ref, heads, tq):
    return jnp.concatenate([jnp.full((tq, 1), sink_ref[h] * LOG2E, F32) for h in heads], axis=0)


def _softmax_pv(parts, sink):
    m = parts[0][0].max(axis=1, keepdims=True)
    for s, _ in parts[1:]:
        m = jnp.maximum(m, s.max(axis=1, keepdims=True))
    if sink is not None:
        m = jnp.maximum(m, sink)
    l = None
    o = None
    for s, v in parts:
        p = jnp.exp2(s - m)
        ps = p.sum(axis=1, keepdims=True)
        pv = jnp.dot(p.astype(BF16), v, preferred_element_type=F32)
        l = ps if l is None else l + ps
        o = pv if o is None else o + pv
    if sink is not None:
        l = l + jnp.exp2(sink - m)
    return o * (1.0 / l)


def _ctx_attn_kernel(sink_ref, q_ref, k_ref, v_ref, o_ref, *, groups):
    tq = q_ref.shape[0]
    lo, hi = _half_masks(BF16)
    lane_lo = lax.broadcasted_iota(jnp.int32, (tq, LANES), 1) < HEAD_DIM
    for slabs, ks, sink_heads in groups:
        kk = k_ref[:, ks * LANES:(ks + 1) * LANES].astype(BF16)
        vv = v_ref[:, ks * LANES:(ks + 1) * LANES].astype(BF16)
        s = lax.dot_general(_stack_heads(q_ref, slabs, lo, hi), kk, _NT, preferred_element_type=F32)
        sink = None if sink_heads is None else _sink_column(sink_ref, sink_heads, tq)
        _unstack_heads(_softmax_pv([(s, vv)], sink), o_ref, slabs, tq, lane_lo)


def _ctx_attention(q, k, v, sink, *, groups):
    kw = k.shape[1]
    return pl.pallas_call(
        functools.partial(_ctx_attn_kernel, groups=groups),
        grid=(BATCH,),
        in_specs=[
            pl.BlockSpec(memory_space=pltpu.SMEM),
            pl.BlockSpec((SEQ, D_MODEL), lambda b: (b, 0)),
            pl.BlockSpec((SEQ, kw), lambda b: (b, 0)),
            pl.BlockSpec((SEQ, kw), lambda b: (b, 0)),
        ],
        out_specs=pl.BlockSpec((SEQ, D_MODEL), lambda b: (b, 0)),
        out_shape=jax.ShapeDtypeStruct((P_TOK, D_MODEL), BF16),
        compiler_params=_cparams(("parallel",)),
        name="ctx_attn",
    )(sink, q, k, v)


def _global_attn_kernel(q_ref, k_ref, v_ref, o_ref, m_ref, l_ref, acc_ref, *, tk):
    tq = q_ref.shape[0]
    slabs = tuple(range(q_ref.shape[1] // LANES))
    nk = k_ref.shape[0] // tk
    lo, hi = _half_masks(BF16)
    lane_lo = lax.broadcasted_iota(jnp.int32, (tq, LANES), 1) < HEAD_DIM
    qstack = _stack_heads(q_ref, slabs, lo, hi)
    m_ref[...] = jnp.full(m_ref.shape, NEG_BIG, F32)
    l_ref[...] = jnp.zeros(l_ref.shape, F32)
    acc_ref[...] = jnp.zeros(acc_ref.shape, F32)

    def body(c, carry):
        off = pl.multiple_of(c * tk, tk)
        s = lax.dot_general(qstack, k_ref[pl.ds(off, tk), :], _NT, preferred_element_type=F32)
        m_old = m_ref[...]
        m_new = jnp.maximum(m_old, s.max(axis=1, keepdims=True))
        alpha = jnp.exp2(m_old - m_new)
        p = jnp.exp2(s - m_new)
        l_ref[...] = alpha * l_ref[...] + p.sum(axis=1, keepdims=True)
        pv = jnp.dot(p.astype(BF16), v_ref[pl.ds(off, tk), :], preferred_element_type=F32)
        acc_ref[...] = alpha * acc_ref[...] + pv
        m_ref[...] = m_new
        return carry

    lax.fori_loop(0, nk, body, 0)
    _unstack_heads(acc_ref[...] * (1.0 / l_ref[...]), o_ref, slabs, tq, lane_lo)


def _global_attention(q, kcat, vcat):
    tq, tk = 256, 512
    t = kcat.shape[1]
    per_batch = DEC_SEQ // tq
    rows = N_HEADS_A * tq
    return pl.pallas_call(
        functools.partial(_global_attn_kernel, tk=tk),
        grid=(DEC_BATCH, per_batch),
        in_specs=[
            pl.BlockSpec((tq, 4 * LANES), lambda b, i: (b * per_batch + i, 0)),
            pl.BlockSpec((None, t, LANES), lambda b, i: (b, 0, 0)),
            pl.BlockSpec((None, t, LANES), lambda b, i: (b, 0, 0)),
        ],
        out_specs=pl.BlockSpec((tq, 4 * LANES), lambda b, i: (b * per_batch + i, 0)),
        out_shape=jax.ShapeDtypeStruct((S_TOK, 4 * LANES), BF16),
        scratch_shapes=[pltpu.VMEM((rows, 1), F32), pltpu.VMEM((rows, 1), F32), pltpu.VMEM((rows, LANES), F32)],
        compiler_params=_cparams(("parallel", "parallel")),
        name="global_attn",
    )(q, kcat, vcat)


def _window_attn_kernel(sink_ref, q_ref, k_ref, v_ref, o_ref):
    i = pl.program_id(1)
    tq = q_ref.shape[0]
    slabs = tuple(range(q_ref.shape[1] // LANES))
    n_heads = 2 * len(slabs)
    span = tq + 2 * WINDOW
    lo, hi = _half_masks(BF16)
    lane_lo = lax.broadcasted_iota(jnp.int32, (tq, LANES), 1) < HEAD_DIM
    off = pl.multiple_of(PAST_LEN - WINDOW + i * tq, LANES)
    k_loc = k_ref[pl.ds(off, span), :]
    v_loc = v_ref[pl.ds(off, span), :]
    row = lax.broadcasted_iota(jnp.int32, (tq, span), 0)
    col = lax.broadcasted_iota(jnp.int32, (tq, span), 1)
    kpos = i * tq - WINDOW + col
    valid = (col >= row) & (col <= row + 2 * WINDOW) & (kpos >= 0) & (kpos < DEC_SEQ)
    qstack = _stack_heads(q_ref, slabs, lo, hi)
    s_loc = lax.dot_general(qstack, k_loc, _NT, preferred_element_type=F32)
    s_loc = jnp.where(valid[None], s_loc.reshape(n_heads, tq, span), NEG_BIG).reshape(n_heads * tq, span)
    s_ctx = lax.dot_general(qstack, k_ref[0:PAST_LEN, :], _NT, preferred_element_type=F32)
    sink = _sink_column(sink_ref, [j + 4 * half for j in slabs for half in range(2)], tq)
    o = _softmax_pv([(s_loc, v_loc), (s_ctx, v_ref[0:PAST_LEN, :])], sink)
    _unstack_heads(o, o_ref, slabs, tq, lane_lo)


def _window_attention(q, kcat, vcat, sink):
    tq = 128
    t = kcat.shape[1]
    per_batch = DEC_SEQ // tq
    return pl.pallas_call(
        _window_attn_kernel,
        grid=(DEC_BATCH, per_batch),
        in_specs=[
            pl.BlockSpec(memory_space=pltpu.SMEM),
            pl.BlockSpec((tq, 4 * LANES), lambda b, i: (b * per_batch + i, 1)),
            pl.BlockSpec((None, t, LANES), lambda b, i: (b, 0, 0)),
            pl.BlockSpec((None, t, LANES), lambda b, i: (b, 0, 0)),
        ],
        out_specs=pl.BlockSpec((tq, 4 * LANES), lambda b, i: (b * per_batch + i, 0)),
        out_shape=jax.ShapeDtypeStruct((S_TOK, 4 * LANES), BF16),
        compiler_params=_cparams(("parallel", "parallel")),
        name="window_attn",
    )(sink, q, kcat, vcat)


def _na_start_row(g):
    return jnp.clip(NA_QROWS * g - NA_ROWS // 2, 0, DEC_SEQ // GRID_W - NA_KROWS)


def _na_step_bias(t_ref, half, g, lane_lo64):
    rows = DEC_SEQ // GRID_W
    start = _na_start_row(g)
    d0 = start - NA_QROWS * g + NA_ROWS - 1
    row_blocks = []
    for qr in range(NA_QROWS):
        rs = jnp.clip(NA_QROWS * g + qr - NA_ROWS // 2, 0, rows - NA_ROWS)
        blocks = []
        for m in range(NA_KROWS // 2):
            tiles = []
            for kr in (2 * m, 2 * m + 1):
                krow = start + kr
                inside = (krow >= rs) & (krow < rs + NA_ROWS)
                dr = jnp.clip(kr - qr + d0, 0, 2 * NA_ROWS - 2)
                tiles.append(t_ref[half, dr] + jnp.where(inside, 0.0, NEG_BIG))
            blocks.append(jnp.where(lane_lo64, tiles[0], tiles[1]))
        row_blocks.append(jnp.concatenate(blocks, axis=1))
    return jnp.concatenate(row_blocks, axis=0)


def _na_attn_kernel(q_ref, k_ref, v_ref, kc_ref, vc_ref, t_ref, o_ref):
    g = pl.program_id(2)
    tq = q_ref.shape[0]
    lo, hi = _half_masks(BF16)
    lane_lo = lax.broadcasted_iota(jnp.int32, (tq, LANES), 1) < HEAD_DIM
    lane_lo64 = lax.broadcasted_iota(jnp.int32, (GRID_W, LANES), 1) < HEAD_DIM
    off = pl.multiple_of(_na_start_row(g) * GRID_W, GRID_W)
    k_loc = k_ref[pl.ds(off, NA_KROWS * GRID_W), :]
    v_loc = v_ref[pl.ds(off, NA_KROWS * GRID_W), :]
    qstack = _stack_heads(q_ref, (0,), lo, hi)
    bias = jnp.concatenate([_na_step_bias(t_ref, half, g, lane_lo64) for half in range(2)], axis=0)
    s_loc = lax.dot_general(qstack, k_loc, _NT, preferred_element_type=F32) + bias
    s_ctx = lax.dot_general(qstack, kc_ref[...], _NT, preferred_element_type=F32)
    o = _softmax_pv([(s_loc, v_loc), (s_ctx, vc_ref[...])], None)
    _unstack_heads(o, o_ref, (0,), tq, lane_lo)


def _na_bias_table(rpb):
    cols = np.arange(GRID_W)
    cs = np.clip(cols - NA_COLS // 2, 0, GRID_W - NA_COLS)
    dc = cols[None, :] - cols[:, None] + NA_COLS - 1
    inside = (cols[None, :] >= cs[:, None]) & (cols[None, :] < cs[:, None] + NA_COLS)
    onehot = (np.arange(2 * NA_COLS - 1)[:, None, None] == dc[None]) & inside[None]
    t = jnp.einsum("hdc,cqk->hdqk", rpb.astype(F32), jnp.asarray(onehot, F32), precision=lax.Precision.HIGHEST)
    t = jnp.where(inside[None, None], t * LOG2E, NEG_BIG)
    return jnp.concatenate([t, t], axis=-1)


def _na_attention(q, k, v, kc, vc, bias):
    tq = NA_QROWS * GRID_W
    n_groups = DEC_SEQ // tq
    n_slabs = D_MODEL // LANES
    k3 = k.reshape(DEC_BATCH, DEC_SEQ, D_MODEL)
    v3 = v.reshape(DEC_BATCH, DEC_SEQ, D_MODEL)
    return pl.pallas_call(
        _na_attn_kernel,
        grid=(n_slabs, DEC_BATCH, n_groups),
        in_specs=[
            pl.BlockSpec((tq, LANES), lambda s, b, g: (b * n_groups + g, s)),
            pl.BlockSpec((None, DEC_SEQ, LANES), lambda s, b, g: (b, 0, s)),
            pl.BlockSpec((None, DEC_SEQ, LANES), lambda s, b, g: (b, 0, s)),
            pl.BlockSpec((None, PAST_LEN, LANES), lambda s, b, g: (b, 0, s)),
            pl.BlockSpec((None, PAST_LEN, LANES), lambda s, b, g: (b, 0, s)),
            pl.BlockSpec((2, 2 * NA_ROWS - 1, GRID_W, LANES), lambda s, b, g: (s, 0, 0, 0)),
        ],
        out_specs=pl.BlockSpec((tq, LANES), lambda s, b, g: (b * n_groups + g, s)),
        out_shape=jax.ShapeDtypeStruct((S_TOK, D_MODEL), BF16),
        compiler_params=_cparams(("parallel", "parallel", "parallel")),
        name="na_attn",
    )(q, k3, v3, kc, vc, bias)


def _layer_norm(z, g, b):
    mu = jnp.mean(z, axis=-1, keepdims=True)
    zc = z - mu
    var = jnp.mean(zc * zc, axis=-1, keepdims=True)
    return zc * lax.rsqrt(var + LN_EPS) * g + b


def _post_kernel(*refs, n_parts):
    o_refs = refs[:n_parts]
    (y_ref, mod_ref, w_ref, lng_ref, lnb_ref, wr_hi_ref, wr_lo_ref, br_ref,
     y_out_ref, h_out_ref, gates_ref) = refs[n_parts:]
    pw = D_MODEL // n_parts
    mix = None
    for p, o_ref in enumerate(o_refs):
        part = jnp.dot(o_ref[...], w_ref[p * pw:(p + 1) * pw, :], preferred_element_type=F32)
        mix = part if mix is None else mix + part
    y = _layer_norm(ALPHA * y_ref[...] + mod_ref[2] * mix, lng_ref[...], lnb_ref[...])
    y_out_ref[...] = y
    h = y * (1.0 + mod_ref[4]) + mod_ref[3]
    h_hi = h.astype(BF16)
    h_out_ref[...] = h_hi
    h_lo = (h - h_hi.astype(F32)).astype(BF16)
    logits = jnp.dot(h_hi, wr_hi_ref[...], preferred_element_type=F32)
    logits += jnp.dot(h_lo, wr_hi_ref[...], preferred_element_type=F32)
    logits += jnp.dot(h_hi, wr_lo_ref[...], preferred_element_type=F32)
    scores = 1.0 / (1.0 + jnp.exp(-logits))
    work = scores + br_ref[...]
    lane = lax.broadcasted_iota(jnp.int32, work.shape, 1).astype(F32)
    chosen = jnp.zeros(work.shape, F32)
    for _ in range(TOP_K):
        mx = work.max(axis=1, keepdims=True)
        first = jnp.where(work == mx, lane, float(N_EXPERTS)).min(axis=1, keepdims=True)
        pick = lane == first
        chosen = jnp.where(pick, scores, chosen)
        work = jnp.where(pick, NEG_BIG, work)
    gates_ref[...] = chosen / chosen.sum(axis=1, keepdims=True) * ROUTED_SCALE


def _post_mixer(o_parts, y2d, mod, w_out_bf16, ln_g, ln_b, wr_hi, wr_lo, b_router, *, latent):
    n = y2d.shape[0]
    n_parts = len(o_parts)
    pw = D_MODEL // n_parts
    per_batch = DEC_SEQ // TM
    mod_map = (lambda i: (0, 1 + i // per_batch, 0, 0)) if latent else (lambda i: (0, 0, 0, 0))
    o_specs = [pl.BlockSpec((TM, pw), functools.partial(lambda i, c: (i, c), c=col)) for _, col in o_parts]
    full = lambda i: (0, 0)
    return pl.pallas_call(
        functools.partial(_post_kernel, n_parts=n_parts),
        grid=(n // TM,),
        in_specs=o_specs + [
            pl.BlockSpec((TM, D_MODEL), lambda i: (i, 0)),
            pl.BlockSpec((6, None, 1, D_MODEL), mod_map),
            pl.BlockSpec((D_MODEL, D_MODEL), full),
            pl.BlockSpec((1, D_MODEL), full),
            pl.BlockSpec((1, D_MODEL), full),
            pl.BlockSpec((D_MODEL, N_EXPERTS), full),
            pl.BlockSpec((D_MODEL, N_EXPERTS), full),
            pl.BlockSpec((1, N_EXPERTS), full),
        ],
        out_specs=[
            pl.BlockSpec((TM, D_MODEL), lambda i: (i, 0)),
            pl.BlockSpec((TM, D_MODEL), lambda i: (i, 0)),
            pl.BlockSpec((TM, N_EXPERTS), lambda i: (i, 0)),
        ],
        out_shape=[
            jax.ShapeDtypeStruct((n, D_MODEL), F32),
            jax.ShapeDtypeStruct((n, D_MODEL), BF16),
            jax.ShapeDtypeStruct((n, N_EXPERTS), F32),
        ],
        compiler_params=_cparams(("parallel",)),
        name="post_mixer",
    )(*[a for a, _ in o_parts], y2d, mod, w_out_bf16, ln_g, ln_b, wr_hi, wr_lo, b_router)


def _silu(x):
    return x / (1.0 + jnp.exp(-x))


def _moe_kernel(x_ref, gates_ref, y_ref, mod_ref, wg_ref, wu_ref, wd_ref, sg_ref, su_ref, sd_ref,
                lng_ref, lnb_ref, o_ref, acc_ref):
    e = pl.program_id(1)
    x = x_ref[...]

    @pl.when(e == 0)
    def _():
        a = _silu(jnp.dot(x, sg_ref[...].astype(BF16), preferred_element_type=F32))
        a = a * jnp.dot(x, su_ref[...].astype(BF16), preferred_element_type=F32)
        acc_ref[...] = jnp.dot(a.astype(BF16), sd_ref[...].astype(BF16), preferred_element_type=F32)

    gates = gates_ref[...]
    lane = lax.broadcasted_iota(jnp.int32, gates.shape, 1)
    for pair in range(EXPERTS_PER_STEP // 2):
        acts = []
        for j in (2 * pair, 2 * pair + 1):
            w1 = jnp.concatenate([wg_ref[j].astype(BF16), wu_ref[j].astype(BF16)], axis=1)
            hcat = jnp.dot(x, w1, preferred_element_type=F32)
            ge = jnp.sum(jnp.where(lane == e * EXPERTS_PER_STEP + j, gates, 0.0), axis=1, keepdims=True)
            acts.append((_silu(hcat[:, :D_EXPERT]) * hcat[:, D_EXPERT:] * ge).astype(BF16))
        w2 = jnp.concatenate([wd_ref[2 * pair].astype(BF16), wd_ref[2 * pair + 1].astype(BF16)], axis=0)
        acc_ref[...] += jnp.dot(jnp.concatenate(acts, axis=1), w2, preferred_element_type=F32)

    @pl.when(e == pl.num_programs(1) - 1)
    def _():
        o_ref[...] = _layer_norm(ALPHA * y_ref[...] + mod_ref[5] * acc_ref[...], lng_ref[...], lnb_ref[...])


def _moe(x_bf16, gates, y2d, mod, w_gate, w_up, w_down, ws_gate, ws_up, ws_down, ln_g, ln_b, *, latent):
    n = x_bf16.shape[0]
    per_batch = DEC_SEQ // TM_MOE
    mod_map = (lambda i, e: (0, 1 + i // per_batch, 0, 0)) if latent else (lambda i, e: (0, 0, 0, 0))
    ep = EXPERTS_PER_STEP
    tok = lambda i, e: (i, 0)
    full = lambda i, e: (0, 0)
    return pl.pallas_call(
        _moe_kernel,
        grid=(n // TM_MOE, N_EXPERTS // ep),
        in_specs=[
            pl.BlockSpec((TM_MOE, D_MODEL), tok),
            pl.BlockSpec((TM_MOE, N_EXPERTS), tok),
            pl.BlockSpec((TM_MOE, D_MODEL), tok),
            pl.BlockSpec((6, None, 1, D_MODEL), mod_map),
            pl.BlockSpec((ep, D_MODEL, D_EXPERT), lambda i, e: (e, 0, 0)),
            pl.BlockSpec((ep, D_MODEL, D_EXPERT), lambda i, e: (e, 0, 0)),
            pl.BlockSpec((ep, D_EXPERT, D_MODEL), lambda i, e: (e, 0, 0)),
            pl.BlockSpec((D_MODEL, D_EXPERT), full),
            pl.BlockSpec((D_MODEL, D_EXPERT), full),
            pl.BlockSpec((D_EXPERT, D_MODEL), full),
            pl.BlockSpec((1, D_MODEL), full),
            pl.BlockSpec((1, D_MODEL), full),
        ],
        out_specs=pl.BlockSpec((TM_MOE, D_MODEL), tok),
        out_shape=jax.ShapeDtypeStruct((n, D_MODEL), F32),
        scratch_shapes=[pltpu.VMEM((TM_MOE, D_MODEL), F32)],
        compiler_params=_cparams(("parallel", "arbitrary")),
        name="moe",
    )(x_bf16, gates, y2d, mod, w_gate, w_up, w_down, ws_gate, ws_up, ws_down, ln_g, ln_b)


def _slab_perm():
    idx = []
    for j in range(4):
        for half in range(2):
            head = j + 4 * half
            idx.extend(range(head * HEAD_DIM, (head + 1) * HEAD_DIM))
    return np.asarray(idx, np.int32)


def _rope_tables():
    t = np.arange(DEC_SEQ)
    quarter = HEAD_DIM // 4
    inv = jnp.asarray(ROPE_THETA, F32) ** (-jnp.arange(quarter, dtype=F32) / quarter)
    ar = jnp.asarray(t // GRID_W, F32)[:, None] * inv
    ac = jnp.asarray(t % GRID_W, F32)[:, None] * inv
    ang = jnp.concatenate([ar, ar, ac, ac] * 2, axis=-1)
    sign = np.where((np.arange(LANES) % 32) < 16, -1.0, 1.0).astype(np.float32)
    return jnp.cos(ang), jnp.sin(ang) * sign


def kernel(x_prompt, x_sample, cache_ka, cache_va, cache_kb, cache_vb, cache_kc, cache_vc, c, c_ctx, w_ada, b_ada, ln_g, ln_b, w_in_even, w_out_even, qnorm_a, knorm_a, sink_b, w_in_odd, w_out_odd, rpb_c, w_router, b_router, w_gate, w_up, w_down, ws_gate, ws_up, ws_down):
    D = D_MODEL
    y_p = x_prompt.reshape(P_TOK, D)
    y_s = x_sample.reshape(S_TOK, D)

    cond8 = jnp.concatenate([c_ctx[None, :], c, jnp.zeros((8 - 1 - DEC_BATCH, D), F32)], axis=0)
    ada = _adaln(cond8, w_ada, b_ada)
    mods = ada.reshape(DEPTH, 8, 6, 1, D).transpose(0, 2, 1, 3, 4)

    cos, sin_signed = _rope_tables()
    perm = _slab_perm()
    in_perm = np.concatenate([perm, 512 + perm, np.arange(1024, EVEN_IN)])
    out_perm = np.concatenate([perm, 512 + perm])
    ones_row = jnp.ones((1, LANES), F32)
    no_sink = jnp.zeros((N_HEADS_B,), F32)

    new = {}
    for l in range(DEPTH):
        mod = mods[l]
        if l % 2 == 0:
            e = l // 2
            w_in = w_in_even[e][:, in_perm].astype(BF16)
            w_out = w_out_even[e][out_perm, :].astype(BF16)
            qn2 = jnp.tile(qnorm_a[e], 2)[None, :]
            kn2 = jnp.tile(knorm_a[e], 2)[None, :]
            proj = functools.partial(_project, w_bf16=w_in, qn2=qn2, kn2=kn2, cos=cos, sin_signed=sin_signed,
                                     dq=1024, dk=256, dv=256, q_norm=512, k_norm=128)
            q_p, k_p, v_p = proj(y_p, mod, rope=False, latent=False)
            q_s, k_s, v_s = proj(y_s, mod, rope=True, latent=True)
            new["ka"], new["kb"] = k_p[:, :LANES], k_p[:, LANES:]
            new["va"], new["vb"] = v_p[:, :LANES], v_p[:, LANES:]
            sink = sink_b[e].astype(F32)
            b_heads = tuple(j + 4 * half for j in range(4) for half in range(2))
            groups = (((0, 1, 2, 3), 0, None), ((4, 5, 6, 7), 1, b_heads))
            o_p = _ctx_attention(q_p, k_p, v_p, sink, groups=groups)

            k_s3 = k_s.reshape(DEC_BATCH, DEC_SEQ, 2 * LANES)
            v_s3 = v_s.reshape(DEC_BATCH, DEC_SEQ, 2 * LANES)
            cka = cache_ka[:, e].reshape(DEC_BATCH, PAST_LEN, LANES).astype(BF16)
            cva = cache_va[:, e].reshape(DEC_BATCH, PAST_LEN, LANES).astype(BF16)
            ckb = cache_kb[:, e].reshape(DEC_BATCH, PAST_LEN, LANES).astype(BF16)
            cvb = cache_vb[:, e].reshape(DEC_BATCH, PAST_LEN, LANES).astype(BF16)
            pad = jnp.zeros((DEC_BATCH, WINDOW, LANES), BF16)
            o_a = _global_attention(q_s, jnp.concatenate([cka, k_s3[:, :, :LANES]], axis=1),
                                    jnp.concatenate([cva, v_s3[:, :, :LANES]], axis=1))
            o_b = _window_attention(q_s, jnp.concatenate([ckb, k_s3[:, :, LANES:], pad], axis=1),
                                    jnp.concatenate([cvb, v_s3[:, :, LANES:], pad], axis=1), sink)
            parts_p = [(o_p, 0), (o_p, 1)]
            parts_s = [(o_a, 0), (o_b, 0)]
        else:
            o = l // 2
            w_in = w_in_odd[o].astype(BF16)
            w_out = w_out_odd[o].astype(BF16)
            proj = functools.partial(_project, w_bf16=w_in, qn2=ones_row, kn2=ones_row, cos=cos,
                                     sin_signed=sin_signed, dq=1024, dk=1024, dv=1024, q_norm=0, k_norm=0,
                                     rope=False)
            q_p, k_p, v_p = proj(y_p, mod, latent=False)
            q_s, k_s, v_s = proj(y_s, mod, latent=True)
            new["kc"], new["vc"] = k_p, v_p
            groups = tuple(((j,), j, None) for j in range(D // LANES))
            o_p = _ctx_attention(q_p, k_p, v_p, no_sink, groups=groups)
            kc = cache_kc[:, o].reshape(DEC_BATCH, PAST_LEN, D).astype(BF16)
            vc = cache_vc[:, o].reshape(DEC_BATCH, PAST_LEN, D).astype(BF16)
            o_s = _na_attention(q_s, k_s, v_s, kc, vc, _na_bias_table(rpb_c[o]))
            parts_p = [(o_p, 0)]
            parts_s = [(o_s, 0)]

        wr = w_router[l]
        wr_hi = wr.astype(BF16)
        wr_lo = (wr - wr_hi.astype(F32)).astype(BF16)
        post = functools.partial(_post_mixer, mod=mod, w_out_bf16=w_out, ln_g=ln_g[l, 0][None, :],
                                 ln_b=ln_b[l, 0][None, :], wr_hi=wr_hi, wr_lo=wr_lo, b_router=b_router[l][None, :])
        y_p, h_p, g_p = post(parts_p, y_p, latent=False)
        y_s, h_s, g_s = post(parts_s, y_s, latent=True)
        moe = functools.partial(_moe, mod=mod, w_gate=w_gate[l], w_up=w_up[l], w_down=w_down[l],
                                ws_gate=ws_gate[l], ws_up=ws_up[l], ws_down=ws_down[l],
                                ln_g=ln_g[l, 1][None, :], ln_b=ln_b[l, 1][None, :])
        y_p = moe(h_p, g_p, y_p, latent=False)
        y_s = moe(h_s, g_s, y_s, latent=True)

    kv_a = (BATCH, 1, SEQ, N_KV_A, HEAD_DIM)
    kv_c = (BATCH, 1, SEQ, N_HEADS_C, HEAD_DIM)
    return (y_p.reshape(BATCH, SEQ, D), y_s.reshape(DEC_BATCH, DEC_SEQ, D),
            new["ka"].reshape(kv_a), new["va"].reshape(kv_a), new["kb"].reshape(kv_a), new["vb"].reshape(kv_a),
            new["kc"].reshape(kv_c), new["vc"].reshape(kv_c))
```

```python
import functools
import math

import numpy as np
import jax
import jax.numpy as jnp
from jax import lax
from jax.experimental import pallas as pl
from jax.experimental.pallas import tpu as pltpu

F32 = jnp.float32
BF16 = jnp.bfloat16

D_MODEL = 1024
BATCH = 16
SEQ = 256
DEPTH = 2
DEC_BATCH = 2
DEC_SEQ = 4096
PAST_LEN = 512
GRID_W = 64
HEAD_DIM = 64
N_HEADS_A = 8
N_KV_A = 2
N_HEADS_B = 8
N_KV_B = 2
N_HEADS_C = 16
EVEN_IN = 1536
ODD_IN = 3072
WINDOW = 128
NA_ROWS = 8
NA_COLS = 16
ROPE_THETA = 10000.0
N_EXPERTS = 64
TOP_K = 8
D_EXPERT = 128
ROUTED_SCALE = 2.5
ALPHA = (2 * DEPTH) ** 0.25
LN_EPS = 1e-6
RMS_EPS = 1e-6
NEG_BIG = -1e30
LOG2E = math.log2(math.e)

LANES = 128
P_TOK = BATCH * SEQ
S_TOK = DEC_BATCH * DEC_SEQ
TM = 512
TM_MOE = 1024
EXPERTS_PER_STEP = 4
NA_QROWS = 4
NA_KROWS = 12
NA_SLABS_PER_STEP = 2
VMEM_LIMIT = 56 * 1024 * 1024

_NT = (((1,), (1,)), ((), ()))


def _cparams(sem):
    return pltpu.CompilerParams(dimension_semantics=sem, vmem_limit_bytes=VMEM_LIMIT)


def _half_masks(dtype):
    lane = lax.broadcasted_iota(jnp.int32, (1, LANES), 1)
    lo = jnp.where(lane < HEAD_DIM, 1.0, 0.0).astype(dtype)
    hi = jnp.where(lane < HEAD_DIM, 0.0, 1.0).astype(dtype)
    return lo, hi


def _ada_kernel(c_ref, w_ref, b_ref, o_ref):
    c = c_ref[...]
    a = c / (1.0 + jnp.exp(-c))
    w = w_ref[0]
    a_hi = a.astype(BF16)
    a_lo = (a - a_hi.astype(F32)).astype(BF16)
    w_hi = w.astype(BF16)
    w_lo = (w - w_hi.astype(F32)).astype(BF16)
    acc = jnp.dot(a_hi, w_hi, preferred_element_type=F32)
    acc += jnp.dot(a_lo, w_hi, preferred_element_type=F32)
    acc += jnp.dot(a_hi, w_lo, preferred_element_type=F32)
    o_ref[0] = acc + b_ref[0]


def _adaln(cond8, w_ada, b_ada):
    tn = 1536
    return pl.pallas_call(
        _ada_kernel,
        grid=(DEPTH, 6 * D_MODEL // tn),
        in_specs=[
            pl.BlockSpec((8, D_MODEL), lambda l, j: (0, 0)),
            pl.BlockSpec((1, D_MODEL, tn), lambda l, j: (l, 0, j)),
            pl.BlockSpec((1, 1, tn), lambda l, j: (l, 0, j)),
        ],
        out_specs=pl.BlockSpec((1, 8, tn), lambda l, j: (l, 0, j)),
        out_shape=jax.ShapeDtypeStruct((DEPTH, 8, 6 * D_MODEL), F32),
        compiler_params=_cparams(("parallel", "parallel")),
        name="adaln",
    )(cond8, w_ada, b_ada.reshape(DEPTH, 1, 6 * D_MODEL))


def _group_sum_matrix():
    r = lax.broadcasted_iota(jnp.int32, (LANES, LANES), 0) // HEAD_DIM
    c = lax.broadcasted_iota(jnp.int32, (LANES, LANES), 1) // HEAD_DIM
    return jnp.where(r == c, 1.0, 0.0).astype(BF16)


def _rms_slab(t, g, gmat):
    sq = t * t
    hi = sq.astype(BF16)
    lo = (sq - hi.astype(F32)).astype(BF16)
    ss = jnp.dot(hi, gmat, preferred_element_type=F32) + jnp.dot(lo, gmat, preferred_element_type=F32)
    return t * lax.rsqrt(ss * (1.0 / HEAD_DIM) + RMS_EPS) * g


def _rope_slab(t, cos, sin_signed, first):
    r = jnp.where(first, pltpu.roll(t, LANES - 16, 1), pltpu.roll(t, 16, 1))
    return t * cos + r * sin_signed


def _proj_kernel(x_ref, mod_ref, w_ref, qn_ref, kn_ref, cos_ref, sin_ref, q_ref, k_ref, v_ref,
                 *, dq, dk, q_norm, k_norm, rope):
    x = x_ref[...]
    h = (x * (1.0 + mod_ref[1]) + mod_ref[0]).astype(BF16)
    y = jnp.dot(h, w_ref[...], preferred_element_type=F32)
    tm = x.shape[0]
    gmat = _group_sum_matrix() if (q_norm or k_norm) else None
    if rope:
        cos = cos_ref[...]
        sin_signed = sin_ref[...]
        first = (lax.broadcasted_iota(jnp.int32, (tm, LANES), 1) % 32) < 16
    for s in range(dq // LANES):
        t = y[:, s * LANES:(s + 1) * LANES]
        if s * LANES < q_norm:
            t = _rms_slab(t, qn_ref[...], gmat)
        if rope:
            t = _rope_slab(t, cos, sin_signed, first)
        q_ref[:, s * LANES:(s + 1) * LANES] = (t * (HEAD_DIM ** -0.5 * LOG2E)).astype(q_ref.dtype)
    for s in range(dk // LANES):
        t = y[:, dq + s * LANES:dq + (s + 1) * LANES]
        if s * LANES < k_norm:
            t = _rms_slab(t, kn_ref[...], gmat)
        if rope:
            t = _rope_slab(t, cos, sin_signed, first)
        k_ref[:, s * LANES:(s + 1) * LANES] = t.astype(k_ref.dtype)
    v_ref[...] = y[:, dq + dk:].astype(v_ref.dtype)


def _project(x2d, mod, w_bf16, qn2, kn2, cos, sin_signed, *, dq, dk, dv, q_norm, k_norm, rope, latent):
    n = x2d.shape[0]
    per_batch = DEC_SEQ // TM
    if latent:
        mod_map = lambda i: (0, 1 + i // per_batch, 0, 0)
        pos_map = lambda i: (i % per_batch, 0)
        kv_dtype = BF16
    else:
        mod_map = lambda i: (0, 0, 0, 0)
        pos_map = lambda i: (0, 0)
        kv_dtype = F32
    kern = functools.partial(_proj_kernel, dq=dq, dk=dk, q_norm=q_norm, k_norm=k_norm, rope=rope)
    return pl.pallas_call(
        kern,
        grid=(n // TM,),
        in_specs=[
            pl.BlockSpec((TM, D_MODEL), lambda i: (i, 0)),
            pl.BlockSpec((6, None, 1, D_MODEL), mod_map),
            pl.BlockSpec((D_MODEL, dq + dk + dv), lambda i: (0, 0)),
            pl.BlockSpec((1, LANES), lambda i: (0, 0)),
            pl.BlockSpec((1, LANES), lambda i: (0, 0)),
            pl.BlockSpec((TM, LANES), pos_map),
            pl.BlockSpec((TM, LANES), pos_map),
        ],
        out_specs=[
            pl.BlockSpec((TM, dq), lambda i: (i, 0)),
            pl.BlockSpec((TM, dk), lambda i: (i, 0)),
            pl.BlockSpec((TM, dv), lambda i: (i, 0)),
        ],
        out_shape=[
            jax.ShapeDtypeStruct((n, dq), BF16),
            jax.ShapeDtypeStruct((n, dk), kv_dtype),
            jax.ShapeDtypeStruct((n, dv), kv_dtype),
        ],
        compiler_params=_cparams(("parallel",)),
        name="in_proj",
    )(x2d, mod, w_bf16, qn2, kn2, cos, sin_signed)


def _stack_heads(q_ref, slabs, lo, hi):
    qs = [q_ref[:, j * LANES:(j + 1) * LANES] for j in slabs]
    return jnp.concatenate([q * lo for q in qs] + [q * hi for q in qs], axis=0)


def _pv_with_denominator(p, v, lo, hi):
    half = p.shape[0] // 2
    pv_lo = jnp.dot(p[:half], v * lo + hi, preferred_element_type=F32)
    pv_hi = jnp.dot(p[half:], v * hi + lo, preferred_element_type=F32)
    return jnp.concatenate([pv_lo, pv_hi], axis=0)


def _normalize_store(o, o_ref, slabs, tq, lane_lo):
    half = len(slabs) * tq
    for n, j in enumerate(slabs):
        o_lo = o[n * tq:(n + 1) * tq]
        o_hi = o[half + n * tq:half + (n + 1) * tq]
        o_lo = o_lo * (1.0 / pltpu.roll(o_lo, HEAD_DIM, 1))
        o_hi = o_hi * (1.0 / pltpu.roll(o_hi, HEAD_DIM, 1))
        o_ref[:, j * LANES:(j + 1) * LANES] = jnp.where(lane_lo, o_lo, o_hi).astype(o_ref.dtype)


def _sink_rows(sink_ref, heads, tq):
    return jnp.concatenate([jnp.full((tq, LANES), sink_ref[h] * LOG2E, F32) for h in heads], axis=0)


def _softmax_pv(parts, sink, lo, hi):
    rows = parts[0][0].shape[0]
    m = jnp.full((rows, LANES), NEG_BIG, F32) if sink is None else sink
    for s, _ in parts:
        m = jnp.maximum(m, s.max(axis=1, keepdims=True))
    o = None
    for s, v in parts:
        p = jnp.exp2(s - pltpu.repeat(m, s.shape[1] // LANES, axis=1)).astype(BF16)
        pv = _pv_with_denominator(p, v, lo, hi)
        o = pv if o is None else o + pv
    if sink is not None:
        lo32, hi32 = _half_masks(F32)
        e = jnp.exp2(sink - m)
        half = rows // 2
        o = o + jnp.concatenate([e[:half] * hi32, e[half:] * lo32], axis=0)
    return o


def _ctx_attn_kernel(sink_ref, q_ref, k_ref, v_ref, o_ref, *, groups):
    tq = q_ref.shape[0]
    lo, hi = _half_masks(BF16)
    lane_lo = lax.broadcasted_iota(jnp.int32, (tq, LANES), 1) < HEAD_DIM
    for slabs, ks, sink_heads in groups:
        kk = k_ref[:, ks * LANES:(ks + 1) * LANES].astype(BF16)
        vv = v_ref[:, ks * LANES:(ks + 1) * LANES].astype(BF16)
        s = lax.dot_general(_stack_heads(q_ref, slabs, lo, hi), kk, _NT, preferred_element_type=F32)
        sink = None if sink_heads is None else _sink_rows(sink_ref, sink_heads, tq)
        _normalize_store(_softmax_pv([(s, vv)], sink, lo, hi), o_ref, slabs, tq, lane_lo)


def _ctx_attention(q, k, v, sink, *, groups):
    kw = k.shape[1]
    return pl.pallas_call(
        functools.partial(_ctx_attn_kernel, groups=groups),
        grid=(BATCH,),
        in_specs=[
            pl.BlockSpec(memory_space=pltpu.SMEM),
            pl.BlockSpec((SEQ, D_MODEL), lambda b: (b, 0)),
            pl.BlockSpec((SEQ, kw), lambda b: (b, 0)),
            pl.BlockSpec((SEQ, kw), lambda b: (b, 0)),
        ],
        out_specs=pl.BlockSpec((SEQ, D_MODEL), lambda b: (b, 0)),
        out_shape=jax.ShapeDtypeStruct((P_TOK, D_MODEL), BF16),
        compiler_params=_cparams(("parallel",)),
        name="ctx_attn",
    )(sink, q, k, v)


def _global_attn_kernel(q_ref, k_ref, v_ref, o_ref, s0, s1, p0, p1, a0, a1, m_ref, acc_ref, *, tk):
    tq = q_ref.shape[0]
    slabs = tuple(range(q_ref.shape[1] // LANES))
    nk = k_ref.shape[0] // tk
    lo, hi = _half_masks(BF16)
    lane_lo = lax.broadcasted_iota(jnp.int32, (tq, LANES), 1) < HEAD_DIM
    qstack = _stack_heads(q_ref, slabs, lo, hi)
    m_ref[...] = jnp.full(m_ref.shape, NEG_BIG, F32)
    acc_ref[...] = jnp.zeros(acc_ref.shape, F32)
    s_bufs, p_bufs, a_bufs = (s0, s1), (p0, p1), (a0, a1)

    def stage_a(c):
        s_bufs[c % 2][...] = lax.dot_general(qstack, k_ref[c * tk:(c + 1) * tk, :], _NT, preferred_element_type=F32)

    def stage_b(c):
        s = s_bufs[c % 2][...]
        m_old = m_ref[...]
        m_new = jnp.maximum(m_old, s.max(axis=1, keepdims=True))
        a_bufs[c % 2][...] = jnp.exp2(m_old - m_new)
        p_bufs[c % 2][...] = jnp.exp2(s - pltpu.repeat(m_new, tk // LANES, axis=1)).astype(BF16)
        m_ref[...] = m_new

    def stage_c(c):
        pv = _pv_with_denominator(p_bufs[c % 2][...], v_ref[c * tk:(c + 1) * tk, :], lo, hi)
        acc_ref[...] = a_bufs[c % 2][...] * acc_ref[...] + pv

    for t in range(nk + 2):
        if t < nk:
            stage_a(t)
        if 1 <= t <= nk:
            stage_b(t - 1)
        if t >= 2:
            stage_c(t - 2)
    _normalize_store(acc_ref[...], o_ref, slabs, tq, lane_lo)


def _global_attention(q, kcat, vcat):
    tq, tk = 256, 768
    t = kcat.shape[1]
    per_batch = DEC_SEQ // tq
    rows = N_HEADS_A * tq
    kv_spec = pl.BlockSpec((None, t, LANES), lambda b, i: (b, 0, 0))
    return pl.pallas_call(
        functools.partial(_global_attn_kernel, tk=tk),
        grid=(DEC_BATCH, per_batch),
        in_specs=[pl.BlockSpec((tq, 4 * LANES), lambda b, i: (b * per_batch + i, 0)), kv_spec, kv_spec],
        out_specs=pl.BlockSpec((tq, 4 * LANES), lambda b, i: (b * per_batch + i, 0)),
        out_shape=jax.ShapeDtypeStruct((S_TOK, 4 * LANES), BF16),
        scratch_shapes=[pltpu.VMEM((rows, tk), F32), pltpu.VMEM((rows, tk), F32),
                        pltpu.VMEM((rows, tk), BF16), pltpu.VMEM((rows, tk), BF16),
                        pltpu.VMEM((rows, LANES), F32), pltpu.VMEM((rows, LANES), F32),
                        pltpu.VMEM((rows, LANES), F32), pltpu.VMEM((rows, LANES), F32)],
        compiler_params=_cparams(("parallel", "parallel")),
        name="global_attn",
    )(q, kcat, vcat)


def _window_attn_kernel(sink_ref, q_ref, k_ref, v_ref, o_ref):
    i = pl.program_id(1)
    tq = q_ref.shape[0]
    slabs = tuple(range(q_ref.shape[1] // LANES))
    n_heads = 2 * len(slabs)
    span = tq + 2 * WINDOW
    lo, hi = _half_masks(BF16)
    lane_lo = lax.broadcasted_iota(jnp.int32, (tq, LANES), 1) < HEAD_DIM
    off = pl.multiple_of(PAST_LEN - WINDOW + i * tq, LANES)
    k_loc = k_ref[pl.ds(off, span), :]
    v_loc = v_ref[pl.ds(off, span), :]
    row = lax.broadcasted_iota(jnp.int32, (tq, span), 0)
    col = lax.broadcasted_iota(jnp.int32, (tq, span), 1)
    kpos = i * tq - WINDOW + col
    valid = (col >= row) & (col <= row + 2 * WINDOW) & (kpos >= 0) & (kpos < DEC_SEQ)
    qstack = _stack_heads(q_ref, slabs, lo, hi)
    s_loc = lax.dot_general(qstack, k_loc, _NT, preferred_element_type=F32)
    s_loc = jnp.where(valid[None], s_loc.reshape(n_heads, tq, span), NEG_BIG).reshape(n_heads * tq, span)
    s_ctx = lax.dot_general(qstack, k_ref[0:PAST_LEN, :], _NT, preferred_element_type=F32)
    sink = _sink_rows(sink_ref, range(n_heads), tq)
    o = _softmax_pv([(s_loc, v_loc), (s_ctx, v_ref[0:PAST_LEN, :])], sink, lo, hi)
    _normalize_store(o, o_ref, slabs, tq, lane_lo)


def _window_attention(q, kcat, vcat, sink):
    tq = 128
    t = kcat.shape[1]
    per_batch = DEC_SEQ // tq
    return pl.pallas_call(
        _window_attn_kernel,
        grid=(DEC_BATCH, per_batch),
        in_specs=[
            pl.BlockSpec(memory_space=pltpu.SMEM),
            pl.BlockSpec((tq, 4 * LANES), lambda b, i: (b * per_batch + i, 1)),
            pl.BlockSpec((None, t, LANES), lambda b, i: (b, 0, 0)),
            pl.BlockSpec((None, t, LANES), lambda b, i: (b, 0, 0)),
        ],
        out_specs=pl.BlockSpec((tq, 4 * LANES), lambda b, i: (b * per_batch + i, 0)),
        out_shape=jax.ShapeDtypeStruct((S_TOK, 4 * LANES), BF16),
        compiler_params=_cparams(("parallel", "parallel")),
        name="window_attn",
    )(sink, q, kcat, vcat)


def _na_start_row(g):
    return jnp.clip(NA_QROWS * g - NA_ROWS // 2, 0, DEC_SEQ // GRID_W - NA_KROWS)


def _na_step_bias(t_ref, half, g, lane_lo64):
    rows = DEC_SEQ // GRID_W
    start = _na_start_row(g)
    d0 = start - NA_QROWS * g + NA_ROWS - 1
    row_blocks = []
    for qr in range(NA_QROWS):
        rs = jnp.clip(NA_QROWS * g + qr - NA_ROWS // 2, 0, rows - NA_ROWS)
        blocks = []
        for m in range(NA_KROWS // 2):
            tiles = []
            for kr in (2 * m, 2 * m + 1):
                krow = start + kr
                inside = (krow >= rs) & (krow < rs + NA_ROWS)
                dr = jnp.clip(kr - qr + d0, 0, 2 * NA_ROWS - 2)
                tiles.append(t_ref[half, dr] + jnp.where(inside, 0.0, NEG_BIG))
            blocks.append(jnp.where(lane_lo64, tiles[0], tiles[1]))
        row_blocks.append(jnp.concatenate(blocks, axis=1))
    return jnp.concatenate(row_blocks, axis=0)


def _na_attn_kernel(q_ref, k_ref, v_ref, kc_ref, vc_ref, t_ref, o_ref):
    g = pl.program_id(2)
    tq = q_ref.shape[0]
    lo, hi = _half_masks(BF16)
    lane_lo = lax.broadcasted_iota(jnp.int32, (tq, LANES), 1) < HEAD_DIM
    lane_lo64 = lax.broadcasted_iota(jnp.int32, (GRID_W, LANES), 1) < HEAD_DIM
    off = pl.multiple_of(_na_start_row(g) * GRID_W, GRID_W)
    for n in range(q_ref.shape[1] // LANES):
        cols = slice(n * LANES, (n + 1) * LANES)
        k_loc = k_ref[pl.ds(off, NA_KROWS * GRID_W), cols]
        v_loc = v_ref[pl.ds(off, NA_KROWS * GRID_W), cols]
        qstack = _stack_heads(q_ref, (n,), lo, hi)
        bias = jnp.concatenate([_na_step_bias(t_ref, 2 * n + half, g, lane_lo64) for half in range(2)], axis=0)
        s_loc = lax.dot_general(qstack, k_loc, _NT, preferred_element_type=F32) + bias
        s_ctx = lax.dot_general(qstack, kc_ref[:, cols], _NT, preferred_element_type=F32)
        o = _softmax_pv([(s_loc, v_loc), (s_ctx, vc_ref[:, cols])], None, lo, hi)
        _normalize_store(o, o_ref, (n,), tq, lane_lo)


def _na_bias_table(rpb):
    cols = np.arange(GRID_W)
    cs = np.clip(cols - NA_COLS // 2, 0, GRID_W - NA_COLS)
    dc = cols[None, :] - cols[:, None] + NA_COLS - 1
    inside = (cols[None, :] >= cs[:, None]) & (cols[None, :] < cs[:, None] + NA_COLS)
    onehot = (np.arange(2 * NA_COLS - 1)[:, None, None] == dc[None]) & inside[None]
    t = jnp.einsum("hdc,cqk->hdqk", rpb.astype(F32), jnp.asarray(onehot, F32), precision=lax.Precision.HIGHEST)
    t = jnp.where(inside[None, None], t * LOG2E, NEG_BIG)
    return jnp.concatenate([t, t], axis=-1)


def _na_attention(q, k, v, kc, vc, bias):
    tq = NA_QROWS * GRID_W
    n_groups = DEC_SEQ // tq
    w = NA_SLABS_PER_STEP * LANES
    k3 = k.reshape(DEC_BATCH, DEC_SEQ, D_MODEL)
    v3 = v.reshape(DEC_BATCH, DEC_SEQ, D_MODEL)
    tok = lambda s, b, g: (b * n_groups + g, s)
    per_batch = lambda s, b, g: (b, 0, s)
    return pl.pallas_call(
        _na_attn_kernel,
        grid=(D_MODEL // w, DEC_BATCH, n_groups),
        in_specs=[
            pl.BlockSpec((tq, w), tok),
            pl.BlockSpec((None, DEC_SEQ, w), per_batch),
            pl.BlockSpec((None, DEC_SEQ, w), per_batch),
            pl.BlockSpec((None, PAST_LEN, w), per_batch),
            pl.BlockSpec((None, PAST_LEN, w), per_batch),
            pl.BlockSpec((2 * NA_SLABS_PER_STEP, 2 * NA_ROWS - 1, GRID_W, LANES), lambda s, b, g: (s, 0, 0, 0)),
        ],
        out_specs=pl.BlockSpec((tq, w), tok),
        out_shape=jax.ShapeDtypeStruct((S_TOK, D_MODEL), BF16),
        compiler_params=_cparams(("parallel", "parallel", "parallel")),
        name="na_attn",
    )(q, k3, v3, kc, vc, bias)


def _layer_norm(z, g, b):
    mu = jnp.mean(z, axis=-1, keepdims=True)
    zc = z - mu
    var = jnp.mean(zc * zc, axis=-1, keepdims=True)
    return zc * lax.rsqrt(var + LN_EPS) * g + b


def _post_kernel(*refs, n_parts):
    o_refs = refs[:n_parts]
    (y_ref, mod_ref, w_ref, lng_ref, lnb_ref, wr_hi_ref, wr_lo_ref, br_ref,
     y_out_ref, h_out_ref, gates_ref) = refs[n_parts:]
    pw = D_MODEL // n_parts
    mix = None
    for p, o_ref in enumerate(o_refs):
        part = jnp.dot(o_ref[...], w_ref[p * pw:(p + 1) * pw, :], preferred_element_type=F32)
        mix = part if mix is None else mix + part
    y = _layer_norm(ALPHA * y_ref[...] + mod_ref[2] * mix, lng_ref[...], lnb_ref[...])
    y_out_ref[...] = y
    h = y * (1.0 + mod_ref[4]) + mod_ref[3]
    h_hi = h.astype(BF16)
    h_out_ref[...] = h_hi
    h_lo = (h - h_hi.astype(F32)).astype(BF16)
    logits = jnp.dot(h_hi, wr_hi_ref[...], preferred_element_type=F32)
    logits += jnp.dot(h_lo, wr_hi_ref[...], preferred_element_type=F32)
    logits += jnp.dot(h_hi, wr_lo_ref[...], preferred_element_type=F32)
    scores = 1.0 / (1.0 + jnp.exp(-logits))
    work = scores + br_ref[...]
    lane = lax.broadcasted_iota(jnp.int32, work.shape, 1).astype(F32)
    chosen = jnp.zeros(work.shape, F32)
    for _ in range(TOP_K):
        mx = work.max(axis=1, keepdims=True)
        first = jnp.where(work == mx, lane, float(N_EXPERTS)).min(axis=1, keepdims=True)
        pick = lane == first
        chosen = jnp.where(pick, scores, chosen)
        work = jnp.where(pick, NEG_BIG, work)
    gates_ref[...] = chosen / chosen.sum(axis=1, keepdims=True) * ROUTED_SCALE


def _post_mixer(o_parts, y2d, mod, w_out_bf16, ln_g, ln_b, wr_hi, wr_lo, b_router, *, latent):
    n = y2d.shape[0]
    n_parts = len(o_parts)
    pw = D_MODEL // n_parts
    per_batch = DEC_SEQ // TM
    mod_map = (lambda i: (0, 1 + i // per_batch, 0, 0)) if latent else (lambda i: (0, 0, 0, 0))
    o_specs = [pl.BlockSpec((TM, pw), functools.partial(lambda i, c: (i, c), c=col)) for _, col in o_parts]
    full = lambda i: (0, 0)
    return pl.pallas_call(
        functools.partial(_post_kernel, n_parts=n_parts),
        grid=(n // TM,),
        in_specs=o_specs + [
            pl.BlockSpec((TM, D_MODEL), lambda i: (i, 0)),
            pl.BlockSpec((6, None, 1, D_MODEL), mod_map),
            pl.BlockSpec((D_MODEL, D_MODEL), full),
            pl.BlockSpec((1, D_MODEL), full),
            pl.BlockSpec((1, D_MODEL), full),
            pl.BlockSpec((D_MODEL, N_EXPERTS), full),
            pl.BlockSpec((D_MODEL, N_EXPERTS), full),
            pl.BlockSpec((1, N_EXPERTS), full),
        ],
        out_specs=[
            pl.BlockSpec((TM, D_MODEL), lambda i: (i, 0)),
            pl.BlockSpec((TM, D_MODEL), lambda i: (i, 0)),
            pl.BlockSpec((TM, N_EXPERTS), lambda i: (i, 0)),
        ],
        out_shape=[
            jax.ShapeDtypeStruct((n, D_MODEL), F32),
            jax.ShapeDtypeStruct((n, D_MODEL), BF16),
            jax.ShapeDtypeStruct((n, N_EXPERTS), F32),
        ],
        compiler_params=_cparams(("parallel",)),
        name="post_mixer",
    )(*[a for a, _ in o_parts], y2d, mod, w_out_bf16, ln_g, ln_b, wr_hi, wr_lo, b_router)


def _silu(x):
    return x / (1.0 + jnp.exp(-x))


def _moe_kernel(x_ref, gates_ref, y_ref, mod_ref, wg_ref, wu_ref, wd_ref, sg_ref, su_ref, sd_ref,
                lng_ref, lnb_ref, o_ref, acc_ref):
    e = pl.program_id(1)
    x = x_ref[...]

    @pl.when(e == 0)
    def _():
        a = _silu(jnp.dot(x, sg_ref[...].astype(BF16), preferred_element_type=F32))
        a = a * jnp.dot(x, su_ref[...].astype(BF16), preferred_element_type=F32)
        acc_ref[...] = jnp.dot(a.astype(BF16), sd_ref[...].astype(BF16), preferred_element_type=F32)

    gates = gates_ref[...]
    lane = lax.broadcasted_iota(jnp.int32, gates.shape, 1)
    for pair in range(EXPERTS_PER_STEP // 2):
        acts = []
        for j in (2 * pair, 2 * pair + 1):
            w1 = jnp.concatenate([wg_ref[j].astype(BF16), wu_ref[j].astype(BF16)], axis=1)
            hcat = jnp.dot(x, w1, preferred_element_type=F32)
            ge = jnp.sum(jnp.where(lane == e * EXPERTS_PER_STEP + j, gates, 0.0), axis=1, keepdims=True)
            acts.append((_silu(hcat[:, :D_EXPERT]) * hcat[:, D_EXPERT:] * ge).astype(BF16))
        w2 = jnp.concatenate([wd_ref[2 * pair].astype(BF16), wd_ref[2 * pair + 1].astype(BF16)], axis=0)
        acc_ref[...] += jnp.dot(jnp.concatenate(acts, axis=1), w2, preferred_element_type=F32)

    @pl.when(e == pl.num_programs(1) - 1)
    def _():
        o_ref[...] = _layer_norm(ALPHA * y_ref[...] + mod_ref[5] * acc_ref[...], lng_ref[...], lnb_ref[...])


def _moe(x_bf16, gates, y2d, mod, w_gate, w_up, w_down, ws_gate, ws_up, ws_down, ln_g, ln_b, *, layer, latent):
    n = x_bf16.shape[0]
    per_batch = DEC_SEQ // TM_MOE
    mod_map = (lambda i, e: (0, 1 + i // per_batch, 0, 0)) if latent else (lambda i, e: (0, 0, 0, 0))
    ep = EXPERTS_PER_STEP
    tok = lambda i, e: (i, 0)
    full = lambda i, e: (0, 0)
    routed = lambda i, e: (layer, e, 0, 0)
    shared = lambda i, e: (layer, 0, 0)
    return pl.pallas_call(
        _moe_kernel,
        grid=(n // TM_MOE, N_EXPERTS // ep),
        in_specs=[
            pl.BlockSpec((TM_MOE, D_MODEL), tok),
            pl.BlockSpec((TM_MOE, N_EXPERTS), tok),
            pl.BlockSpec((TM_MOE, D_MODEL), tok),
            pl.BlockSpec((6, None, 1, D_MODEL), mod_map),
            pl.BlockSpec((None, ep, D_MODEL, D_EXPERT), routed),
            pl.BlockSpec((None, ep, D_MODEL, D_EXPERT), routed),
            pl.BlockSpec((None, ep, D_EXPERT, D_MODEL), routed),
            pl.BlockSpec((None, D_MODEL, D_EXPERT), shared),
            pl.BlockSpec((None, D_MODEL, D_EXPERT), shared),
            pl.BlockSpec((None, D_EXPERT, D_MODEL), shared),
            pl.BlockSpec((1, D_MODEL), full),
            pl.BlockSpec((1, D_MODEL), full),
        ],
        out_specs=pl.BlockSpec((TM_MOE, D_MODEL), tok),
        out_shape=jax.ShapeDtypeStruct((n, D_MODEL), F32),
        scratch_shapes=[pltpu.VMEM((TM_MOE, D_MODEL), F32)],
        compiler_params=_cparams(("parallel", "arbitrary")),
        name="moe",
    )(x_bf16, gates, y2d, mod, w_gate, w_up, w_down, ws_gate, ws_up, ws_down, ln_g, ln_b)


def _slab_perm():
    idx = []
    for j in range(4):
        for half in range(2):
            head = j + 4 * half
            idx.extend(range(head * HEAD_DIM, (head + 1) * HEAD_DIM))
    return np.asarray(idx, np.int32)


def _rope_tables():
    t = np.arange(DEC_SEQ)
    quarter = HEAD_DIM // 4
    inv = jnp.asarray(ROPE_THETA, F32) ** (-jnp.arange(quarter, dtype=F32) / quarter)
    ar = jnp.asarray(t // GRID_W, F32)[:, None] * inv
    ac = jnp.asarray(t % GRID_W, F32)[:, None] * inv
    ang = jnp.concatenate([ar, ar, ac, ac] * 2, axis=-1)
    sign = np.where((np.arange(LANES) % 32) < 16, -1.0, 1.0).astype(np.float32)
    return jnp.cos(ang), jnp.sin(ang) * sign


def kernel(x_prompt, x_sample, cache_ka, cache_va, cache_kb, cache_vb, cache_kc, cache_vc, c, c_ctx, w_ada, b_ada, ln_g, ln_b, w_in_even, w_out_even, qnorm_a, knorm_a, sink_b, w_in_odd, w_out_odd, rpb_c, w_router, b_router, w_gate, w_up, w_down, ws_gate, ws_up, ws_down):
    D = D_MODEL
    y_p = x_prompt.reshape(P_TOK, D)
    y_s = x_sample.reshape(S_TOK, D)

    cond8 = jnp.concatenate([c_ctx[None, :], c, jnp.zeros((8 - 1 - DEC_BATCH, D), F32)], axis=0)
    ada = _adaln(cond8, w_ada, b_ada)
    mods = ada.reshape(DEPTH, 8, 6, 1, D).transpose(0, 2, 1, 3, 4)

    cos, sin_signed = _rope_tables()
    perm = _slab_perm()
    in_perm = np.concatenate([perm, 512 + perm, np.arange(1024, EVEN_IN)])
    out_perm = np.concatenate([perm, 512 + perm])
    ones_row = jnp.ones((1, LANES), F32)
    no_sink = jnp.zeros((N_HEADS_B,), F32)

    new = {}
    for l in range(DEPTH):
        mod = mods[l]
        if l % 2 == 0:
            e = l // 2
            w_in = w_in_even[e][:, in_perm].astype(BF16)
            w_out = w_out_even[e][out_perm, :].astype(BF16)
            qn2 = jnp.tile(qnorm_a[e], 2)[None, :]
            kn2 = jnp.tile(knorm_a[e], 2)[None, :]
            proj = functools.partial(_project, w_bf16=w_in, qn2=qn2, kn2=kn2, cos=cos, sin_signed=sin_signed,
                                     dq=1024, dk=256, dv=256, q_norm=512, k_norm=128)
            q_p, k_p, v_p = proj(y_p, mod, rope=False, latent=False)
            q_s, k_s, v_s = proj(y_s, mod, rope=True, latent=True)
            new["ka"], new["kb"] = k_p[:, :LANES], k_p[:, LANES:]
            new["va"], new["vb"] = v_p[:, :LANES], v_p[:, LANES:]
            sink = sink_b[e].astype(F32)
            groups = (((0, 1, 2, 3), 0, None), ((4, 5, 6, 7), 1, tuple(range(N_HEADS_B))))
            o_p = _ctx_attention(q_p, k_p, v_p, sink, groups=groups)

            k_s3 = k_s.reshape(DEC_BATCH, DEC_SEQ, 2 * LANES)
            v_s3 = v_s.reshape(DEC_BATCH, DEC_SEQ, 2 * LANES)
            cka = cache_ka[:, e].reshape(DEC_BATCH, PAST_LEN, LANES).astype(BF16)
            cva = cache_va[:, e].reshape(DEC_BATCH, PAST_LEN, LANES).astype(BF16)
            ckb = cache_kb[:, e].reshape(DEC_BATCH, PAST_LEN, LANES).astype(BF16)
            cvb = cache_vb[:, e].reshape(DEC_BATCH, PAST_LEN, LANES).astype(BF16)
            pad = jnp.zeros((DEC_BATCH, WINDOW, LANES), BF16)
            o_a = _global_attention(q_s, jnp.concatenate([cka, k_s3[:, :, :LANES]], axis=1),
                                    jnp.concatenate([cva, v_s3[:, :, :LANES]], axis=1))
            o_b = _window_attention(q_s, jnp.concatenate([ckb, k_s3[:, :, LANES:], pad], axis=1),
                                    jnp.concatenate([cvb, v_s3[:, :, LANES:], pad], axis=1), sink)
            parts_p = [(o_p, 0), (o_p, 1)]
            parts_s = [(o_a, 0), (o_b, 0)]
        else:
            o = l // 2
            w_in = w_in_odd[o].astype(BF16)
            w_out = w_out_odd[o].astype(BF16)
            proj = functools.partial(_project, w_bf16=w_in, qn2=ones_row, kn2=ones_row, cos=cos,
                                     sin_signed=sin_signed, dq=1024, dk=1024, dv=1024, q_norm=0, k_norm=0,
                                     rope=False)
            q_p, k_p, v_p = proj(y_p, mod, latent=False)
            q_s, k_s, v_s = proj(y_s, mod, latent=True)
            new["kc"], new["vc"] = k_p, v_p
            groups = tuple(((j,), j, None) for j in range(D // LANES))
            o_p = _ctx_attention(q_p, k_p, v_p, no_sink, groups=groups)
            kc = cache_kc[:, o].reshape(DEC_BATCH, PAST_LEN, D).astype(BF16)
            vc = cache_vc[:, o].reshape(DEC_BATCH, PAST_LEN, D).astype(BF16)
            o_s = _na_attention(q_s, k_s, v_s, kc, vc, _na_bias_table(rpb_c[o]))
            parts_p = [(o_p, 0)]
            parts_s = [(o_s, 0)]

        wr = w_router[l]
        wr_hi = wr.astype(BF16)
        wr_lo = (wr - wr_hi.astype(F32)).astype(BF16)
        post = functools.partial(_post_mixer, mod=mod, w_out_bf16=w_out, ln_g=ln_g[l, 0][None, :],
                                 ln_b=ln_b[l, 0][None, :], wr_hi=wr_hi, wr_lo=wr_lo, b_router=b_router[l][None, :])
        y_p, h_p, g_p = post(parts_p, y_p, latent=False)
        y_s, h_s, g_s = post(parts_s, y_s, latent=True)
        moe = functools.partial(_moe, mod=mod, w_gate=w_gate, w_up=w_up, w_down=w_down,
                                ws_gate=ws_gate, ws_up=ws_up, ws_down=ws_down,
                                ln_g=ln_g[l, 1][None, :], ln_b=ln_b[l, 1][None, :], layer=l)
        y_p = moe(h_p, g_p, y_p, latent=False)
        y_s = moe(h_s, g_s, y_s, latent=True)

    kv_a = (BATCH, 1, SEQ, N_KV_A, HEAD_DIM)
    kv_c = (BATCH, 1, SEQ, N_HEADS_C, HEAD_DIM)
    return (y_p.reshape(BATCH, SEQ, D), y_s.reshape(DEC_BATCH, DEC_SEQ, D),
            new["ka"].reshape(kv_a), new["va"].reshape(kv_a), new["kb"].reshape(kv_a), new["vb"].reshape(kv_a),
            new["kc"].reshape(kv_c), new["vc"].reshape(kv_c))
```

```python
import functools
import math

import numpy as np
import jax
import jax.numpy as jnp
from jax import lax
from jax.experimental import pallas as pl
from jax.experimental.pallas import tpu as pltpu

F32 = jnp.float32
BF16 = jnp.bfloat16

D_MODEL = 1024
BATCH = 16
SEQ = 256
DEPTH = 2
DEC_BATCH = 2
DEC_SEQ = 4096
PAST_LEN = 512
GRID_W = 64
HEAD_DIM = 64
N_HEADS_A = 8
N_KV_A = 2
N_HEADS_B = 8
N_KV_B = 2
N_HEADS_C = 16
EVEN_IN = 1536
ODD_IN = 3072
WINDOW = 128
NA_ROWS = 8
NA_COLS = 16
ROPE_THETA = 10000.0
N_EXPERTS = 64
TOP_K = 8
D_EXPERT = 128
ROUTED_SCALE = 2.5
ALPHA = (2 * DEPTH) ** 0.25
LN_EPS = 1e-6
RMS_EPS = 1e-6
NEG_BIG = -1e30
LOG2E = math.log2(math.e)

LANES = 128
P_TOK = BATCH * SEQ
S_TOK = DEC_BATCH * DEC_SEQ
TM = 512
TM_MOE = 1024
EXPERTS_PER_STEP = 8
NA_QROWS = 4
NA_KROWS = 12
NA_SLABS_PER_STEP = 4
VMEM_LIMIT = 56 * 1024 * 1024

_NT = (((1,), (1,)), ((), ()))


def _cparams(sem):
    return pltpu.CompilerParams(dimension_semantics=sem, vmem_limit_bytes=VMEM_LIMIT)


def _half_masks(dtype):
    lane = lax.broadcasted_iota(jnp.int32, (1, LANES), 1)
    lo = jnp.where(lane < HEAD_DIM, 1.0, 0.0).astype(dtype)
    hi = jnp.where(lane < HEAD_DIM, 0.0, 1.0).astype(dtype)
    return lo, hi


def _ada_kernel(c_ref, w_ref, b_ref, o_ref):
    c = c_ref[...]
    a = c / (1.0 + jnp.exp(-c))
    w = w_ref[0]
    a_hi = a.astype(BF16)
    a_lo = (a - a_hi.astype(F32)).astype(BF16)
    w_hi = w.astype(BF16)
    w_lo = (w - w_hi.astype(F32)).astype(BF16)
    acc = jnp.dot(a_hi, w_hi, preferred_element_type=F32)
    acc += jnp.dot(a_lo, w_hi, preferred_element_type=F32)
    acc += jnp.dot(a_hi, w_lo, preferred_element_type=F32)
    o_ref[0] = acc + b_ref[0]


def _adaln(cond8, w_ada, b_ada):
    tn = 1536
    return pl.pallas_call(
        _ada_kernel,
        grid=(DEPTH, 6 * D_MODEL // tn),
        in_specs=[
            pl.BlockSpec((8, D_MODEL), lambda l, j: (0, 0)),
            pl.BlockSpec((1, D_MODEL, tn), lambda l, j: (l, 0, j)),
            pl.BlockSpec((1, 1, tn), lambda l, j: (l, 0, j)),
        ],
        out_specs=pl.BlockSpec((1, 8, tn), lambda l, j: (l, 0, j)),
        out_shape=jax.ShapeDtypeStruct((DEPTH, 8, 6 * D_MODEL), F32),
        compiler_params=_cparams(("parallel", "parallel")),
        name="adaln",
    )(cond8, w_ada, b_ada.reshape(DEPTH, 1, 6 * D_MODEL))


def _group_sum_matrix():
    r = lax.broadcasted_iota(jnp.int32, (LANES, LANES), 0) // HEAD_DIM
    c = lax.broadcasted_iota(jnp.int32, (LANES, LANES), 1) // HEAD_DIM
    return jnp.where(r == c, 1.0, 0.0).astype(BF16)


def _rms_slab(t, g, gmat):
    sq = t * t
    hi = sq.astype(BF16)
    lo = (sq - hi.astype(F32)).astype(BF16)
    ss = jnp.dot(hi, gmat, preferred_element_type=F32) + jnp.dot(lo, gmat, preferred_element_type=F32)
    return t * lax.rsqrt(ss * (1.0 / HEAD_DIM) + RMS_EPS) * g


def _rope_slab(t, cos, sin_signed, first):
    r = jnp.where(first, pltpu.roll(t, LANES - 16, 1), pltpu.roll(t, 16, 1))
    return t * cos + r * sin_signed


def _proj_kernel(x_ref, mod_ref, w_ref, qn_ref, kn_ref, cos_ref, sin_ref, q_ref, k_ref, v_ref,
                 *, dq, dk, q_norm, k_norm, rope):
    x = x_ref[...]
    h = (x * (1.0 + mod_ref[1]) + mod_ref[0]).astype(BF16)
    y = jnp.dot(h, w_ref[...], preferred_element_type=F32)
    tm = x.shape[0]
    gmat = _group_sum_matrix() if (q_norm or k_norm) else None
    if rope:
        cos = cos_ref[...]
        sin_signed = sin_ref[...]
        first = (lax.broadcasted_iota(jnp.int32, (tm, LANES), 1) % 32) < 16
    for s in range(dq // LANES):
        t = y[:, s * LANES:(s + 1) * LANES]
        if s * LANES < q_norm:
            t = _rms_slab(t, qn_ref[...], gmat)
        if rope:
            t = _rope_slab(t, cos, sin_signed, first)
        q_ref[:, s * LANES:(s + 1) * LANES] = (t * (HEAD_DIM ** -0.5 * LOG2E)).astype(q_ref.dtype)
    for s in range(dk // LANES):
        t = y[:, dq + s * LANES:dq + (s + 1) * LANES]
        if s * LANES < k_norm:
            t = _rms_slab(t, kn_ref[...], gmat)
        if rope:
            t = _rope_slab(t, cos, sin_signed, first)
        k_ref[:, s * LANES:(s + 1) * LANES] = t.astype(k_ref.dtype)
    v_ref[...] = y[:, dq + dk:].astype(v_ref.dtype)


def _project(x2d, mod, w_bf16, qn2, kn2, cos, sin_signed, *, dq, dk, dv, q_norm, k_norm, rope, latent):
    n = x2d.shape[0]
    per_batch = DEC_SEQ // TM
    if latent:
        mod_map = lambda i: (0, 1 + i // per_batch, 0, 0)
        pos_map = lambda i: (i % per_batch, 0)
        kv_dtype = BF16
    else:
        mod_map = lambda i: (0, 0, 0, 0)
        pos_map = lambda i: (0, 0)
        kv_dtype = F32
    kern = functools.partial(_proj_kernel, dq=dq, dk=dk, q_norm=q_norm, k_norm=k_norm, rope=rope)
    return pl.pallas_call(
        kern,
        grid=(n // TM,),
        in_specs=[
            pl.BlockSpec((TM, D_MODEL), lambda i: (i, 0)),
            pl.BlockSpec((6, None, 1, D_MODEL), mod_map),
            pl.BlockSpec((D_MODEL, dq + dk + dv), lambda i: (0, 0)),
            pl.BlockSpec((1, LANES), lambda i: (0, 0)),
            pl.BlockSpec((1, LANES), lambda i: (0, 0)),
            pl.BlockSpec((TM, LANES), pos_map),
            pl.BlockSpec((TM, LANES), pos_map),
        ],
        out_specs=[
            pl.BlockSpec((TM, dq), lambda i: (i, 0)),
            pl.BlockSpec((TM, dk), lambda i: (i, 0)),
            pl.BlockSpec((TM, dv), lambda i: (i, 0)),
        ],
        out_shape=[
            jax.ShapeDtypeStruct((n, dq), BF16),
            jax.ShapeDtypeStruct((n, dk), kv_dtype),
            jax.ShapeDtypeStruct((n, dv), kv_dtype),
        ],
        compiler_params=_cparams(("parallel",)),
        name="in_proj",
    )(x2d, mod, w_bf16, qn2, kn2, cos, sin_signed)


def _stack_heads(q_ref, slabs, lo, hi):
    qs = [q_ref[:, j * LANES:(j + 1) * LANES] for j in slabs]
    return jnp.concatenate([q * lo for q in qs] + [q * hi for q in qs], axis=0)


def _tile_lanes(x, n):
    return jnp.concatenate([x] * n, axis=1)


def _pv_with_denominator(p, v, lo, hi):
    half = p.shape[0] // 2
    pv_lo = jnp.dot(p[:half], v * lo + hi, preferred_element_type=F32)
    pv_hi = jnp.dot(p[half:], v * hi + lo, preferred_element_type=F32)
    return jnp.concatenate([pv_lo, pv_hi], axis=0)


def _normalize_store(o, o_ref, slabs, tq, lane_lo):
    half = len(slabs) * tq
    for n, j in enumerate(slabs):
        o_lo = o[n * tq:(n + 1) * tq]
        o_hi = o[half + n * tq:half + (n + 1) * tq]
        o_lo = o_lo * (1.0 / pltpu.roll(o_lo, HEAD_DIM, 1))
        o_hi = o_hi * (1.0 / pltpu.roll(o_hi, HEAD_DIM, 1))
        o_ref[:, j * LANES:(j + 1) * LANES] = jnp.where(lane_lo, o_lo, o_hi).astype(o_ref.dtype)


def _sink_rows(sink_ref, heads, tq):
    return jnp.concatenate([jnp.full((tq, LANES), sink_ref[h] * LOG2E, F32) for h in heads], axis=0)


def _softmax_pv(parts, sink, lo, hi):
    rows = parts[0][0].shape[0]
    m = jnp.full((rows, LANES), NEG_BIG, F32) if sink is None else sink
    for s, _ in parts:
        m = jnp.maximum(m, s.max(axis=1, keepdims=True))
    o = None
    for s, v in parts:
        p = jnp.exp2(s - _tile_lanes(m, s.shape[1] // LANES)).astype(BF16)
        pv = _pv_with_denominator(p, v, lo, hi)
        o = pv if o is None else o + pv
    if sink is not None:
        lo32, hi32 = _half_masks(F32)
        e = jnp.exp2(sink - m)
        half = rows // 2
        o = o + jnp.concatenate([e[:half] * hi32, e[half:] * lo32], axis=0)
    return o


def _ctx_attn_kernel(sink_ref, q_ref, k_ref, v_ref, o_ref, *, groups):
    tq = q_ref.shape[0]
    lo, hi = _half_masks(BF16)
    lane_lo = lax.broadcasted_iota(jnp.int32, (tq, LANES), 1) < HEAD_DIM
    lo32, hi32 = _half_masks(F32)
    for slabs, ks, sink_heads in groups:
        kk = k_ref[:, ks * LANES:(ks + 1) * LANES].astype(BF16)
        vv = v_ref[:, ks * LANES:(ks + 1) * LANES].astype(BF16)
        for n, j in enumerate(slabs):
            qs = q_ref[:, j * LANES:(j + 1) * LANES]
            halves = []
            for own, other, own32, other32, h in ((lo, hi, lo32, hi32, n), (hi, lo, hi32, lo32, len(slabs) + n)):
                s = lax.dot_general(qs * own, kk, _NT, preferred_element_type=F32)
                if sink_heads is None:
                    m = jnp.maximum(jnp.full((tq, LANES), NEG_BIG, F32), s.max(axis=1, keepdims=True))
                else:
                    sink = jnp.full((tq, LANES), sink_ref[sink_heads[h]] * LOG2E, F32)
                    m = jnp.maximum(sink, s.max(axis=1, keepdims=True))
                p = jnp.exp2(s - _tile_lanes(m, s.shape[1] // LANES)).astype(BF16)
                o = jnp.dot(p, vv * own + other, preferred_element_type=F32)
                if sink_heads is not None:
                    o = o + jnp.exp2(sink - m) * other32
                halves.append(o * (1.0 / pltpu.roll(o, HEAD_DIM, 1)))
            o_ref[:, j * LANES:(j + 1) * LANES] = jnp.where(lane_lo, halves[0], halves[1]).astype(o_ref.dtype)


def _ctx_attention(q, k, v, sink, *, groups):
    kw = k.shape[1]
    return pl.pallas_call(
        functools.partial(_ctx_attn_kernel, groups=groups),
        grid=(BATCH,),
        in_specs=[
            pl.BlockSpec(memory_space=pltpu.SMEM),
            pl.BlockSpec((SEQ, D_MODEL), lambda b: (b, 0)),
            pl.BlockSpec((SEQ, kw), lambda b: (b, 0)),
            pl.BlockSpec((SEQ, kw), lambda b: (b, 0)),
        ],
        out_specs=pl.BlockSpec((SEQ, D_MODEL), lambda b: (b, 0)),
        out_shape=jax.ShapeDtypeStruct((P_TOK, D_MODEL), BF16),
        compiler_params=_cparams(("parallel",)),
        name="ctx_attn",
    )(sink, q, k, v)


def _global_attn_kernel(q_ref, k_ref, v_ref, o_ref, s0, s1, p0, p1, a0, a1, m_ref, acc_ref, *, tk):
    tq = q_ref.shape[0]
    slabs = tuple(range(q_ref.shape[1] // LANES))
    nk = k_ref.shape[0] // tk
    lo, hi = _half_masks(BF16)
    lane_lo = lax.broadcasted_iota(jnp.int32, (tq, LANES), 1) < HEAD_DIM
    qstack = _stack_heads(q_ref, slabs, lo, hi)
    m_ref[...] = jnp.full(m_ref.shape, NEG_BIG, F32)
    acc_ref[...] = jnp.zeros(acc_ref.shape, F32)
    s_bufs, p_bufs, a_bufs = (s0, s1), (p0, p1), (a0, a1)

    def stage_a(c):
        s_bufs[c % 2][...] = lax.dot_general(qstack, k_ref[c * tk:(c + 1) * tk, :], _NT, preferred_element_type=F32)

    def stage_b(c):
        s = s_bufs[c % 2][...]
        m_old = m_ref[...]
        m_new = jnp.maximum(m_old, s.max(axis=1, keepdims=True))
        a_bufs[c % 2][...] = jnp.exp2(m_old - m_new)
        p_bufs[c % 2][...] = jnp.exp2(s - _tile_lanes(m_new, tk // LANES)).astype(BF16)
        m_ref[...] = m_new

    def stage_c(c):
        pv = _pv_with_denominator(p_bufs[c % 2][...], v_ref[c * tk:(c + 1) * tk, :], lo, hi)
        acc_ref[...] = a_bufs[c % 2][...] * acc_ref[...] + pv

    for t in range(nk + 2):
        if t < nk:
            stage_a(t)
        if 1 <= t <= nk:
            stage_b(t - 1)
        if t >= 2:
            stage_c(t - 2)
    _normalize_store(acc_ref[...], o_ref, slabs, tq, lane_lo)


def _global_attention(q, kcat, vcat):
    tq, tk = 256, 768
    t = kcat.shape[1]
    per_batch = DEC_SEQ // tq
    rows = N_HEADS_A * tq
    kv_spec = pl.BlockSpec((None, t, LANES), lambda b, i: (b, 0, 0))
    return pl.pallas_call(
        functools.partial(_global_attn_kernel, tk=tk),
        grid=(DEC_BATCH, per_batch),
        in_specs=[pl.BlockSpec((tq, 4 * LANES), lambda b, i: (b * per_batch + i, 0)), kv_spec, kv_spec],
        out_specs=pl.BlockSpec((tq, 4 * LANES), lambda b, i: (b * per_batch + i, 0)),
        out_shape=jax.ShapeDtypeStruct((S_TOK, 4 * LANES), BF16),
        scratch_shapes=[pltpu.VMEM((rows, tk), F32), pltpu.VMEM((rows, tk), F32),
                        pltpu.VMEM((rows, tk), BF16), pltpu.VMEM((rows, tk), BF16),
                        pltpu.VMEM((rows, LANES), F32), pltpu.VMEM((rows, LANES), F32),
                        pltpu.VMEM((rows, LANES), F32), pltpu.VMEM((rows, LANES), F32)],
        compiler_params=_cparams(("parallel", "parallel")),
        name="global_attn",
    )(q, kcat, vcat)


def _window_attn_kernel(sink_ref, q_ref, k_ref, v_ref, o_ref):
    i = pl.program_id(1)
    tq = q_ref.shape[0]
    slabs = tuple(range(q_ref.shape[1] // LANES))
    n_heads = 2 * len(slabs)
    span = tq + 2 * WINDOW
    lo, hi = _half_masks(BF16)
    lane_lo = lax.broadcasted_iota(jnp.int32, (tq, LANES), 1) < HEAD_DIM
    off = pl.multiple_of(PAST_LEN - WINDOW + i * tq, LANES)
    k_loc = k_ref[pl.ds(off, span), :]
    v_loc = v_ref[pl.ds(off, span), :]
    row = lax.broadcasted_iota(jnp.int32, (tq, span), 0)
    col = lax.broadcasted_iota(jnp.int32, (tq, span), 1)
    kpos = i * tq - WINDOW + col
    valid = (col >= row) & (col <= row + 2 * WINDOW) & (kpos >= 0) & (kpos < DEC_SEQ)
    qstack = _stack_heads(q_ref, slabs, lo, hi)
    s_loc = lax.dot_general(qstack, k_loc, _NT, preferred_element_type=F32)
    s_loc = jnp.where(valid[None], s_loc.reshape(n_heads, tq, span), NEG_BIG).reshape(n_heads * tq, span)
    s_ctx = lax.dot_general(qstack, k_ref[0:PAST_LEN, :], _NT, preferred_element_type=F32)
    sink = _sink_rows(sink_ref, range(n_heads), tq)
    o = _softmax_pv([(s_loc, v_loc), (s_ctx, v_ref[0:PAST_LEN, :])], sink, lo, hi)
    _normalize_store(o, o_ref, slabs, tq, lane_lo)


def _window_attention(q, kcat, vcat, sink):
    tq = 128
    t = kcat.shape[1]
    per_batch = DEC_SEQ // tq
    return pl.pallas_call(
        _window_attn_kernel,
        grid=(DEC_BATCH, per_batch),
        in_specs=[
            pl.BlockSpec(memory_space=pltpu.SMEM),
            pl.BlockSpec((tq, 4 * LANES), lambda b, i: (b * per_batch + i, 1)),
            pl.BlockSpec((None, t, LANES), lambda b, i: (b, 0, 0)),
            pl.BlockSpec((None, t, LANES), lambda b, i: (b, 0, 0)),
        ],
        out_specs=pl.BlockSpec((tq, 4 * LANES), lambda b, i: (b * per_batch + i, 0)),
        out_shape=jax.ShapeDtypeStruct((S_TOK, 4 * LANES), BF16),
        compiler_params=_cparams(("parallel", "parallel")),
        name="window_attn",
    )(sink, q, kcat, vcat)


def _na_start_row(g):
    return jnp.clip(NA_QROWS * g - NA_ROWS // 2, 0, DEC_SEQ // GRID_W - NA_KROWS)


def _na_step_bias(t_ref, half, g, lane_lo64):
    rows = DEC_SEQ // GRID_W
    start = _na_start_row(g)
    d0 = start - NA_QROWS * g + NA_ROWS - 1
    row_blocks = []
    for qr in range(NA_QROWS):
        rs = jnp.clip(NA_QROWS * g + qr - NA_ROWS // 2, 0, rows - NA_ROWS)
        blocks = []
        for m in range(NA_KROWS // 2):
            tiles = []
            for kr in (2 * m, 2 * m + 1):
                krow = start + kr
                inside = (krow >= rs) & (krow < rs + NA_ROWS)
                dr = jnp.clip(kr - qr + d0, 0, 2 * NA_ROWS - 2)
                tiles.append(t_ref[half, dr] + jnp.where(inside, 0.0, NEG_BIG))
            blocks.append(jnp.where(lane_lo64, tiles[0], tiles[1]))
        row_blocks.append(jnp.concatenate(blocks, axis=1))
    return jnp.concatenate(row_blocks, axis=0)


def _na_attn_kernel(q_ref, k_ref, v_ref, kc_ref, vc_ref, t_ref, o_ref):
    g = pl.program_id(2)
    tq = q_ref.shape[0]
    lo, hi = _half_masks(BF16)
    lane_lo = lax.broadcasted_iota(jnp.int32, (tq, LANES), 1) < HEAD_DIM
    lane_lo64 = lax.broadcasted_iota(jnp.int32, (GRID_W, LANES), 1) < HEAD_DIM
    off = pl.multiple_of(_na_start_row(g) * GRID_W, GRID_W)
    n_slabs = q_ref.shape[1] // LANES
    cols = [slice(n * LANES, (n + 1) * LANES) for n in range(n_slabs)]

    def scores(n):
        qstack = _stack_heads(q_ref, (n,), lo, hi)
        bias = jnp.concatenate([_na_step_bias(t_ref, 2 * n + half, g, lane_lo64) for half in range(2)], axis=0)
        k_loc = k_ref[pl.ds(off, NA_KROWS * GRID_W), cols[n]]
        s_loc = lax.dot_general(qstack, k_loc, _NT, preferred_element_type=F32) + bias
        return [s_loc, lax.dot_general(qstack, kc_ref[:, cols[n]], _NT, preferred_element_type=F32)]

    def probs(s_parts):
        m = jnp.full((2 * tq, LANES), NEG_BIG, F32)
        for s in s_parts:
            m = jnp.maximum(m, s.max(axis=1, keepdims=True))
        return [jnp.exp2(s - _tile_lanes(m, s.shape[1] // LANES)).astype(BF16) for s in s_parts]

    def output(n, p_parts):
        v_loc = v_ref[pl.ds(off, NA_KROWS * GRID_W), cols[n]]
        o = _pv_with_denominator(p_parts[0], v_loc, lo, hi) + _pv_with_denominator(p_parts[1], vc_ref[:, cols[n]], lo, hi)
        _normalize_store(o, o_ref, (n,), tq, lane_lo)

    s_live, p_live = {}, {}
    for t in range(n_slabs + 2):
        if t < n_slabs:
            s_live[t] = scores(t)
        if 1 <= t <= n_slabs:
            p_live[t - 1] = probs(s_live.pop(t - 1))
        if t >= 2:
            output(t - 2, p_live.pop(t - 2))


def _na_bias_table(rpb):
    cols = np.arange(GRID_W)
    cs = np.clip(cols - NA_COLS // 2, 0, GRID_W - NA_COLS)
    dc = cols[None, :] - cols[:, None] + NA_COLS - 1
    inside = (cols[None, :] >= cs[:, None]) & (cols[None, :] < cs[:, None] + NA_COLS)
    onehot = (np.arange(2 * NA_COLS - 1)[:, None, None] == dc[None]) & inside[None]
    t = jnp.einsum("hdc,cqk->hdqk", rpb.astype(F32), jnp.asarray(onehot, F32), precision=lax.Precision.HIGHEST)
    t = jnp.where(inside[None, None], t * LOG2E, NEG_BIG)
    return jnp.concatenate([t, t], axis=-1)


def _na_attention(q, k, v, kc, vc, bias):
    tq = NA_QROWS * GRID_W
    n_groups = DEC_SEQ // tq
    w = NA_SLABS_PER_STEP * LANES
    k3 = k.reshape(DEC_BATCH, DEC_SEQ, D_MODEL)
    v3 = v.reshape(DEC_BATCH, DEC_SEQ, D_MODEL)
    tok = lambda s, b, g: (b * n_groups + g, s)
    per_batch = lambda s, b, g: (b, 0, s)
    return pl.pallas_call(
        _na_attn_kernel,
        grid=(D_MODEL // w, DEC_BATCH, n_groups),
        in_specs=[
            pl.BlockSpec((tq, w), tok),
            pl.BlockSpec((None, DEC_SEQ, w), per_batch),
            pl.BlockSpec((None, DEC_SEQ, w), per_batch),
            pl.BlockSpec((None, PAST_LEN, w), per_batch),
            pl.BlockSpec((None, PAST_LEN, w), per_batch),
            pl.BlockSpec((2 * NA_SLABS_PER_STEP, 2 * NA_ROWS - 1, GRID_W, LANES), lambda s, b, g: (s, 0, 0, 0)),
        ],
        out_specs=pl.BlockSpec((tq, w), tok),
        out_shape=jax.ShapeDtypeStruct((S_TOK, D_MODEL), BF16),
        compiler_params=_cparams(("parallel", "parallel", "parallel")),
        name="na_attn",
    )(q, k3, v3, kc, vc, bias)


def _layer_norm(z, g, b):
    mu = jnp.mean(z, axis=-1, keepdims=True)
    zc = z - mu
    var = jnp.mean(zc * zc, axis=-1, keepdims=True)
    return zc * lax.rsqrt(var + LN_EPS) * g + b


def _post_kernel(*refs, n_parts):
    o_refs = refs[:n_parts]
    (y_ref, mod_ref, w_ref, lng_ref, lnb_ref, wr_hi_ref, wr_lo_ref, br_ref,
     y_out_ref, h_out_ref, gates_ref) = refs[n_parts:]
    pw = D_MODEL // n_parts
    mix = None
    for p, o_ref in enumerate(o_refs):
        part = jnp.dot(o_ref[...], w_ref[p * pw:(p + 1) * pw, :], preferred_element_type=F32)
        mix = part if mix is None else mix + part
    y = _layer_norm(ALPHA * y_ref[...] + mod_ref[2] * mix, lng_ref[...], lnb_ref[...])
    y_out_ref[...] = y
    h = y * (1.0 + mod_ref[4]) + mod_ref[3]
    h_hi = h.astype(BF16)
    h_out_ref[...] = h_hi
    h_lo = (h - h_hi.astype(F32)).astype(BF16)
    logits = jnp.dot(h_hi, wr_hi_ref[...], preferred_element_type=F32)
    logits += jnp.dot(h_lo, wr_hi_ref[...], preferred_element_type=F32)
    logits += jnp.dot(h_hi, wr_lo_ref[...], preferred_element_type=F32)
    scores = 1.0 / (1.0 + jnp.exp(-logits))
    work = scores + br_ref[...]
    lane = lax.broadcasted_iota(jnp.int32, work.shape, 1).astype(F32)
    chosen = jnp.zeros(work.shape, F32)
    for _ in range(TOP_K):
        mx = work.max(axis=1, keepdims=True)
        first = jnp.where(work == mx, lane, float(N_EXPERTS)).min(axis=1, keepdims=True)
        pick = lane == first
        chosen = jnp.where(pick, scores, chosen)
        work = jnp.where(pick, NEG_BIG, work)
    gates_ref[...] = chosen / chosen.sum(axis=1, keepdims=True) * ROUTED_SCALE


def _post_mixer(o_parts, y2d, mod, w_out_bf16, ln_g, ln_b, wr_hi, wr_lo, b_router, *, latent):
    n = y2d.shape[0]
    n_parts = len(o_parts)
    pw = D_MODEL // n_parts
    per_batch = DEC_SEQ // TM
    mod_map = (lambda i: (0, 1 + i // per_batch, 0, 0)) if latent else (lambda i: (0, 0, 0, 0))
    o_specs = [pl.BlockSpec((TM, pw), functools.partial(lambda i, c: (i, c), c=col)) for _, col in o_parts]
    full = lambda i: (0, 0)
    return pl.pallas_call(
        functools.partial(_post_kernel, n_parts=n_parts),
        grid=(n // TM,),
        in_specs=o_specs + [
            pl.BlockSpec((TM, D_MODEL), lambda i: (i, 0)),
            pl.BlockSpec((6, None, 1, D_MODEL), mod_map),
            pl.BlockSpec((D_MODEL, D_MODEL), full),
            pl.BlockSpec((1, D_MODEL), full),
            pl.BlockSpec((1, D_MODEL), full),
            pl.BlockSpec((D_MODEL, N_EXPERTS), full),
            pl.BlockSpec((D_MODEL, N_EXPERTS), full),
            pl.BlockSpec((1, N_EXPERTS), full),
        ],
        out_specs=[
            pl.BlockSpec((TM, D_MODEL), lambda i: (i, 0)),
            pl.BlockSpec((TM, D_MODEL), lambda i: (i, 0)),
            pl.BlockSpec((TM, N_EXPERTS), lambda i: (i, 0)),
        ],
        out_shape=[
            jax.ShapeDtypeStruct((n, D_MODEL), F32),
            jax.ShapeDtypeStruct((n, D_MODEL), BF16),
            jax.ShapeDtypeStruct((n, N_EXPERTS), F32),
        ],
        compiler_params=_cparams(("parallel",)),
        name="post_mixer",
    )(*[a for a, _ in o_parts], y2d, mod, w_out_bf16, ln_g, ln_b, wr_hi, wr_lo, b_router)


def _silu(x):
    return x / (1.0 + jnp.exp(-x))


def _moe_kernel(x_ref, gates_ref, y_ref, mod_ref, wg_ref, wu_ref, wd_ref, sg_ref, su_ref, sd_ref,
                lng_ref, lnb_ref, o_ref, acc_ref):
    e = pl.program_id(1)
    x = x_ref[...]

    @pl.when(e == 0)
    def _():
        a = _silu(jnp.dot(x, sg_ref[...].astype(BF16), preferred_element_type=F32))
        a = a * jnp.dot(x, su_ref[...].astype(BF16), preferred_element_type=F32)
        acc_ref[...] = jnp.dot(a.astype(BF16), sd_ref[...].astype(BF16), preferred_element_type=F32)

    gates = gates_ref[...]
    lane = lax.broadcasted_iota(jnp.int32, gates.shape, 1)
    for pair in range(EXPERTS_PER_STEP // 2):
        acts = []
        for j in (2 * pair, 2 * pair + 1):
            w1 = jnp.concatenate([wg_ref[j].astype(BF16), wu_ref[j].astype(BF16)], axis=1)
            hcat = jnp.dot(x, w1, preferred_element_type=F32)
            ge = jnp.sum(jnp.where(lane == e * EXPERTS_PER_STEP + j, gates, 0.0), axis=1, keepdims=True)
            acts.append((_silu(hcat[:, :D_EXPERT]) * hcat[:, D_EXPERT:] * ge).astype(BF16))
        w2 = jnp.concatenate([wd_ref[2 * pair].astype(BF16), wd_ref[2 * pair + 1].astype(BF16)], axis=0)
        acc_ref[...] += jnp.dot(jnp.concatenate(acts, axis=1), w2, preferred_element_type=F32)

    @pl.when(e == pl.num_programs(1) - 1)
    def _():
        o_ref[...] = _layer_norm(ALPHA * y_ref[...] + mod_ref[5] * acc_ref[...], lng_ref[...], lnb_ref[...])


def _moe(x_bf16, gates, y2d, mod, w_gate, w_up, w_down, ws_gate, ws_up, ws_down, ln_g, ln_b, *, layer, latent):
    n = x_bf16.shape[0]
    per_batch = DEC_SEQ // TM_MOE
    mod_map = (lambda i, e: (0, 1 + i // per_batch, 0, 0)) if latent else (lambda i, e: (0, 0, 0, 0))
    ep = EXPERTS_PER_STEP
    tok = lambda i, e: (i, 0)
    full = lambda i, e: (0, 0)
    routed = lambda i, e: (layer, e, 0, 0)
    shared = lambda i, e: (layer, 0, 0)
    return pl.pallas_call(
        _moe_kernel,
        grid=(n // TM_MOE, N_EXPERTS // ep),
        in_specs=[
            pl.BlockSpec((TM_MOE, D_MODEL), tok),
            pl.BlockSpec((TM_MOE, N_EXPERTS), tok),
            pl.BlockSpec((TM_MOE, D_MODEL), tok, pipeline_mode=pl.Buffered(1)),
            pl.BlockSpec((6, None, 1, D_MODEL), mod_map),
            pl.BlockSpec((None, ep, D_MODEL, D_EXPERT), routed),
            pl.BlockSpec((None, ep, D_MODEL, D_EXPERT), routed),
            pl.BlockSpec((None, ep, D_EXPERT, D_MODEL), routed),
            pl.BlockSpec((None, D_MODEL, D_EXPERT), shared),
            pl.BlockSpec((None, D_MODEL, D_EXPERT), shared),
            pl.BlockSpec((None, D_EXPERT, D_MODEL), shared),
            pl.BlockSpec((1, D_MODEL), full),
            pl.BlockSpec((1, D_MODEL), full),
        ],
        out_specs=pl.BlockSpec((TM_MOE, D_MODEL), tok),
        out_shape=jax.ShapeDtypeStruct((n, D_MODEL), F32),
        scratch_shapes=[pltpu.VMEM((TM_MOE, D_MODEL), F32)],
        compiler_params=_cparams(("parallel", "arbitrary")),
        name="moe",
    )(x_bf16, gates, y2d, mod, w_gate, w_up, w_down, ws_gate, ws_up, ws_down, ln_g, ln_b)


def _slab_perm():
    idx = []
    for j in range(4):
        for half in range(2):
            head = j + 4 * half
            idx.extend(range(head * HEAD_DIM, (head + 1) * HEAD_DIM))
    return np.asarray(idx, np.int32)


def _rope_tables():
    t = np.arange(DEC_SEQ)
    quarter = HEAD_DIM // 4
    inv = jnp.asarray(ROPE_THETA, F32) ** (-jnp.arange(quarter, dtype=F32) / quarter)
    ar = jnp.asarray(t // GRID_W, F32)[:, None] * inv
    ac = jnp.asarray(t % GRID_W, F32)[:, None] * inv
    ang = jnp.concatenate([ar, ar, ac, ac] * 2, axis=-1)
    sign = np.where((np.arange(LANES) % 32) < 16, -1.0, 1.0).astype(np.float32)
    return jnp.cos(ang), jnp.sin(ang) * sign


def kernel(x_prompt, x_sample, cache_ka, cache_va, cache_kb, cache_vb, cache_kc, cache_vc, c, c_ctx, w_ada, b_ada, ln_g, ln_b, w_in_even, w_out_even, qnorm_a, knorm_a, sink_b, w_in_odd, w_out_odd, rpb_c, w_router, b_router, w_gate, w_up, w_down, ws_gate, ws_up, ws_down):
    D = D_MODEL
    y_p = x_prompt.reshape(P_TOK, D)
    y_s = x_sample.reshape(S_TOK, D)

    cond8 = jnp.concatenate([c_ctx[None, :], c, jnp.zeros((8 - 1 - DEC_BATCH, D), F32)], axis=0)
    ada = _adaln(cond8, w_ada, b_ada)
    mods = ada.reshape(DEPTH, 8, 6, 1, D).transpose(0, 2, 1, 3, 4)

    cos, sin_signed = _rope_tables()
    perm = _slab_perm()
    in_perm = np.concatenate([perm, 512 + perm, np.arange(1024, EVEN_IN)])
    out_perm = np.concatenate([perm, 512 + perm])
    ones_row = jnp.ones((1, LANES), F32)
    no_sink = jnp.zeros((N_HEADS_B,), F32)

    new = {}
    for l in range(DEPTH):
        mod = mods[l]
        if l % 2 == 0:
            e = l // 2
            w_in = w_in_even[e][:, in_perm].astype(BF16)
            w_out = w_out_even[e][out_perm, :].astype(BF16)
            qn2 = jnp.tile(qnorm_a[e], 2)[None, :]
            kn2 = jnp.tile(knorm_a[e], 2)[None, :]
            proj = functools.partial(_project, w_bf16=w_in, qn2=qn2, kn2=kn2, cos=cos, sin_signed=sin_signed,
                                     dq=1024, dk=256, dv=256, q_norm=512, k_norm=128)
            q_p, k_p, v_p = proj(y_p, mod, rope=False, latent=False)
            q_s, k_s, v_s = proj(y_s, mod, rope=True, latent=True)
            new["ka"], new["kb"] = k_p[:, :LANES], k_p[:, LANES:]
            new["va"], new["vb"] = v_p[:, :LANES], v_p[:, LANES:]
            sink = sink_b[e].astype(F32)
            groups = (((0, 1, 2, 3), 0, None), ((4, 5, 6, 7), 1, tuple(range(N_HEADS_B))))
            o_p = _ctx_attention(q_p, k_p, v_p, sink, groups=groups)

            k_s3 = k_s.reshape(DEC_BATCH, DEC_SEQ, 2 * LANES)
            v_s3 = v_s.reshape(DEC_BATCH, DEC_SEQ, 2 * LANES)
            cka = cache_ka[:, e].reshape(DEC_BATCH, PAST_LEN, LANES).astype(BF16)
            cva = cache_va[:, e].reshape(DEC_BATCH, PAST_LEN, LANES).astype(BF16)
            ckb = cache_kb[:, e].reshape(DEC_BATCH, PAST_LEN, LANES).astype(BF16)
            cvb = cache_vb[:, e].reshape(DEC_BATCH, PAST_LEN, LANES).astype(BF16)
            pad = jnp.zeros((DEC_BATCH, WINDOW, LANES), BF16)
            o_a = _global_attention(q_s, jnp.concatenate([cka, k_s3[:, :, :LANES]], axis=1),
                                    jnp.concatenate([cva, v_s3[:, :, :LANES]], axis=1))
            o_b = _window_attention(q_s, jnp.concatenate([ckb, k_s3[:, :, LANES:], pad], axis=1),
                                    jnp.concatenate([cvb, v_s3[:, :, LANES:], pad], axis=1), sink)
            parts_p = [(o_p, 0), (o_p, 1)]
            parts_s = [(o_a, 0), (o_b, 0)]
        else:
            o = l // 2
            w_in = w_in_odd[o].astype(BF16)
            w_out = w_out_odd[o].astype(BF16)
            proj = functools.partial(_project, w_bf16=w_in, qn2=ones_row, kn2=ones_row, cos=cos,
                                     sin_signed=sin_signed, dq=1024, dk=1024, dv=1024, q_norm=0, k_norm=0,
                                     rope=False)
            q_p, k_p, v_p = proj(y_p, mod, latent=False)
            q_s, k_s, v_s = proj(y_s, mod, latent=True)
            new["kc"], new["vc"] = k_p, v_p
            groups = tuple(((j,), j, None) for j in range(D // LANES))
            o_p = _ctx_attention(q_p, k_p, v_p, no_sink, groups=groups)
            kc = cache_kc[:, o].reshape(DEC_BATCH, PAST_LEN, D).astype(BF16)
            vc = cache_vc[:, o].reshape(DEC_BATCH, PAST_LEN, D).astype(BF16)
            o_s = _na_attention(q_s, k_s, v_s, kc, vc, _na_bias_table(rpb_c[o]))
            parts_p = [(o_p, 0)]
            parts_s = [(o_s, 0)]

        wr = w_router[l]
        wr_hi = wr.astype(BF16)
        wr_lo = (wr - wr_hi.astype(F32)).astype(BF16)
        post = functools.partial(_post_mixer, mod=mod, w_out_bf16=w_out, ln_g=ln_g[l, 0][None, :],
                                 ln_b=ln_b[l, 0][None, :], wr_hi=wr_hi, wr_lo=wr_lo, b_router=b_router[l][None, :])
        y_p, h_p, g_p = post(parts_p, y_p, latent=False)
        y_s, h_s, g_s = post(parts_s, y_s, latent=True)
        moe = functools.partial(_moe, mod=mod, w_gate=w_gate, w_up=w_up, w_down=w_down,
                                ws_gate=ws_gate, ws_up=ws_up, ws_down=ws_down,
                                ln_g=ln_g[l, 1][None, :], ln_b=ln_b[l, 1][None, :], layer=l)
        y_p = moe(h_p, g_p, y_p, latent=False)
        y_s = moe(h_s, g_s, y_s, latent=True)

    kv_a = (BATCH, 1, SEQ, N_KV_A, HEAD_DIM)
    kv_c = (BATCH, 1, SEQ, N_HEADS_C, HEAD_DIM)
    return (y_p.reshape(BATCH, SEQ, D), y_s.reshape(DEC_BATCH, DEC_SEQ, D),
            new["ka"].reshape(kv_a), new["va"].reshape(kv_a), new["kb"].reshape(kv_a), new["vb"].reshape(kv_a),
            new["kc"].reshape(kv_c), new["vc"].reshape(kv_c))
```

```python
import functools
import math

import numpy as np
import jax
import jax.numpy as jnp
from jax import lax
from jax.experimental import pallas as pl
from jax.experimental.pallas import tpu as pltpu

F32 = jnp.float32
BF16 = jnp.bfloat16

D_MODEL = 1024
BATCH = 16
SEQ = 256
DEPTH = 2
DEC_BATCH = 2
DEC_SEQ = 4096
PAST_LEN = 512
GRID_W = 64
HEAD_DIM = 64
N_HEADS_A = 8
N_KV_A = 2
N_HEADS_B = 8
N_KV_B = 2
N_HEADS_C = 16
EVEN_IN = 1536
ODD_IN = 3072
WINDOW = 128
NA_ROWS = 8
NA_COLS = 16
ROPE_THETA = 10000.0
N_EXPERTS = 64
TOP_K = 8
D_EXPERT = 128
ROUTED_SCALE = 2.5
ALPHA = (2 * DEPTH) ** 0.25
LN_EPS = 1e-6
RMS_EPS = 1e-6
NEG_BIG = -1e30
LOG2E = math.log2(math.e)

LANES = 128
P_TOK = BATCH * SEQ
S_TOK = DEC_BATCH * DEC_SEQ
TM = 512
TM_MOE = 1024
EXPERTS_PER_STEP = 8
NA_QROWS = 4
NA_KROWS = 12
NA_SLABS_PER_STEP = 4
CTX_SEQS_PER_STEP = 2
VMEM_LIMIT = 56 * 1024 * 1024

_NT = (((1,), (1,)), ((), ()))


def _cparams(sem):
    return pltpu.CompilerParams(dimension_semantics=sem, vmem_limit_bytes=VMEM_LIMIT)


def _half_masks(dtype):
    lane = lax.broadcasted_iota(jnp.int32, (1, LANES), 1)
    lo = jnp.where(lane < HEAD_DIM, 1.0, 0.0).astype(dtype)
    hi = jnp.where(lane < HEAD_DIM, 0.0, 1.0).astype(dtype)
    return lo, hi


def _ada_kernel(c_ref, w_ref, b_ref, o_ref):
    c = c_ref[...]
    a = c / (1.0 + jnp.exp(-c))
    w = w_ref[0]
    a_hi = a.astype(BF16)
    a_lo = (a - a_hi.astype(F32)).astype(BF16)
    w_hi = w.astype(BF16)
    w_lo = (w - w_hi.astype(F32)).astype(BF16)
    acc = jnp.dot(a_hi, w_hi, preferred_element_type=F32)
    acc += jnp.dot(a_lo, w_hi, preferred_element_type=F32)
    acc += jnp.dot(a_hi, w_lo, preferred_element_type=F32)
    o_ref[0] = acc + b_ref[0]


def _adaln(cond8, w_ada, b_ada):
    tn = 1536
    return pl.pallas_call(
        _ada_kernel,
        grid=(DEPTH, 6 * D_MODEL // tn),
        in_specs=[
            pl.BlockSpec((8, D_MODEL), lambda l, j: (0, 0)),
            pl.BlockSpec((1, D_MODEL, tn), lambda l, j: (l, 0, j)),
            pl.BlockSpec((1, 1, tn), lambda l, j: (l, 0, j)),
        ],
        out_specs=pl.BlockSpec((1, 8, tn), lambda l, j: (l, 0, j)),
        out_shape=jax.ShapeDtypeStruct((DEPTH, 8, 6 * D_MODEL), F32),
        compiler_params=_cparams(("parallel", "parallel")),
        name="adaln",
    )(cond8, w_ada, b_ada.reshape(DEPTH, 1, 6 * D_MODEL))


def _group_sum_matrix():
    r = lax.broadcasted_iota(jnp.int32, (LANES, LANES), 0) // HEAD_DIM
    c = lax.broadcasted_iota(jnp.int32, (LANES, LANES), 1) // HEAD_DIM
    return jnp.where(r == c, 1.0, 0.0).astype(BF16)


def _rms_slab(t, g, gmat):
    sq = t * t
    hi = sq.astype(BF16)
    lo = (sq - hi.astype(F32)).astype(BF16)
    ss = jnp.dot(hi, gmat, preferred_element_type=F32) + jnp.dot(lo, gmat, preferred_element_type=F32)
    return t * lax.rsqrt(ss * (1.0 / HEAD_DIM) + RMS_EPS) * g


def _rope_slab(t, cos, sin_signed, first):
    r = jnp.where(first, pltpu.roll(t, LANES - 16, 1), pltpu.roll(t, 16, 1))
    return t * cos + r * sin_signed


def _proj_kernel(x_ref, mod_ref, w_ref, qn_ref, kn_ref, cos_ref, sin_ref, q_ref, k_ref, v_ref,
                 *, dq, dk, q_norm, k_norm, rope):
    x = x_ref[...]
    h = (x * (1.0 + mod_ref[1]) + mod_ref[0]).astype(BF16)
    y = jnp.dot(h, w_ref[...], preferred_element_type=F32)
    tm = x.shape[0]
    gmat = _group_sum_matrix() if (q_norm or k_norm) else None
    if rope:
        cos = cos_ref[...]
        sin_signed = sin_ref[...]
        first = (lax.broadcasted_iota(jnp.int32, (tm, LANES), 1) % 32) < 16
    for s in range(dq // LANES):
        t = y[:, s * LANES:(s + 1) * LANES]
        if s * LANES < q_norm:
            t = _rms_slab(t, qn_ref[...], gmat)
        if rope:
            t = _rope_slab(t, cos, sin_signed, first)
        q_ref[:, s * LANES:(s + 1) * LANES] = (t * (HEAD_DIM ** -0.5 * LOG2E)).astype(q_ref.dtype)
    for s in range(dk // LANES):
        t = y[:, dq + s * LANES:dq + (s + 1) * LANES]
        if s * LANES < k_norm:
            t = _rms_slab(t, kn_ref[...], gmat)
        if rope:
            t = _rope_slab(t, cos, sin_signed, first)
        k_ref[:, s * LANES:(s + 1) * LANES] = t.astype(k_ref.dtype)
    v_ref[...] = y[:, dq + dk:].astype(v_ref.dtype)


def _project(x2d, mod, w_bf16, qn2, kn2, cos, sin_signed, *, dq, dk, dv, q_norm, k_norm, rope, latent):
    n = x2d.shape[0]
    per_batch = DEC_SEQ // TM
    if latent:
        mod_map = lambda i: (0, 1 + i // per_batch, 0, 0)
        pos_map = lambda i: (i % per_batch, 0)
        kv_dtype = BF16
    else:
        mod_map = lambda i: (0, 0, 0, 0)
        pos_map = lambda i: (0, 0)
        kv_dtype = F32
    kern = functools.partial(_proj_kernel, dq=dq, dk=dk, q_norm=q_norm, k_norm=k_norm, rope=rope)
    return pl.pallas_call(
        kern,
        grid=(n // TM,),
        in_specs=[
            pl.BlockSpec((TM, D_MODEL), lambda i: (i, 0)),
            pl.BlockSpec((6, None, 1, D_MODEL), mod_map),
            pl.BlockSpec((D_MODEL, dq + dk + dv), lambda i: (0, 0)),
            pl.BlockSpec((1, LANES), lambda i: (0, 0)),
            pl.BlockSpec((1, LANES), lambda i: (0, 0)),
            pl.BlockSpec((TM, LANES), pos_map),
            pl.BlockSpec((TM, LANES), pos_map),
        ],
        out_specs=[
            pl.BlockSpec((TM, dq), lambda i: (i, 0)),
            pl.BlockSpec((TM, dk), lambda i: (i, 0)),
            pl.BlockSpec((TM, dv), lambda i: (i, 0)),
        ],
        out_shape=[
            jax.ShapeDtypeStruct((n, dq), BF16),
            jax.ShapeDtypeStruct((n, dk), kv_dtype),
            jax.ShapeDtypeStruct((n, dv), kv_dtype),
        ],
        compiler_params=_cparams(("parallel",)),
        name="in_proj",
    )(x2d, mod, w_bf16, qn2, kn2, cos, sin_signed)


def _stack_heads(q_ref, slabs, lo, hi):
    qs = [q_ref[:, j * LANES:(j + 1) * LANES] for j in slabs]
    return jnp.concatenate([q * lo for q in qs] + [q * hi for q in qs], axis=0)


def _tile_lanes(x, n):
    return jnp.concatenate([x] * n, axis=1)


def _pv_with_denominator(p, v, lo, hi):
    half = p.shape[0] // 2
    pv_lo = jnp.dot(p[:half], v * lo + hi, preferred_element_type=F32)
    pv_hi = jnp.dot(p[half:], v * hi + lo, preferred_element_type=F32)
    return jnp.concatenate([pv_lo, pv_hi], axis=0)


def _normalize_store(o, o_ref, slabs, tq, lane_lo):
    half = len(slabs) * tq
    for n, j in enumerate(slabs):
        o_lo = o[n * tq:(n + 1) * tq]
        o_hi = o[half + n * tq:half + (n + 1) * tq]
        o_lo = o_lo * (1.0 / pltpu.roll(o_lo, HEAD_DIM, 1))
        o_hi = o_hi * (1.0 / pltpu.roll(o_hi, HEAD_DIM, 1))
        o_ref[:, j * LANES:(j + 1) * LANES] = jnp.where(lane_lo, o_lo, o_hi).astype(o_ref.dtype)


def _sink_rows(sink_ref, heads, tq):
    return jnp.concatenate([jnp.full((tq, LANES), sink_ref[h] * LOG2E, F32) for h in heads], axis=0)


def _softmax_pv(parts, sink, lo, hi):
    rows = parts[0][0].shape[0]
    m = jnp.full((rows, LANES), NEG_BIG, F32) if sink is None else sink
    for s, _ in parts:
        m = jnp.maximum(m, s.max(axis=1, keepdims=True))
    o = None
    for s, v in parts:
        p = jnp.exp2(s - _tile_lanes(m, s.shape[1] // LANES)).astype(BF16)
        pv = _pv_with_denominator(p, v, lo, hi)
        o = pv if o is None else o + pv
    if sink is not None:
        lo32, hi32 = _half_masks(F32)
        e = jnp.exp2(sink - m)
        half = rows // 2
        o = o + jnp.concatenate([e[:half] * hi32, e[half:] * lo32], axis=0)
    return o


def _ctx_attn_kernel(sink_ref, q_ref, k_ref, v_ref, o_ref, *, groups):
    lo, hi = _half_masks(BF16)
    lo32, hi32 = _half_masks(F32)
    lane_lo = lax.broadcasted_iota(jnp.int32, (SEQ, LANES), 1) < HEAD_DIM
    units = [(b, grp) for b in range(q_ref.shape[0] // SEQ) for grp in groups]

    def scores(unit):
        b, (slabs, ks, _) = unit
        kk = k_ref[b * SEQ:(b + 1) * SEQ, ks * LANES:(ks + 1) * LANES].astype(BF16)
        return lax.dot_general(_stack_heads(q_ref.at[b * SEQ:(b + 1) * SEQ], slabs, lo, hi), kk, _NT,
                               preferred_element_type=F32)

    def probs(unit, s):
        sink_heads = unit[1][2]
        if sink_heads is None:
            m = jnp.maximum(jnp.full((s.shape[0], LANES), NEG_BIG, F32), s.max(axis=1, keepdims=True))
            extra = None
        else:
            sink = _sink_rows(sink_ref, sink_heads, SEQ)
            m = jnp.maximum(sink, s.max(axis=1, keepdims=True))
            e = jnp.exp2(sink - m)
            half = s.shape[0] // 2
            extra = jnp.concatenate([e[:half] * hi32, e[half:] * lo32], axis=0)
        return jnp.exp2(s - _tile_lanes(m, s.shape[1] // LANES)).astype(BF16), extra

    def output(unit, p, extra):
        b, (slabs, ks, _) = unit
        vv = v_ref[b * SEQ:(b + 1) * SEQ, ks * LANES:(ks + 1) * LANES].astype(BF16)
        o = _pv_with_denominator(p, vv, lo, hi)
        if extra is not None:
            o = o + extra
        _normalize_store(o, o_ref.at[b * SEQ:(b + 1) * SEQ], slabs, SEQ, lane_lo)

    s_live, p_live = {}, {}
    for t in range(len(units) + 2):
        if t < len(units):
            s_live[t] = scores(units[t])
        if 1 <= t <= len(units):
            p_live[t - 1] = probs(units[t - 1], s_live.pop(t - 1))
        if t >= 2:
            output(units[t - 2], *p_live.pop(t - 2))


def _ctx_attention(q, k, v, sink, *, groups):
    kw = k.shape[1]
    rows = CTX_SEQS_PER_STEP * SEQ
    return pl.pallas_call(
        functools.partial(_ctx_attn_kernel, groups=groups),
        grid=(BATCH // CTX_SEQS_PER_STEP,),
        in_specs=[
            pl.BlockSpec(memory_space=pltpu.SMEM),
            pl.BlockSpec((rows, D_MODEL), lambda b: (b, 0)),
            pl.BlockSpec((rows, kw), lambda b: (b, 0)),
            pl.BlockSpec((rows, kw), lambda b: (b, 0)),
        ],
        out_specs=pl.BlockSpec((rows, D_MODEL), lambda b: (b, 0)),
        out_shape=jax.ShapeDtypeStruct((P_TOK, D_MODEL), BF16),
        compiler_params=_cparams(("parallel",)),
        name="ctx_attn",
    )(sink, q, k, v)


def _global_attn_kernel(q_ref, k_ref, v_ref, o_ref, s0, s1, p0, p1, a0, a1, m_ref, acc_ref, *, tk):
    tq = q_ref.shape[0]
    slabs = tuple(range(q_ref.shape[1] // LANES))
    nk = k_ref.shape[0] // tk
    lo, hi = _half_masks(BF16)
    lane_lo = lax.broadcasted_iota(jnp.int32, (tq, LANES), 1) < HEAD_DIM
    qstack = _stack_heads(q_ref, slabs, lo, hi)
    m_ref[...] = jnp.full(m_ref.shape, NEG_BIG, F32)
    acc_ref[...] = jnp.zeros(acc_ref.shape, F32)
    s_bufs, p_bufs, a_bufs = (s0, s1), (p0, p1), (a0, a1)

    def stage_a(c):
        s_bufs[c % 2][...] = lax.dot_general(qstack, k_ref[c * tk:(c + 1) * tk, :], _NT, preferred_element_type=F32)

    def stage_b(c):
        s = s_bufs[c % 2][...]
        m_old = m_ref[...]
        m_new = jnp.maximum(m_old, s.max(axis=1, keepdims=True))
        a_bufs[c % 2][...] = jnp.exp2(m_old - m_new)
        p_bufs[c % 2][...] = jnp.exp2(s - _tile_lanes(m_new, tk // LANES)).astype(BF16)
        m_ref[...] = m_new

    def stage_c(c):
        pv = _pv_with_denominator(p_bufs[c % 2][...], v_ref[c * tk:(c + 1) * tk, :], lo, hi)
        acc_ref[...] = a_bufs[c % 2][...] * acc_ref[...] + pv

    for t in range(nk + 2):
        if t < nk:
            stage_a(t)
        if 1 <= t <= nk:
            stage_b(t - 1)
        if t >= 2:
            stage_c(t - 2)
    _normalize_store(acc_ref[...], o_ref, slabs, tq, lane_lo)


def _global_attention(q, kcat, vcat):
    tq, tk = 256, 768
    t = kcat.shape[1]
    per_batch = DEC_SEQ // tq
    rows = N_HEADS_A * tq
    kv_spec = pl.BlockSpec((None, t, LANES), lambda b, i: (b, 0, 0))
    return pl.pallas_call(
        functools.partial(_global_attn_kernel, tk=tk),
        grid=(DEC_BATCH, per_batch),
        in_specs=[pl.BlockSpec((tq, 4 * LANES), lambda b, i: (b * per_batch + i, 0)), kv_spec, kv_spec],
        out_specs=pl.BlockSpec((tq, 4 * LANES), lambda b, i: (b * per_batch + i, 0)),
        out_shape=jax.ShapeDtypeStruct((S_TOK, 4 * LANES), BF16),
        scratch_shapes=[pltpu.VMEM((rows, tk), F32), pltpu.VMEM((rows, tk), F32),
                        pltpu.VMEM((rows, tk), BF16), pltpu.VMEM((rows, tk), BF16),
                        pltpu.VMEM((rows, LANES), F32), pltpu.VMEM((rows, LANES), F32),
                        pltpu.VMEM((rows, LANES), F32), pltpu.VMEM((rows, LANES), F32)],
        compiler_params=_cparams(("parallel", "parallel")),
        name="global_attn",
    )(q, kcat, vcat)


def _window_attn_kernel(sink_ref, q_ref, k_ref, v_ref, o_ref):
    i = pl.program_id(1)
    tq = q_ref.shape[0]
    slabs = tuple(range(q_ref.shape[1] // LANES))
    n_heads = 2 * len(slabs)
    span = tq + 2 * WINDOW
    lo, hi = _half_masks(BF16)
    lane_lo = lax.broadcasted_iota(jnp.int32, (tq, LANES), 1) < HEAD_DIM
    off = pl.multiple_of(PAST_LEN - WINDOW + i * tq, LANES)
    k_loc = k_ref[pl.ds(off, span), :]
    v_loc = v_ref[pl.ds(off, span), :]
    row = lax.broadcasted_iota(jnp.int32, (tq, span), 0)
    col = lax.broadcasted_iota(jnp.int32, (tq, span), 1)
    kpos = i * tq - WINDOW + col
    valid = (col >= row) & (col <= row + 2 * WINDOW) & (kpos >= 0) & (kpos < DEC_SEQ)
    qstack = _stack_heads(q_ref, slabs, lo, hi)
    s_loc = lax.dot_general(qstack, k_loc, _NT, preferred_element_type=F32)
    s_loc = jnp.where(valid[None], s_loc.reshape(n_heads, tq, span), NEG_BIG).reshape(n_heads * tq, span)
    s_ctx = lax.dot_general(qstack, k_ref[0:PAST_LEN, :], _NT, preferred_element_type=F32)
    sink = _sink_rows(sink_ref, range(n_heads), tq)
    o = _softmax_pv([(s_loc, v_loc), (s_ctx, v_ref[0:PAST_LEN, :])], sink, lo, hi)
    _normalize_store(o, o_ref, slabs, tq, lane_lo)


def _window_attention(q, kcat, vcat, sink):
    tq = 128
    t = kcat.shape[1]
    per_batch = DEC_SEQ // tq
    return pl.pallas_call(
        _window_attn_kernel,
        grid=(DEC_BATCH, per_batch),
        in_specs=[
            pl.BlockSpec(memory_space=pltpu.SMEM),
            pl.BlockSpec((tq, 4 * LANES), lambda b, i: (b * per_batch + i, 1)),
            pl.BlockSpec((None, t, LANES), lambda b, i: (b, 0, 0)),
            pl.BlockSpec((None, t, LANES), lambda b, i: (b, 0, 0)),
        ],
        out_specs=pl.BlockSpec((tq, 4 * LANES), lambda b, i: (b * per_batch + i, 0)),
        out_shape=jax.ShapeDtypeStruct((S_TOK, 4 * LANES), BF16),
        compiler_params=_cparams(("parallel", "parallel")),
        name="window_attn",
    )(sink, q, kcat, vcat)


def _na_start_row(g):
    return jnp.clip(NA_QROWS * g - NA_ROWS // 2, 0, DEC_SEQ // GRID_W - NA_KROWS)


def _na_step_bias(t_ref, half, g, lane_lo64):
    rows = DEC_SEQ // GRID_W
    start = _na_start_row(g)
    d0 = start - NA_QROWS * g + NA_ROWS - 1
    row_blocks = []
    for qr in range(NA_QROWS):
        rs = jnp.clip(NA_QROWS * g + qr - NA_ROWS // 2, 0, rows - NA_ROWS)
        blocks = []
        for m in range(NA_KROWS // 2):
            tiles = []
            for kr in (2 * m, 2 * m + 1):
                krow = start + kr
                inside = (krow >= rs) & (krow < rs + NA_ROWS)
                dr = jnp.clip(kr - qr + d0, 0, 2 * NA_ROWS - 2)
                tiles.append(t_ref[half, dr] + jnp.where(inside, 0.0, NEG_BIG))
            blocks.append(jnp.where(lane_lo64, tiles[0], tiles[1]))
        row_blocks.append(jnp.concatenate(blocks, axis=1))
    return jnp.concatenate(row_blocks, axis=0)


def _na_attn_kernel(q_ref, k_ref, v_ref, kc_ref, vc_ref, t_ref, o_ref):
    g = pl.program_id(2)
    tq = q_ref.shape[0]
    lo, hi = _half_masks(BF16)
    lane_lo = lax.broadcasted_iota(jnp.int32, (tq, LANES), 1) < HEAD_DIM
    lane_lo64 = lax.broadcasted_iota(jnp.int32, (GRID_W, LANES), 1) < HEAD_DIM
    off = pl.multiple_of(_na_start_row(g) * GRID_W, GRID_W)
    n_slabs = q_ref.shape[1] // LANES
    cols = [slice(n * LANES, (n + 1) * LANES) for n in range(n_slabs)]

    def scores(n):
        qstack = _stack_heads(q_ref, (n,), lo, hi)
        bias = jnp.concatenate([_na_step_bias(t_ref, 2 * n + half, g, lane_lo64) for half in range(2)], axis=0)
        k_loc = k_ref[pl.ds(off, NA_KROWS * GRID_W), cols[n]]
        s_loc = lax.dot_general(qstack, k_loc, _NT, preferred_element_type=F32) + bias
        return [s_loc, lax.dot_general(qstack, kc_ref[:, cols[n]], _NT, preferred_element_type=F32)]

    def probs(s_parts):
        m = jnp.full((2 * tq, LANES), NEG_BIG, F32)
        for s in s_parts:
            m = jnp.maximum(m, s.max(axis=1, keepdims=True))
        return [jnp.exp2(s - _tile_lanes(m, s.shape[1] // LANES)).astype(BF16) for s in s_parts]

    def output(n, p_parts):
        v_loc = v_ref[pl.ds(off, NA_KROWS * GRID_W), cols[n]]
        o = _pv_with_denominator(p_parts[0], v_loc, lo, hi) + _pv_with_denominator(p_parts[1], vc_ref[:, cols[n]], lo, hi)
        _normalize_store(o, o_ref, (n,), tq, lane_lo)

    s_live, p_live = {}, {}
    for t in range(n_slabs + 2):
        if t < n_slabs:
            s_live[t] = scores(t)
        if 1 <= t <= n_slabs:
            p_live[t - 1] = probs(s_live.pop(t - 1))
        if t >= 2:
            output(t - 2, p_live.pop(t - 2))


def _na_bias_table(rpb):
    cols = np.arange(GRID_W)
    cs = np.clip(cols - NA_COLS // 2, 0, GRID_W - NA_COLS)
    dc = cols[None, :] - cols[:, None] + NA_COLS - 1
    inside = (cols[None, :] >= cs[:, None]) & (cols[None, :] < cs[:, None] + NA_COLS)
    onehot = (np.arange(2 * NA_COLS - 1)[:, None, None] == dc[None]) & inside[None]
    t = jnp.einsum("hdc,cqk->hdqk", rpb.astype(F32), jnp.asarray(onehot, F32), precision=lax.Precision.HIGHEST)
    t = jnp.where(inside[None, None], t * LOG2E, NEG_BIG)
    return jnp.concatenate([t, t], axis=-1)


def _na_attention(q, k, v, kc, vc, bias):
    tq = NA_QROWS * GRID_W
    n_groups = DEC_SEQ // tq
    w = NA_SLABS_PER_STEP * LANES
    k3 = k.reshape(DEC_BATCH, DEC_SEQ, D_MODEL)
    v3 = v.reshape(DEC_BATCH, DEC_SEQ, D_MODEL)
    tok = lambda s, b, g: (b * n_groups + g, s)
    per_batch = lambda s, b, g: (b, 0, s)
    return pl.pallas_call(
        _na_attn_kernel,
        grid=(D_MODEL // w, DEC_BATCH, n_groups),
        in_specs=[
            pl.BlockSpec((tq, w), tok),
            pl.BlockSpec((None, DEC_SEQ, w), per_batch),
            pl.BlockSpec((None, DEC_SEQ, w), per_batch),
            pl.BlockSpec((None, PAST_LEN, w), per_batch),
            pl.BlockSpec((None, PAST_LEN, w), per_batch),
            pl.BlockSpec((2 * NA_SLABS_PER_STEP, 2 * NA_ROWS - 1, GRID_W, LANES), lambda s, b, g: (s, 0, 0, 0)),
        ],
        out_specs=pl.BlockSpec((tq, w), tok),
        out_shape=jax.ShapeDtypeStruct((S_TOK, D_MODEL), BF16),
        compiler_params=_cparams(("parallel", "parallel", "parallel")),
        name="na_attn",
    )(q, k3, v3, kc, vc, bias)


def _layer_norm(z, g, b):
    mu = jnp.mean(z, axis=-1, keepdims=True)
    zc = z - mu
    var = jnp.mean(zc * zc, axis=-1, keepdims=True)
    return zc * lax.rsqrt(var + LN_EPS) * g + b


def _post_kernel(*refs, n_parts):
    o_refs = refs[:n_parts]
    (y_ref, mod_ref, w_ref, lng_ref, lnb_ref, wr_hi_ref, wr_lo_ref, br_ref,
     y_out_ref, h_out_ref, gates_ref) = refs[n_parts:]
    pw = D_MODEL // n_parts
    mix = None
    for p, o_ref in enumerate(o_refs):
        part = jnp.dot(o_ref[...], w_ref[p * pw:(p + 1) * pw, :], preferred_element_type=F32)
        mix = part if mix is None else mix + part
    y = _layer_norm(ALPHA * y_ref[...] + mod_ref[2] * mix, lng_ref[...], lnb_ref[...])
    y_out_ref[...] = y
    h = y * (1.0 + mod_ref[4]) + mod_ref[3]
    h_hi = h.astype(BF16)
    h_out_ref[...] = h_hi
    h_lo = (h - h_hi.astype(F32)).astype(BF16)
    logits = lax.dot_general(wr_hi_ref[...], h_hi, _NT, preferred_element_type=F32)
    logits += lax.dot_general(wr_hi_ref[...], h_lo, _NT, preferred_element_type=F32)
    logits += lax.dot_general(wr_lo_ref[...], h_hi, _NT, preferred_element_type=F32)
    scores = 1.0 / (1.0 + jnp.exp(-logits))
    work = scores + br_ref[...]
    expert = lax.broadcasted_iota(jnp.int32, work.shape, 0).astype(F32)
    chosen = jnp.zeros(work.shape, F32)
    for _ in range(TOP_K):
        mx = work.max(axis=0, keepdims=True)
        first = jnp.where(work == mx, expert, float(N_EXPERTS)).min(axis=0, keepdims=True)
        pick = expert == first
        chosen = jnp.where(pick, scores, chosen)
        work = jnp.where(pick, NEG_BIG, work)
    gates_t = chosen / chosen.sum(axis=0, keepdims=True) * ROUTED_SCALE
    padded = jnp.concatenate([gates_t, jnp.zeros((LANES - N_EXPERTS, gates_t.shape[1]), F32)], axis=0)
    gates_ref[...] = padded.T[:, :N_EXPERTS]


def _post_mixer(o_parts, y2d, mod, w_out_bf16, ln_g, ln_b, wr_hi, wr_lo, b_router, *, latent):
    n = y2d.shape[0]
    n_parts = len(o_parts)
    pw = D_MODEL // n_parts
    per_batch = DEC_SEQ // TM
    mod_map = (lambda i: (0, 1 + i // per_batch, 0, 0)) if latent else (lambda i: (0, 0, 0, 0))
    o_specs = [pl.BlockSpec((TM, pw), functools.partial(lambda i, c: (i, c), c=col)) for _, col in o_parts]
    full = lambda i: (0, 0)
    return pl.pallas_call(
        functools.partial(_post_kernel, n_parts=n_parts),
        grid=(n // TM,),
        in_specs=o_specs + [
            pl.BlockSpec((TM, D_MODEL), lambda i: (i, 0)),
            pl.BlockSpec((6, None, 1, D_MODEL), mod_map),
            pl.BlockSpec((D_MODEL, D_MODEL), full),
            pl.BlockSpec((1, D_MODEL), full),
            pl.BlockSpec((1, D_MODEL), full),
            pl.BlockSpec((N_EXPERTS, D_MODEL), full),
            pl.BlockSpec((N_EXPERTS, D_MODEL), full),
            pl.BlockSpec((N_EXPERTS, 1), full),
        ],
        out_specs=[
            pl.BlockSpec((TM, D_MODEL), lambda i: (i, 0)),
            pl.BlockSpec((TM, D_MODEL), lambda i: (i, 0)),
            pl.BlockSpec((TM, N_EXPERTS), lambda i: (i, 0)),
        ],
        out_shape=[
            jax.ShapeDtypeStruct((n, D_MODEL), F32),
            jax.ShapeDtypeStruct((n, D_MODEL), BF16),
            jax.ShapeDtypeStruct((n, N_EXPERTS), F32),
        ],
        compiler_params=_cparams(("parallel",)),
        name="post_mixer",
    )(*[a for a, _ in o_parts], y2d, mod, w_out_bf16, ln_g, ln_b, wr_hi, wr_lo, b_router)


def _silu(x):
    return x / (1.0 + jnp.exp(-x))


def _moe_kernel(x_ref, gates_ref, y_ref, mod_ref, wg_ref, wu_ref, wd_ref, sg_ref, su_ref, sd_ref,
                lng_ref, lnb_ref, o_ref, acc_ref):
    e = pl.program_id(1)
    x = x_ref[...]

    @pl.when(e == 0)
    def _():
        a = _silu(jnp.dot(x, sg_ref[...].astype(BF16), preferred_element_type=F32))
        a = a * jnp.dot(x, su_ref[...].astype(BF16), preferred_element_type=F32)
        acc_ref[...] = jnp.dot(a.astype(BF16), sd_ref[...].astype(BF16), preferred_element_type=F32)

    gates = gates_ref[...]
    lane = lax.broadcasted_iota(jnp.int32, gates.shape, 1)
    for pair in range(EXPERTS_PER_STEP // 2):
        acts = []
        for j in (2 * pair, 2 * pair + 1):
            w1 = jnp.concatenate([wg_ref[j].astype(BF16), wu_ref[j].astype(BF16)], axis=1)
            hcat = jnp.dot(x, w1, preferred_element_type=F32)
            ge = jnp.sum(jnp.where(lane == e * EXPERTS_PER_STEP + j, gates, 0.0), axis=1, keepdims=True)
            acts.append((_silu(hcat[:, :D_EXPERT]) * hcat[:, D_EXPERT:] * ge).astype(BF16))
        w2 = jnp.concatenate([wd_ref[2 * pair].astype(BF16), wd_ref[2 * pair + 1].astype(BF16)], axis=0)
        acc_ref[...] += jnp.dot(jnp.concatenate(acts, axis=1), w2, preferred_element_type=F32)

    @pl.when(e == pl.num_programs(1) - 1)
    def _():
        o_ref[...] = _layer_norm(ALPHA * y_ref[...] + mod_ref[5] * acc_ref[...], lng_ref[...], lnb_ref[...])


def _moe(x_bf16, gates, y2d, mod, w_gate, w_up, w_down, ws_gate, ws_up, ws_down, ln_g, ln_b, *, layer, latent):
    n = x_bf16.shape[0]
    per_batch = DEC_SEQ // TM_MOE
    mod_map = (lambda i, e: (0, 1 + i // per_batch, 0, 0)) if latent else (lambda i, e: (0, 0, 0, 0))
    ep = EXPERTS_PER_STEP
    tok = lambda i, e: (i, 0)
    full = lambda i, e: (0, 0)
    routed = lambda i, e: (layer, e, 0, 0)
    shared = lambda i, e: (layer, 0, 0)
    return pl.pallas_call(
        _moe_kernel,
        grid=(n // TM_MOE, N_EXPERTS // ep),
        in_specs=[
            pl.BlockSpec((TM_MOE, D_MODEL), tok),
            pl.BlockSpec((TM_MOE, N_EXPERTS), tok),
            pl.BlockSpec((TM_MOE, D_MODEL), tok, pipeline_mode=pl.Buffered(1)),
            pl.BlockSpec((6, None, 1, D_MODEL), mod_map),
            pl.BlockSpec((None, ep, D_MODEL, D_EXPERT), routed),
            pl.BlockSpec((None, ep, D_MODEL, D_EXPERT), routed),
            pl.BlockSpec((None, ep, D_EXPERT, D_MODEL), routed),
            pl.BlockSpec((None, D_MODEL, D_EXPERT), shared),
            pl.BlockSpec((None, D_MODEL, D_EXPERT), shared),
            pl.BlockSpec((None, D_EXPERT, D_MODEL), shared),
            pl.BlockSpec((1, D_MODEL), full),
            pl.BlockSpec((1, D_MODEL), full),
        ],
        out_specs=pl.BlockSpec((TM_MOE, D_MODEL), tok),
        out_shape=jax.ShapeDtypeStruct((n, D_MODEL), F32),
        scratch_shapes=[pltpu.VMEM((TM_MOE, D_MODEL), F32)],
        compiler_params=_cparams(("parallel", "arbitrary")),
        name="moe",
    )(x_bf16, gates, y2d, mod, w_gate, w_up, w_down, ws_gate, ws_up, ws_down, ln_g, ln_b)


def _slab_perm():
    idx = []
    for j in range(4):
        for half in range(2):
            head = j + 4 * half
            idx.extend(range(head * HEAD_DIM, (head + 1) * HEAD_DIM))
    return np.asarray(idx, np.int32)


def _rope_tables():
    t = np.arange(DEC_SEQ)
    quarter = HEAD_DIM // 4
    inv = ROPE_THETA ** (-np.arange(quarter, dtype=np.float64) / quarter)
    ar = (t // GRID_W)[:, None] * inv
    ac = (t % GRID_W)[:, None] * inv
    ang = np.concatenate([ar, ar, ac, ac] * 2, axis=-1)
    sign = np.where((np.arange(LANES) % 32) < 16, -1.0, 1.0)
    return jnp.asarray(np.cos(ang), F32), jnp.asarray(np.sin(ang) * sign, F32)


def kernel(x_prompt, x_sample, cache_ka, cache_va, cache_kb, cache_vb, cache_kc, cache_vc, c, c_ctx, w_ada, b_ada, ln_g, ln_b, w_in_even, w_out_even, qnorm_a, knorm_a, sink_b, w_in_odd, w_out_odd, rpb_c, w_router, b_router, w_gate, w_up, w_down, ws_gate, ws_up, ws_down):
    D = D_MODEL
    y_p = x_prompt.reshape(P_TOK, D)
    y_s = x_sample.reshape(S_TOK, D)

    cond8 = jnp.concatenate([c_ctx[None, :], c, jnp.zeros((8 - 1 - DEC_BATCH, D), F32)], axis=0)
    ada = _adaln(cond8, w_ada, b_ada)
    mods = ada.reshape(DEPTH, 8, 6, 1, D).transpose(0, 2, 1, 3, 4)

    cos, sin_signed = _rope_tables()
    perm = _slab_perm()
    in_perm = np.concatenate([perm, 512 + perm, np.arange(1024, EVEN_IN)])
    out_perm = np.concatenate([perm, 512 + perm])
    ones_row = jnp.ones((1, LANES), F32)
    no_sink = jnp.zeros((N_HEADS_B,), F32)

    new = {}
    for l in range(DEPTH):
        mod = mods[l]
        if l % 2 == 0:
            e = l // 2
            w_in = w_in_even[e][:, in_perm].astype(BF16)
            w_out = w_out_even[e][out_perm, :].astype(BF16)
            qn2 = jnp.tile(qnorm_a[e], 2)[None, :]
            kn2 = jnp.tile(knorm_a[e], 2)[None, :]
            proj = functools.partial(_project, w_bf16=w_in, qn2=qn2, kn2=kn2, cos=cos, sin_signed=sin_signed,
                                     dq=1024, dk=256, dv=256, q_norm=512, k_norm=128)
            q_p, k_p, v_p = proj(y_p, mod, rope=False, latent=False)
            q_s, k_s, v_s = proj(y_s, mod, rope=True, latent=True)
            new["ka"], new["kb"] = k_p[:, :LANES], k_p[:, LANES:]
            new["va"], new["vb"] = v_p[:, :LANES], v_p[:, LANES:]
            sink = sink_b[e].astype(F32)
            groups = (((0, 1, 2, 3), 0, None), ((4, 5, 6, 7), 1, tuple(range(N_HEADS_B))))
            o_p = _ctx_attention(q_p, k_p, v_p, sink, groups=groups)

            k_s3 = k_s.reshape(DEC_BATCH, DEC_SEQ, 2 * LANES)
            v_s3 = v_s.reshape(DEC_BATCH, DEC_SEQ, 2 * LANES)
            cka = cache_ka[:, e].reshape(DEC_BATCH, PAST_LEN, LANES).astype(BF16)
            cva = cache_va[:, e].reshape(DEC_BATCH, PAST_LEN, LANES).astype(BF16)
            ckb = cache_kb[:, e].reshape(DEC_BATCH, PAST_LEN, LANES).astype(BF16)
            cvb = cache_vb[:, e].reshape(DEC_BATCH, PAST_LEN, LANES).astype(BF16)
            pad = jnp.zeros((DEC_BATCH, WINDOW, LANES), BF16)
            o_a = _global_attention(q_s, jnp.concatenate([cka, k_s3[:, :, :LANES]], axis=1),
                                    jnp.concatenate([cva, v_s3[:, :, :LANES]], axis=1))
            o_b = _window_attention(q_s, jnp.concatenate([ckb, k_s3[:, :, LANES:], pad], axis=1),
                                    jnp.concatenate([cvb, v_s3[:, :, LANES:], pad], axis=1), sink)
            parts_p = [(o_p, 0), (o_p, 1)]
            parts_s = [(o_a, 0), (o_b, 0)]
        else:
            o = l // 2
            w_in = w_in_odd[o].astype(BF16)
            w_out = w_out_odd[o].astype(BF16)
            proj = functools.partial(_project, w_bf16=w_in, qn2=ones_row, kn2=ones_row, cos=cos,
                                     sin_signed=sin_signed, dq=1024, dk=1024, dv=1024, q_norm=0, k_norm=0,
                                     rope=False)
            q_p, k_p, v_p = proj(y_p, mod, latent=False)
            q_s, k_s, v_s = proj(y_s, mod, latent=True)
            new["kc"], new["vc"] = k_p, v_p
            groups = tuple(((j,), j, None) for j in range(D // LANES))
            o_p = _ctx_attention(q_p, k_p, v_p, no_sink, groups=groups)
            kc = cache_kc[:, o].reshape(DEC_BATCH, PAST_LEN, D).astype(BF16)
            vc = cache_vc[:, o].reshape(DEC_BATCH, PAST_LEN, D).astype(BF16)
            o_s = _na_attention(q_s, k_s, v_s, kc, vc, _na_bias_table(rpb_c[o]))
            parts_p = [(o_p, 0)]
            parts_s = [(o_s, 0)]

        wr = w_router[l].T
        wr_hi = wr.astype(BF16)
        wr_lo = (wr - wr_hi.astype(F32)).astype(BF16)
        post = functools.partial(_post_mixer, mod=mod, w_out_bf16=w_out, ln_g=ln_g[l, 0][None, :],
                                 ln_b=ln_b[l, 0][None, :], wr_hi=wr_hi, wr_lo=wr_lo, b_router=b_router[l][:, None])
        y_p, h_p, g_p = post(parts_p, y_p, latent=False)
        y_s, h_s, g_s = post(parts_s, y_s, latent=True)
        moe = functools.partial(_moe, mod=mod, w_gate=w_gate, w_up=w_up, w_down=w_down,
                                ws_gate=ws_gate, ws_up=ws_up, ws_down=ws_down,
                                ln_g=ln_g[l, 1][None, :], ln_b=ln_b[l, 1][None, :], layer=l)
        y_p = moe(h_p, g_p, y_p, latent=False)
        y_s = moe(h_s, g_s, y_s, latent=True)

    kv_a = (BATCH, 1, SEQ, N_KV_A, HEAD_DIM)
    kv_c = (BATCH, 1, SEQ, N_HEADS_C, HEAD_DIM)
    return (y_p.reshape(BATCH, SEQ, D), y_s.reshape(DEC_BATCH, DEC_SEQ, D),
            new["ka"].reshape(kv_a), new["va"].reshape(kv_a), new["kb"].reshape(kv_a), new["vb"].reshape(kv_a),
            new["kc"].reshape(kv_c), new["vc"].reshape(kv_c))
```

```python
import functools
import math

import numpy as np
import jax
import jax.numpy as jnp
from jax import lax
from jax.experimental import pallas as pl
from jax.experimental.pallas import tpu as pltpu

F32 = jnp.float32
BF16 = jnp.bfloat16

D_MODEL = 1024
BATCH = 16
SEQ = 256
DEPTH = 2
DEC_BATCH = 2
DEC_SEQ = 4096
PAST_LEN = 512
GRID_W = 64
HEAD_DIM = 64
N_HEADS_A = 8
N_KV_A = 2
N_HEADS_B = 8
N_KV_B = 2
N_HEADS_C = 16
EVEN_IN = 1536
ODD_IN = 3072
WINDOW = 128
NA_ROWS = 8
NA_COLS = 16
ROPE_THETA = 10000.0
N_EXPERTS = 64
TOP_K = 8
D_EXPERT = 128
ROUTED_SCALE = 2.5
ALPHA = (2 * DEPTH) ** 0.25
LN_EPS = 1e-6
RMS_EPS = 1e-6
NEG_BIG = -1e30
LOG2E = math.log2(math.e)

LANES = 128
P_TOK = BATCH * SEQ
S_TOK = DEC_BATCH * DEC_SEQ
TM = 512
TM_ROUTE = 256
MOE_CAP = 48
TM_MOE = 1024
EXPERTS_PER_STEP = 8
NA_QROWS = 4
NA_KROWS = 12
NA_SLABS_PER_STEP = 4
CTX_SEQS_PER_STEP = 2
VMEM_LIMIT = 56 * 1024 * 1024

_NT = (((1,), (1,)), ((), ()))


def _cparams(sem):
    return pltpu.CompilerParams(dimension_semantics=sem, vmem_limit_bytes=VMEM_LIMIT)


def _half_masks(dtype):
    lane = lax.broadcasted_iota(jnp.int32, (1, LANES), 1)
    lo = jnp.where(lane < HEAD_DIM, 1.0, 0.0).astype(dtype)
    hi = jnp.where(lane < HEAD_DIM, 0.0, 1.0).astype(dtype)
    return lo, hi


def _ada_kernel(c_ref, w_ref, b_ref, o_ref):
    c = c_ref[...]
    a = c / (1.0 + jnp.exp(-c))
    w = w_ref[0]
    a_hi = a.astype(BF16)
    a_lo = (a - a_hi.astype(F32)).astype(BF16)
    w_hi = w.astype(BF16)
    w_lo = (w - w_hi.astype(F32)).astype(BF16)
    acc = jnp.dot(a_hi, w_hi, preferred_element_type=F32)
    acc += jnp.dot(a_lo, w_hi, preferred_element_type=F32)
    acc += jnp.dot(a_hi, w_lo, preferred_element_type=F32)
    o_ref[0] = acc + b_ref[0]


def _adaln(cond8, w_ada, b_ada):
    tn = 1536
    return pl.pallas_call(
        _ada_kernel,
        grid=(DEPTH, 6 * D_MODEL // tn),
        in_specs=[
            pl.BlockSpec((8, D_MODEL), lambda l, j: (0, 0)),
            pl.BlockSpec((1, D_MODEL, tn), lambda l, j: (l, 0, j)),
            pl.BlockSpec((1, 1, tn), lambda l, j: (l, 0, j)),
        ],
        out_specs=pl.BlockSpec((1, 8, tn), lambda l, j: (l, 0, j)),
        out_shape=jax.ShapeDtypeStruct((DEPTH, 8, 6 * D_MODEL), F32),
        compiler_params=_cparams(("parallel", "parallel")),
        name="adaln",
    )(cond8, w_ada, b_ada.reshape(DEPTH, 1, 6 * D_MODEL))


def _group_sum_matrix():
    r = lax.broadcasted_iota(jnp.int32, (LANES, LANES), 0) // HEAD_DIM
    c = lax.broadcasted_iota(jnp.int32, (LANES, LANES), 1) // HEAD_DIM
    return jnp.where(r == c, 1.0, 0.0).astype(BF16)


def _rms_slab(t, g, gmat):
    sq = t * t
    hi = sq.astype(BF16)
    lo = (sq - hi.astype(F32)).astype(BF16)
    ss = jnp.dot(hi, gmat, preferred_element_type=F32) + jnp.dot(lo, gmat, preferred_element_type=F32)
    return t * lax.rsqrt(ss * (1.0 / HEAD_DIM) + RMS_EPS) * g


def _rope_slab(t, cos, sin_signed, first):
    r = jnp.where(first, pltpu.roll(t, LANES - 16, 1), pltpu.roll(t, 16, 1))
    return t * cos + r * sin_signed


def _proj_kernel(x_ref, mod_ref, w_ref, qn_ref, kn_ref, cos_ref, sin_ref, q_ref, k_ref, v_ref,
                 *, dq, dk, q_norm, k_norm, rope):
    x = x_ref[...]
    h = (x * (1.0 + mod_ref[1]) + mod_ref[0]).astype(BF16)
    y = jnp.dot(h, w_ref[...], preferred_element_type=F32)
    tm = x.shape[0]
    gmat = _group_sum_matrix() if (q_norm or k_norm) else None
    if rope:
        cos = cos_ref[...]
        sin_signed = sin_ref[...]
        first = (lax.broadcasted_iota(jnp.int32, (tm, LANES), 1) % 32) < 16
    for s in range(dq // LANES):
        t = y[:, s * LANES:(s + 1) * LANES]
        if s * LANES < q_norm:
            t = _rms_slab(t, qn_ref[...], gmat)
        if rope:
            t = _rope_slab(t, cos, sin_signed, first)
        q_ref[:, s * LANES:(s + 1) * LANES] = (t * (HEAD_DIM ** -0.5 * LOG2E)).astype(q_ref.dtype)
    for s in range(dk // LANES):
        t = y[:, dq + s * LANES:dq + (s + 1) * LANES]
        if s * LANES < k_norm:
            t = _rms_slab(t, kn_ref[...], gmat)
        if rope:
            t = _rope_slab(t, cos, sin_signed, first)
        k_ref[:, s * LANES:(s + 1) * LANES] = t.astype(k_ref.dtype)
    v_ref[...] = y[:, dq + dk:].astype(v_ref.dtype)


def _project(x2d, mod, w_bf16, qn2, kn2, cos, sin_signed, *, dq, dk, dv, q_norm, k_norm, rope, latent):
    n = x2d.shape[0]
    per_batch = DEC_SEQ // TM
    if latent:
        mod_map = lambda i: (0, 1 + i // per_batch, 0, 0)
        pos_map = lambda i: (i % per_batch, 0)
        kv_dtype = BF16
    else:
        mod_map = lambda i: (0, 0, 0, 0)
        pos_map = lambda i: (0, 0)
        kv_dtype = F32
    kern = functools.partial(_proj_kernel, dq=dq, dk=dk, q_norm=q_norm, k_norm=k_norm, rope=rope)
    return pl.pallas_call(
        kern,
        grid=(n // TM,),
        in_specs=[
            pl.BlockSpec((TM, D_MODEL), lambda i: (i, 0)),
            pl.BlockSpec((6, None, 1, D_MODEL), mod_map),
            pl.BlockSpec((D_MODEL, dq + dk + dv), lambda i: (0, 0)),
            pl.BlockSpec((1, LANES), lambda i: (0, 0)),
            pl.BlockSpec((1, LANES), lambda i: (0, 0)),
            pl.BlockSpec((TM, LANES), pos_map),
            pl.BlockSpec((TM, LANES), pos_map),
        ],
        out_specs=[
            pl.BlockSpec((TM, dq), lambda i: (i, 0)),
            pl.BlockSpec((TM, dk), lambda i: (i, 0)),
            pl.BlockSpec((TM, dv), lambda i: (i, 0)),
        ],
        out_shape=[
            jax.ShapeDtypeStruct((n, dq), BF16),
            jax.ShapeDtypeStruct((n, dk), kv_dtype),
            jax.ShapeDtypeStruct((n, dv), kv_dtype),
        ],
        compiler_params=_cparams(("parallel",)),
        name="in_proj",
    )(x2d, mod, w_bf16, qn2, kn2, cos, sin_signed)


def _stack_heads(q_ref, slabs, lo, hi):
    qs = [q_ref[:, j * LANES:(j + 1) * LANES] for j in slabs]
    return jnp.concatenate([q * lo for q in qs] + [q * hi for q in qs], axis=0)


def _tile_lanes(x, n):
    return jnp.concatenate([x] * n, axis=1)


def _pv_with_denominator(p, v, lo, hi):
    half = p.shape[0] // 2
    pv_lo = jnp.dot(p[:half], v * lo + hi, preferred_element_type=F32)
    pv_hi = jnp.dot(p[half:], v * hi + lo, preferred_element_type=F32)
    return jnp.concatenate([pv_lo, pv_hi], axis=0)


def _normalize_store(o, o_ref, slabs, tq, lane_lo):
    half = len(slabs) * tq
    for n, j in enumerate(slabs):
        o_lo = o[n * tq:(n + 1) * tq]
        o_hi = o[half + n * tq:half + (n + 1) * tq]
        o_lo = o_lo * (1.0 / pltpu.roll(o_lo, HEAD_DIM, 1))
        o_hi = o_hi * (1.0 / pltpu.roll(o_hi, HEAD_DIM, 1))
        o_ref[:, j * LANES:(j + 1) * LANES] = jnp.where(lane_lo, o_lo, o_hi).astype(o_ref.dtype)


def _sink_rows(sink_ref, heads, tq):
    return jnp.concatenate([jnp.full((tq, LANES), sink_ref[h] * LOG2E, F32) for h in heads], axis=0)


def _softmax_pv(parts, sink, lo, hi):
    rows = parts[0][0].shape[0]
    m = jnp.full((rows, LANES), NEG_BIG, F32) if sink is None else sink
    for s, _ in parts:
        m = jnp.maximum(m, s.max(axis=1, keepdims=True))
    o = None
    for s, v in parts:
        p = jnp.exp2(s - _tile_lanes(m, s.shape[1] // LANES)).astype(BF16)
        pv = _pv_with_denominator(p, v, lo, hi)
        o = pv if o is None else o + pv
    if sink is not None:
        lo32, hi32 = _half_masks(F32)
        e = jnp.exp2(sink - m)
        half = rows // 2
        o = o + jnp.concatenate([e[:half] * hi32, e[half:] * lo32], axis=0)
    return o


def _ctx_attn_kernel(sink_ref, q_ref, k_ref, v_ref, o_ref, *, groups):
    lo, hi = _half_masks(BF16)
    lo32, hi32 = _half_masks(F32)
    lane_lo = lax.broadcasted_iota(jnp.int32, (SEQ, LANES), 1) < HEAD_DIM
    units = [(b, grp) for b in range(q_ref.shape[0] // SEQ) for grp in groups]

    def scores(unit):
        b, (slabs, ks, _) = unit
        kk = k_ref[b * SEQ:(b + 1) * SEQ, ks * LANES:(ks + 1) * LANES].astype(BF16)
        return lax.dot_general(_stack_heads(q_ref.at[b * SEQ:(b + 1) * SEQ], slabs, lo, hi), kk, _NT,
                               preferred_element_type=F32)

    def probs(unit, s):
        sink_heads = unit[1][2]
        if sink_heads is None:
            m = jnp.maximum(jnp.full((s.shape[0], LANES), NEG_BIG, F32), s.max(axis=1, keepdims=True))
            extra = None
        else:
            sink = _sink_rows(sink_ref, sink_heads, SEQ)
            m = jnp.maximum(sink, s.max(axis=1, keepdims=True))
            e = jnp.exp2(sink - m)
            half = s.shape[0] // 2
            extra = jnp.concatenate([e[:half] * hi32, e[half:] * lo32], axis=0)
        return jnp.exp2(s - _tile_lanes(m, s.shape[1] // LANES)).astype(BF16), extra

    def output(unit, p, extra):
        b, (slabs, ks, _) = unit
        vv = v_ref[b * SEQ:(b + 1) * SEQ, ks * LANES:(ks + 1) * LANES].astype(BF16)
        o = _pv_with_denominator(p, vv, lo, hi)
        if extra is not None:
            o = o + extra
        _normalize_store(o, o_ref.at[b * SEQ:(b + 1) * SEQ], slabs, SEQ, lane_lo)

    s_live, p_live = {}, {}
    for t in range(len(units) + 2):
        if t < len(units):
            s_live[t] = scores(units[t])
        if 1 <= t <= len(units):
            p_live[t - 1] = probs(units[t - 1], s_live.pop(t - 1))
        if t >= 2:
            output(units[t - 2], *p_live.pop(t - 2))


def _ctx_attention(q, k, v, sink, *, groups):
    kw = k.shape[1]
    rows = CTX_SEQS_PER_STEP * SEQ
    return pl.pallas_call(
        functools.partial(_ctx_attn_kernel, groups=groups),
        grid=(BATCH // CTX_SEQS_PER_STEP,),
        in_specs=[
            pl.BlockSpec(memory_space=pltpu.SMEM),
            pl.BlockSpec((rows, D_MODEL), lambda b: (b, 0)),
            pl.BlockSpec((rows, kw), lambda b: (b, 0)),
            pl.BlockSpec((rows, kw), lambda b: (b, 0)),
        ],
        out_specs=pl.BlockSpec((rows, D_MODEL), lambda b: (b, 0)),
        out_shape=jax.ShapeDtypeStruct((P_TOK, D_MODEL), BF16),
        compiler_params=_cparams(("parallel",)),
        name="ctx_attn",
    )(sink, q, k, v)


def _global_attn_kernel(q_ref, k_ref, v_ref, o_ref, s0, s1, p0, p1, a0, a1, m_ref, acc_ref, *, tk):
    tq = q_ref.shape[0]
    slabs = tuple(range(q_ref.shape[1] // LANES))
    nk = k_ref.shape[0] // tk
    lo, hi = _half_masks(BF16)
    lane_lo = lax.broadcasted_iota(jnp.int32, (tq, LANES), 1) < HEAD_DIM
    qstack = _stack_heads(q_ref, slabs, lo, hi)
    m_ref[...] = jnp.full(m_ref.shape, NEG_BIG, F32)
    acc_ref[...] = jnp.zeros(acc_ref.shape, F32)
    s_bufs, p_bufs, a_bufs = (s0, s1), (p0, p1), (a0, a1)

    def stage_a(c):
        s_bufs[c % 2][...] = lax.dot_general(qstack, k_ref[c * tk:(c + 1) * tk, :], _NT, preferred_element_type=F32)

    def stage_b(c):
        s = s_bufs[c % 2][...]
        m_old = m_ref[...]
        m_new = jnp.maximum(m_old, s.max(axis=1, keepdims=True))
        a_bufs[c % 2][...] = jnp.exp2(m_old - m_new)
        p_bufs[c % 2][...] = jnp.exp2(s - _tile_lanes(m_new, tk // LANES)).astype(BF16)
        m_ref[...] = m_new

    def stage_c(c):
        pv = _pv_with_denominator(p_bufs[c % 2][...], v_ref[c * tk:(c + 1) * tk, :], lo, hi)
        acc_ref[...] = a_bufs[c % 2][...] * acc_ref[...] + pv

    for t in range(nk + 2):
        if t < nk:
            stage_a(t)
        if 1 <= t <= nk:
            stage_b(t - 1)
        if t >= 2:
            stage_c(t - 2)
    _normalize_store(acc_ref[...], o_ref, slabs, tq, lane_lo)


def _global_attention(q, kcat, vcat):
    tq, tk = 256, 768
    t = kcat.shape[1]
    per_batch = DEC_SEQ // tq
    rows = N_HEADS_A * tq
    kv_spec = pl.BlockSpec((None, t, LANES), lambda b, i: (b, 0, 0))
    return pl.pallas_call(
        functools.partial(_global_attn_kernel, tk=tk),
        grid=(DEC_BATCH, per_batch),
        in_specs=[pl.BlockSpec((tq, 4 * LANES), lambda b, i: (b * per_batch + i, 0)), kv_spec, kv_spec],
        out_specs=pl.BlockSpec((tq, 4 * LANES), lambda b, i: (b * per_batch + i, 0)),
        out_shape=jax.ShapeDtypeStruct((S_TOK, 4 * LANES), BF16),
        scratch_shapes=[pltpu.VMEM((rows, tk), F32), pltpu.VMEM((rows, tk), F32),
                        pltpu.VMEM((rows, tk), BF16), pltpu.VMEM((rows, tk), BF16),
                        pltpu.VMEM((rows, LANES), F32), pltpu.VMEM((rows, LANES), F32),
                        pltpu.VMEM((rows, LANES), F32), pltpu.VMEM((rows, LANES), F32)],
        compiler_params=_cparams(("parallel", "parallel")),
        name="global_attn",
    )(q, kcat, vcat)


def _window_attn_kernel(sink_ref, q_ref, k_ref, v_ref, o_ref):
    i = pl.program_id(1)
    tq = q_ref.shape[0]
    slabs = tuple(range(q_ref.shape[1] // LANES))
    n_heads = 2 * len(slabs)
    span = tq + 2 * WINDOW
    lo, hi = _half_masks(BF16)
    lane_lo = lax.broadcasted_iota(jnp.int32, (tq, LANES), 1) < HEAD_DIM
    off = pl.multiple_of(PAST_LEN - WINDOW + i * tq, LANES)
    k_loc = k_ref[pl.ds(off, span), :]
    v_loc = v_ref[pl.ds(off, span), :]
    row = lax.broadcasted_iota(jnp.int32, (tq, span), 0)
    col = lax.broadcasted_iota(jnp.int32, (tq, span), 1)
    kpos = i * tq - WINDOW + col
    valid = (col >= row) & (col <= row + 2 * WINDOW) & (kpos >= 0) & (kpos < DEC_SEQ)
    qstack = _stack_heads(q_ref, slabs, lo, hi)
    s_loc = lax.dot_general(qstack, k_loc, _NT, preferred_element_type=F32)
    s_loc = jnp.where(valid[None], s_loc.reshape(n_heads, tq, span), NEG_BIG).reshape(n_heads * tq, span)
    s_ctx = lax.dot_general(qstack, k_ref[0:PAST_LEN, :], _NT, preferred_element_type=F32)
    sink = _sink_rows(sink_ref, range(n_heads), tq)
    o = _softmax_pv([(s_loc, v_loc), (s_ctx, v_ref[0:PAST_LEN, :])], sink, lo, hi)
    _normalize_store(o, o_ref, slabs, tq, lane_lo)


def _window_attention(q, kcat, vcat, sink):
    tq = 128
    t = kcat.shape[1]
    per_batch = DEC_SEQ // tq
    return pl.pallas_call(
        _window_attn_kernel,
        grid=(DEC_BATCH, per_batch),
        in_specs=[
            pl.BlockSpec(memory_space=pltpu.SMEM),
            pl.BlockSpec((tq, 4 * LANES), lambda b, i: (b * per_batch + i, 1)),
            pl.BlockSpec((None, t, LANES), lambda b, i: (b, 0, 0)),
            pl.BlockSpec((None, t, LANES), lambda b, i: (b, 0, 0)),
        ],
        out_specs=pl.BlockSpec((tq, 4 * LANES), lambda b, i: (b * per_batch + i, 0)),
        out_shape=jax.ShapeDtypeStruct((S_TOK, 4 * LANES), BF16),
        compiler_params=_cparams(("parallel", "parallel")),
        name="window_attn",
    )(sink, q, kcat, vcat)


def _na_start_row(g):
    return jnp.clip(NA_QROWS * g - NA_ROWS // 2, 0, DEC_SEQ // GRID_W - NA_KROWS)


def _na_step_bias(t_ref, half, g, lane_lo64):
    rows = DEC_SEQ // GRID_W
    start = _na_start_row(g)
    d0 = start - NA_QROWS * g + NA_ROWS - 1
    row_blocks = []
    for qr in range(NA_QROWS):
        rs = jnp.clip(NA_QROWS * g + qr - NA_ROWS // 2, 0, rows - NA_ROWS)
        blocks = []
        for m in range(NA_KROWS // 2):
            tiles = []
            for kr in (2 * m, 2 * m + 1):
                krow = start + kr
                inside = (krow >= rs) & (krow < rs + NA_ROWS)
                dr = jnp.clip(kr - qr + d0, 0, 2 * NA_ROWS - 2)
                tiles.append(t_ref[half, dr] + jnp.where(inside, 0.0, NEG_BIG))
            blocks.append(jnp.where(lane_lo64, tiles[0], tiles[1]))
        row_blocks.append(jnp.concatenate(blocks, axis=1))
    return jnp.concatenate(row_blocks, axis=0)


def _na_attn_kernel(q_ref, k_ref, v_ref, kc_ref, vc_ref, t_ref, o_ref):
    g = pl.program_id(2)
    tq = q_ref.shape[0]
    lo, hi = _half_masks(BF16)
    lane_lo = lax.broadcasted_iota(jnp.int32, (tq, LANES), 1) < HEAD_DIM
    lane_lo64 = lax.broadcasted_iota(jnp.int32, (GRID_W, LANES), 1) < HEAD_DIM
    off = pl.multiple_of(_na_start_row(g) * GRID_W, GRID_W)
    n_slabs = q_ref.shape[1] // LANES
    cols = [slice(n * LANES, (n + 1) * LANES) for n in range(n_slabs)]

    def scores(n):
        qstack = _stack_heads(q_ref, (n,), lo, hi)
        bias = jnp.concatenate([_na_step_bias(t_ref, 2 * n + half, g, lane_lo64) for half in range(2)], axis=0)
        k_loc = k_ref[pl.ds(off, NA_KROWS * GRID_W), cols[n]]
        s_loc = lax.dot_general(qstack, k_loc, _NT, preferred_element_type=F32) + bias
        return [s_loc, lax.dot_general(qstack, kc_ref[:, cols[n]], _NT, preferred_element_type=F32)]

    def probs(s_parts):
        m = jnp.full((2 * tq, LANES), NEG_BIG, F32)
        for s in s_parts:
            m = jnp.maximum(m, s.max(axis=1, keepdims=True))
        return [jnp.exp2(s - _tile_lanes(m, s.shape[1] // LANES)).astype(BF16) for s in s_parts]

    def output(n, p_parts):
        v_loc = v_ref[pl.ds(off, NA_KROWS * GRID_W), cols[n]]
        o = _pv_with_denominator(p_parts[0], v_loc, lo, hi) + _pv_with_denominator(p_parts[1], vc_ref[:, cols[n]], lo, hi)
        _normalize_store(o, o_ref, (n,), tq, lane_lo)

    s_live, p_live = {}, {}
    for t in range(n_slabs + 2):
        if t < n_slabs:
            s_live[t] = scores(t)
        if 1 <= t <= n_slabs:
            p_live[t - 1] = probs(s_live.pop(t - 1))
        if t >= 2:
            output(t - 2, p_live.pop(t - 2))


def _na_bias_table(rpb):
    cols = np.arange(GRID_W)
    cs = np.clip(cols - NA_COLS // 2, 0, GRID_W - NA_COLS)
    dc = cols[None, :] - cols[:, None] + NA_COLS - 1
    inside = (cols[None, :] >= cs[:, None]) & (cols[None, :] < cs[:, None] + NA_COLS)
    onehot = (np.arange(2 * NA_COLS - 1)[:, None, None] == dc[None]) & inside[None]
    t = jnp.einsum("hdc,cqk->hdqk", rpb.astype(F32), jnp.asarray(onehot, F32), precision=lax.Precision.HIGHEST)
    t = jnp.where(inside[None, None], t * LOG2E, NEG_BIG)
    return jnp.concatenate([t, t], axis=-1)


def _na_attention(q, k, v, kc, vc, bias):
    tq = NA_QROWS * GRID_W
    n_groups = DEC_SEQ // tq
    w = NA_SLABS_PER_STEP * LANES
    k3 = k.reshape(DEC_BATCH, DEC_SEQ, D_MODEL)
    v3 = v.reshape(DEC_BATCH, DEC_SEQ, D_MODEL)
    tok = lambda s, b, g: (b * n_groups + g, s)
    per_batch = lambda s, b, g: (b, 0, s)
    return pl.pallas_call(
        _na_attn_kernel,
        grid=(D_MODEL // w, DEC_BATCH, n_groups),
        in_specs=[
            pl.BlockSpec((tq, w), tok),
            pl.BlockSpec((None, DEC_SEQ, w), per_batch),
            pl.BlockSpec((None, DEC_SEQ, w), per_batch),
            pl.BlockSpec((None, PAST_LEN, w), per_batch),
            pl.BlockSpec((None, PAST_LEN, w), per_batch),
            pl.BlockSpec((2 * NA_SLABS_PER_STEP, 2 * NA_ROWS - 1, GRID_W, LANES), lambda s, b, g: (s, 0, 0, 0)),
        ],
        out_specs=pl.BlockSpec((tq, w), tok),
        out_shape=jax.ShapeDtypeStruct((S_TOK, D_MODEL), BF16),
        compiler_params=_cparams(("parallel", "parallel", "parallel")),
        name="na_attn",
    )(q, k3, v3, kc, vc, bias)


def _layer_norm(z, g, b):
    mu = jnp.mean(z, axis=-1, keepdims=True)
    zc = z - mu
    var = jnp.mean(zc * zc, axis=-1, keepdims=True)
    return zc * lax.rsqrt(var + LN_EPS) * g + b


def _post_kernel(*refs, n_parts):
    o_refs = refs[:n_parts]
    (y_ref, mod_ref, w_ref, lng_ref, lnb_ref, wr_hi_ref, wr_lo_ref, br_ref,
     y_out_ref, h_out_ref, xg_ref, slot_ref, gate_ref, spill_ref) = refs[n_parts:]
    pw = D_MODEL // n_parts
    mix = None
    for p, o_ref in enumerate(o_refs):
        part = jnp.dot(o_ref[...], w_ref[p * pw:(p + 1) * pw, :], preferred_element_type=F32)
        mix = part if mix is None else mix + part
    y = _layer_norm(ALPHA * y_ref[...] + mod_ref[2] * mix, lng_ref[...], lnb_ref[...])
    y_out_ref[...] = y
    h = y * (1.0 + mod_ref[4]) + mod_ref[3]
    h_hi = h.astype(BF16)
    h_out_ref[...] = h_hi
    h_lo = (h - h_hi.astype(F32)).astype(BF16)
    logits = lax.dot_general(wr_hi_ref[...], h_hi, _NT, preferred_element_type=F32)
    logits += lax.dot_general(wr_hi_ref[...], h_lo, _NT, preferred_element_type=F32)
    logits += lax.dot_general(wr_lo_ref[...], h_hi, _NT, preferred_element_type=F32)
    scores = 1.0 / (1.0 + jnp.exp(-logits))
    work = scores + br_ref[...]
    expert = lax.broadcasted_iota(jnp.int32, work.shape, 0).astype(F32)
    chosen = jnp.zeros(work.shape, F32)
    sel = jnp.zeros(work.shape, F32)
    for _ in range(TOP_K):
        mx = work.max(axis=0, keepdims=True)
        first = jnp.where(work == mx, expert, float(N_EXPERTS)).min(axis=0, keepdims=True)
        pick = expert == first
        chosen = jnp.where(pick, scores, chosen)
        sel = jnp.where(pick, 1.0, sel)
        work = jnp.where(pick, NEG_BIG, work)
    gates_t = chosen / chosen.sum(axis=0, keepdims=True) * ROUTED_SCALE

    tm = h.shape[0]
    earlier = lax.broadcasted_iota(jnp.int32, (tm, tm), 0) < lax.broadcasted_iota(jnp.int32, (tm, tm), 1)
    rank = jnp.dot(sel.astype(BF16), jnp.where(earlier, 1.0, 0.0).astype(BF16), preferred_element_type=F32)
    kept = jnp.where(rank < float(MOE_CAP), sel, 0.0)
    slot = jnp.where(kept > 0.0, rank, -1.0)
    slot_ref[...] = slot
    gate_ref[...] = gates_t * kept
    spilled = gates_t * (sel - kept)
    padded = jnp.concatenate([spilled, jnp.zeros((LANES - N_EXPERTS, tm), F32)], axis=0)
    spill_ref[...] = padded.T[:, :N_EXPERTS]
    c = lax.broadcasted_iota(jnp.int32, (MOE_CAP, tm), 0).astype(F32)
    p_rows = jnp.concatenate(
        [jnp.where(c == slot[e:e + 1, :], 1.0, 0.0).astype(BF16) for e in range(N_EXPERTS)], axis=0)
    half = D_MODEL // 2
    for cols in (slice(0, half), slice(half, D_MODEL)):
        xg = jnp.dot(p_rows, h_hi[:, cols], preferred_element_type=F32).astype(BF16)
        xg_ref[:, :, cols] = xg.reshape(N_EXPERTS, MOE_CAP, half)


def _post_mixer(o_parts, y2d, mod, w_out_bf16, ln_g, ln_b, wr_hi, wr_lo, b_router, *, latent):
    n = y2d.shape[0]
    n_parts = len(o_parts)
    pw = D_MODEL // n_parts
    TM = TM_ROUTE
    per_batch = DEC_SEQ // TM
    mod_map = (lambda i: (0, 1 + i // per_batch, 0, 0)) if latent else (lambda i: (0, 0, 0, 0))
    o_specs = [pl.BlockSpec((TM, pw), functools.partial(lambda i, c: (i, c), c=col)) for _, col in o_parts]
    full = lambda i: (0, 0)
    return pl.pallas_call(
        functools.partial(_post_kernel, n_parts=n_parts),
        grid=(n // TM,),
        in_specs=o_specs + [
            pl.BlockSpec((TM, D_MODEL), lambda i: (i, 0)),
            pl.BlockSpec((6, None, 1, D_MODEL), mod_map),
            pl.BlockSpec((D_MODEL, D_MODEL), full),
            pl.BlockSpec((1, D_MODEL), full),
            pl.BlockSpec((1, D_MODEL), full),
            pl.BlockSpec((N_EXPERTS, D_MODEL), full),
            pl.BlockSpec((N_EXPERTS, D_MODEL), full),
            pl.BlockSpec((N_EXPERTS, 1), full),
        ],
        out_specs=[
            pl.BlockSpec((TM, D_MODEL), lambda i: (i, 0)),
            pl.BlockSpec((TM, D_MODEL), lambda i: (i, 0)),
            pl.BlockSpec((None, N_EXPERTS, MOE_CAP, D_MODEL), lambda i: (i, 0, 0, 0)),
            pl.BlockSpec((N_EXPERTS, TM), lambda i: (0, i)),
            pl.BlockSpec((N_EXPERTS, TM), lambda i: (0, i)),
            pl.BlockSpec((TM, N_EXPERTS), lambda i: (i, 0)),
        ],
        out_shape=[
            jax.ShapeDtypeStruct((n, D_MODEL), F32),
            jax.ShapeDtypeStruct((n, D_MODEL), BF16),
            jax.ShapeDtypeStruct((n // TM, N_EXPERTS, MOE_CAP, D_MODEL), BF16),
            jax.ShapeDtypeStruct((N_EXPERTS, n), F32),
            jax.ShapeDtypeStruct((N_EXPERTS, n), F32),
            jax.ShapeDtypeStruct((n, N_EXPERTS), F32),
        ],
        compiler_params=_cparams(("parallel",)),
        name="post_mixer",
    )(*[a for a, _ in o_parts], y2d, mod, w_out_bf16, ln_g, ln_b, wr_hi, wr_lo, b_router)


def _silu(x):
    return x / (1.0 + jnp.exp(-x))


def _expert_kernel(x_ref, wg_ref, wu_ref, wd_ref, o_ref):
    tiles, cap, d = x_ref.shape
    w1 = jnp.concatenate([wg_ref[...].astype(BF16), wu_ref[...].astype(BF16)], axis=1)
    w2 = wd_ref[...].astype(BF16)
    chunk = 8
    n_chunks = tiles // chunk

    def up(c):
        x = x_ref[c * chunk:(c + 1) * chunk].reshape(chunk * cap, d)
        return jnp.dot(x, w1, preferred_element_type=F32)

    def act(hcat):
        return (_silu(hcat[:, :D_EXPERT]) * hcat[:, D_EXPERT:]).astype(BF16)

    def down(c, a):
        y = jnp.dot(a, w2, preferred_element_type=F32)
        o_ref[c * chunk:(c + 1) * chunk] = y.astype(BF16).reshape(chunk, cap, d)

    h_live, a_live = {}, {}
    for t in range(n_chunks + 2):
        if t < n_chunks:
            h_live[t] = up(t)
        if 1 <= t <= n_chunks:
            a_live[t - 1] = act(h_live.pop(t - 1))
        if t >= 2:
            down(t - 2, a_live.pop(t - 2))


def _routed_experts(xg, w_gate, w_up, w_down, *, layer):
    tiles = xg.shape[0]
    slots = pl.BlockSpec((tiles, None, MOE_CAP, D_MODEL), lambda e: (0, e, 0, 0))
    return pl.pallas_call(
        _expert_kernel,
        grid=(N_EXPERTS,),
        in_specs=[
            slots,
            pl.BlockSpec((None, None, D_MODEL, D_EXPERT), lambda e: (layer, e, 0, 0)),
            pl.BlockSpec((None, None, D_MODEL, D_EXPERT), lambda e: (layer, e, 0, 0)),
            pl.BlockSpec((None, None, D_EXPERT, D_MODEL), lambda e: (layer, e, 0, 0)),
        ],
        out_specs=slots,
        out_shape=jax.ShapeDtypeStruct(xg.shape, BF16),
        compiler_params=_cparams(("parallel",)),
        name="routed_experts",
    )(xg, w_gate, w_up, w_down)


def _combine_kernel(*refs, with_spill):
    if with_spill:
        yg_ref, slot_ref, gate_ref, x_ref, y_ref, mod_ref, sg_ref, su_ref, sd_ref, lng_ref, lnb_ref, f_ref, o_ref = refs
    else:
        yg_ref, slot_ref, gate_ref, x_ref, y_ref, mod_ref, sg_ref, su_ref, sd_ref, lng_ref, lnb_ref, o_ref = refs
    tm = x_ref.shape[0]
    slot = slot_ref[...]
    gate = gate_ref[...]
    c = lax.broadcasted_iota(jnp.int32, (MOE_CAP, tm), 0).astype(F32)
    g_rows = jnp.concatenate(
        [jnp.where(c == slot[e:e + 1, :], gate[e:e + 1, :], 0.0) for e in range(N_EXPERTS)], axis=0)
    g = g_rows.T.astype(BF16)
    f = jnp.dot(g, yg_ref[...].reshape(N_EXPERTS * MOE_CAP, D_MODEL), preferred_element_type=F32)
    x = x_ref[...]
    a = _silu(jnp.dot(x, sg_ref[...].astype(BF16), preferred_element_type=F32))
    a = a * jnp.dot(x, su_ref[...].astype(BF16), preferred_element_type=F32)
    f = f + jnp.dot(a.astype(BF16), sd_ref[...].astype(BF16), preferred_element_type=F32)
    if with_spill:
        f = f + f_ref[...]
    o_ref[...] = _layer_norm(ALPHA * y_ref[...] + mod_ref[5] * f, lng_ref[...], lnb_ref[...])


def _combine(yg, slot, gate, x_bf16, y2d, mod, ws_gate, ws_up, ws_down, ln_g, ln_b, f_spill, *, layer, latent):
    n = x_bf16.shape[0]
    TM = TM_ROUTE
    per_batch = DEC_SEQ // TM
    mod_map = (lambda i: (0, 1 + i // per_batch, 0, 0)) if latent else (lambda i: (0, 0, 0, 0))
    tok = lambda i: (i, 0)
    full = lambda i: (0, 0)
    shared = lambda i: (layer, 0, 0)
    with_spill = f_spill is not None
    in_specs = [
        pl.BlockSpec((None, N_EXPERTS, MOE_CAP, D_MODEL), lambda i: (i, 0, 0, 0)),
        pl.BlockSpec((N_EXPERTS, TM), lambda i: (0, i)),
        pl.BlockSpec((N_EXPERTS, TM), lambda i: (0, i)),
        pl.BlockSpec((TM, D_MODEL), tok),
        pl.BlockSpec((TM, D_MODEL), tok),
        pl.BlockSpec((6, None, 1, D_MODEL), mod_map),
        pl.BlockSpec((None, D_MODEL, D_EXPERT), shared),
        pl.BlockSpec((None, D_MODEL, D_EXPERT), shared),
        pl.BlockSpec((None, D_EXPERT, D_MODEL), shared),
        pl.BlockSpec((1, D_MODEL), full),
        pl.BlockSpec((1, D_MODEL), full),
    ]
    args = [yg, slot, gate, x_bf16, y2d, mod, ws_gate, ws_up, ws_down, ln_g, ln_b]
    if with_spill:
        in_specs.append(pl.BlockSpec((TM, D_MODEL), tok))
        args.append(f_spill)
    return pl.pallas_call(
        functools.partial(_combine_kernel, with_spill=with_spill),
        grid=(n // TM,),
        in_specs=in_specs,
        out_specs=pl.BlockSpec((TM, D_MODEL), tok),
        out_shape=jax.ShapeDtypeStruct((n, D_MODEL), F32),
        compiler_params=_cparams(("parallel",)),
        name="moe_combine",
    )(*args)


def _spill_kernel(x_ref, gates_ref, wg_ref, wu_ref, wd_ref, o_ref):
    e = pl.program_id(1)
    x = x_ref[...]

    @pl.when(e == 0)
    def _():
        o_ref[...] = jnp.zeros(o_ref.shape, F32)

    gates = gates_ref[...]
    lane = lax.broadcasted_iota(jnp.int32, gates.shape, 1)
    for pair in range(EXPERTS_PER_STEP // 2):
        acts = []
        for j in (2 * pair, 2 * pair + 1):
            w1 = jnp.concatenate([wg_ref[j].astype(BF16), wu_ref[j].astype(BF16)], axis=1)
            hcat = jnp.dot(x, w1, preferred_element_type=F32)
            ge = jnp.sum(jnp.where(lane == e * EXPERTS_PER_STEP + j, gates, 0.0), axis=1, keepdims=True)
            acts.append((_silu(hcat[:, :D_EXPERT]) * hcat[:, D_EXPERT:] * ge).astype(BF16))
        w2 = jnp.concatenate([wd_ref[2 * pair].astype(BF16), wd_ref[2 * pair + 1].astype(BF16)], axis=0)
        o_ref[...] += jnp.dot(jnp.concatenate(acts, axis=1), w2, preferred_element_type=F32)


def _spilled_experts(x_bf16, gates, w_gate, w_up, w_down, *, layer):
    n = x_bf16.shape[0]
    ep = EXPERTS_PER_STEP
    tok = lambda i, e: (i, 0)
    routed = lambda i, e: (layer, e, 0, 0)
    return pl.pallas_call(
        _spill_kernel,
        grid=(n // TM_MOE, N_EXPERTS // ep),
        in_specs=[
            pl.BlockSpec((TM_MOE, D_MODEL), tok),
            pl.BlockSpec((TM_MOE, N_EXPERTS), tok),
            pl.BlockSpec((None, ep, D_MODEL, D_EXPERT), routed),
            pl.BlockSpec((None, ep, D_MODEL, D_EXPERT), routed),
            pl.BlockSpec((None, ep, D_EXPERT, D_MODEL), routed),
        ],
        out_specs=pl.BlockSpec((TM_MOE, D_MODEL), tok),
        out_shape=jax.ShapeDtypeStruct((n, D_MODEL), F32),
        compiler_params=_cparams(("parallel", "arbitrary")),
        name="moe_spill",
    )(x_bf16, gates, w_gate, w_up, w_down)


def _moe(xg, slot, gate, spill, x_bf16, y2d, mod, w_gate, w_up, w_down, ws_gate, ws_up, ws_down, ln_g, ln_b,
         *, layer, latent):
    yg = _routed_experts(xg, w_gate, w_up, w_down, layer=layer)
    combine = functools.partial(_combine, yg, slot, gate, x_bf16, y2d, mod, ws_gate, ws_up, ws_down, ln_g, ln_b,
                                layer=layer, latent=latent)
    return lax.cond(
        jnp.any(spill != 0.0),
        lambda: combine(_spilled_experts(x_bf16, spill, w_gate, w_up, w_down, layer=layer)),
        lambda: combine(None))


def _slab_perm():
    idx = []
    for j in range(4):
        for half in range(2):
            head = j + 4 * half
            idx.extend(range(head * HEAD_DIM, (head + 1) * HEAD_DIM))
    return np.asarray(idx, np.int32)


def _rope_tables():
    t = np.arange(DEC_SEQ)
    quarter = HEAD_DIM // 4
    inv = ROPE_THETA ** (-np.arange(quarter, dtype=np.float64) / quarter)
    ar = (t // GRID_W)[:, None] * inv
    ac = (t % GRID_W)[:, None] * inv
    ang = np.concatenate([ar, ar, ac, ac] * 2, axis=-1)
    sign = np.where((np.arange(LANES) % 32) < 16, -1.0, 1.0)
    return jnp.asarray(np.cos(ang), F32), jnp.asarray(np.sin(ang) * sign, F32)


def kernel(x_prompt, x_sample, cache_ka, cache_va, cache_kb, cache_vb, cache_kc, cache_vc, c, c_ctx, w_ada, b_ada, ln_g, ln_b, w_in_even, w_out_even, qnorm_a, knorm_a, sink_b, w_in_odd, w_out_odd, rpb_c, w_router, b_router, w_gate, w_up, w_down, ws_gate, ws_up, ws_down):
    D = D_MODEL
    y_p = x_prompt.reshape(P_TOK, D)
    y_s = x_sample.reshape(S_TOK, D)

    cond8 = jnp.concatenate([c_ctx[None, :], c, jnp.zeros((8 - 1 - DEC_BATCH, D), F32)], axis=0)
    ada = _adaln(cond8, w_ada, b_ada)
    mods = ada.reshape(DEPTH, 8, 6, 1, D).transpose(0, 2, 1, 3, 4)

    cos, sin_signed = _rope_tables()
    perm = _slab_perm()
    in_perm = np.concatenate([perm, 512 + perm, np.arange(1024, EVEN_IN)])
    out_perm = np.concatenate([perm, 512 + perm])
    ones_row = jnp.ones((1, LANES), F32)
    no_sink = jnp.zeros((N_HEADS_B,), F32)

    new = {}
    for l in range(DEPTH):
        mod = mods[l]
        if l % 2 == 0:
            e = l // 2
            w_in = w_in_even[e][:, in_perm].astype(BF16)
            w_out = w_out_even[e][out_perm, :].astype(BF16)
            qn2 = jnp.tile(qnorm_a[e], 2)[None, :]
            kn2 = jnp.tile(knorm_a[e], 2)[None, :]
            proj = functools.partial(_project, w_bf16=w_in, qn2=qn2, kn2=kn2, cos=cos, sin_signed=sin_signed,
                                     dq=1024, dk=256, dv=256, q_norm=512, k_norm=128)
            q_p, k_p, v_p = proj(y_p, mod, rope=False, latent=False)
            q_s, k_s, v_s = proj(y_s, mod, rope=True, latent=True)
            new["ka"], new["kb"] = k_p[:, :LANES], k_p[:, LANES:]
            new["va"], new["vb"] = v_p[:, :LANES], v_p[:, LANES:]
            sink = sink_b[e].astype(F32)
            groups = (((0, 1, 2, 3), 0, None), ((4, 5, 6, 7), 1, tuple(range(N_HEADS_B))))
            o_p = _ctx_attention(q_p, k_p, v_p, sink, groups=groups)

            k_s3 = k_s.reshape(DEC_BATCH, DEC_SEQ, 2 * LANES)
            v_s3 = v_s.reshape(DEC_BATCH, DEC_SEQ, 2 * LANES)
            cka = cache_ka[:, e].reshape(DEC_BATCH, PAST_LEN, LANES).astype(BF16)
            cva = cache_va[:, e].reshape(DEC_BATCH, PAST_LEN, LANES).astype(BF16)
            ckb = cache_kb[:, e].reshape(DEC_BATCH, PAST_LEN, LANES).astype(BF16)
            cvb = cache_vb[:, e].reshape(DEC_BATCH, PAST_LEN, LANES).astype(BF16)
            pad = jnp.zeros((DEC_BATCH, WINDOW, LANES), BF16)
            o_a = _global_attention(q_s, jnp.concatenate([cka, k_s3[:, :, :LANES]], axis=1),
                                    jnp.concatenate([cva, v_s3[:, :, :LANES]], axis=1))
            o_b = _window_attention(q_s, jnp.concatenate([ckb, k_s3[:, :, LANES:], pad], axis=1),
                                    jnp.concatenate([cvb, v_s3[:, :, LANES:], pad], axis=1), sink)
            parts_p = [(o_p, 0), (o_p, 1)]
            parts_s = [(o_a, 0), (o_b, 0)]
        else:
            o = l // 2
            w_in = w_in_odd[o].astype(BF16)
            w_out = w_out_odd[o].astype(BF16)
            proj = functools.partial(_project, w_bf16=w_in, qn2=ones_row, kn2=ones_row, cos=cos,
                                     sin_signed=sin_signed, dq=1024, dk=1024, dv=1024, q_norm=0, k_norm=0,
                                     rope=False)
            q_p, k_p, v_p = proj(y_p, mod, latent=False)
            q_s, k_s, v_s = proj(y_s, mod, latent=True)
            new["kc"], new["vc"] = k_p, v_p
            groups = tuple(((j,), j, None) for j in range(D // LANES))
            o_p = _ctx_attention(q_p, k_p, v_p, no_sink, groups=groups)
            kc = cache_kc[:, o].reshape(DEC_BATCH, PAST_LEN, D).astype(BF16)
            vc = cache_vc[:, o].reshape(DEC_BATCH, PAST_LEN, D).astype(BF16)
            o_s = _na_attention(q_s, k_s, v_s, kc, vc, _na_bias_table(rpb_c[o]))
            parts_p = [(o_p, 0)]
            parts_s = [(o_s, 0)]

        wr = w_router[l].T
        wr_hi = wr.astype(BF16)
        wr_lo = (wr - wr_hi.astype(F32)).astype(BF16)
        post = functools.partial(_post_mixer, mod=mod, w_out_bf16=w_out, ln_g=ln_g[l, 0][None, :],
                                 ln_b=ln_b[l, 0][None, :], wr_hi=wr_hi, wr_lo=wr_lo, b_router=b_router[l][:, None])
        y_p, h_p, *route_p = post(parts_p, y_p, latent=False)
        y_s, h_s, *route_s = post(parts_s, y_s, latent=True)
        moe = functools.partial(_moe, mod=mod, w_gate=w_gate, w_up=w_up, w_down=w_down,
                                ws_gate=ws_gate, ws_up=ws_up, ws_down=ws_down,
                                ln_g=ln_g[l, 1][None, :], ln_b=ln_b[l, 1][None, :], layer=l)
        y_p = moe(*route_p, h_p, y_p, latent=False)
        y_s = moe(*route_s, h_s, y_s, latent=True)

    kv_a = (BATCH, 1, SEQ, N_KV_A, HEAD_DIM)
    kv_c = (BATCH, 1, SEQ, N_HEADS_C, HEAD_DIM)
    return (y_p.reshape(BATCH, SEQ, D), y_s.reshape(DEC_BATCH, DEC_SEQ, D),
            new["ka"].reshape(kv_a), new["va"].reshape(kv_a), new["kb"].reshape(kv_a), new["vb"].reshape(kv_a),
            new["kc"].reshape(kv_c), new["vc"].reshape(kv_c))
```

```python
import functools
import math

import numpy as np
import jax
import jax.numpy as jnp
from jax import lax
from jax.experimental import pallas as pl
from jax.experimental.pallas import tpu as pltpu

F32 = jnp.float32
BF16 = jnp.bfloat16

D_MODEL = 1024
BATCH = 16
SEQ = 256
DEPTH = 2
DEC_BATCH = 2
DEC_SEQ = 4096
PAST_LEN = 512
GRID_W = 64
HEAD_DIM = 64
N_HEADS_A = 8
N_KV_A = 2
N_HEADS_B = 8
N_KV_B = 2
N_HEADS_C = 16
EVEN_IN = 1536
ODD_IN = 3072
WINDOW = 128
NA_ROWS = 8
NA_COLS = 16
ROPE_THETA = 10000.0
N_EXPERTS = 64
TOP_K = 8
D_EXPERT = 128
ROUTED_SCALE = 2.5
ALPHA = (2 * DEPTH) ** 0.25
LN_EPS = 1e-6
RMS_EPS = 1e-6
NEG_BIG = -1e30
LOG2E = math.log2(math.e)

LANES = 128
P_TOK = BATCH * SEQ
S_TOK = DEC_BATCH * DEC_SEQ
TM = 512
TM_ROUTE = 256
MOE_CAP = 48
NA_QROWS = 4
NA_KROWS = 12
NA_SLABS_PER_STEP = 4
CTX_SEQS_PER_STEP = 2
VMEM_LIMIT = 56 * 1024 * 1024

_NT = (((1,), (1,)), ((), ()))


def _cparams(sem):
    return pltpu.CompilerParams(dimension_semantics=sem, vmem_limit_bytes=VMEM_LIMIT)


def _half_masks(dtype):
    lane = lax.broadcasted_iota(jnp.int32, (1, LANES), 1)
    lo = jnp.where(lane < HEAD_DIM, 1.0, 0.0).astype(dtype)
    hi = jnp.where(lane < HEAD_DIM, 0.0, 1.0).astype(dtype)
    return lo, hi


def _ada_kernel(c_ref, w_ref, b_ref, o_ref):
    c = c_ref[...]
    a = c / (1.0 + jnp.exp(-c))
    w = w_ref[0]
    a_hi = a.astype(BF16)
    a_lo = (a - a_hi.astype(F32)).astype(BF16)
    w_hi = w.astype(BF16)
    w_lo = (w - w_hi.astype(F32)).astype(BF16)
    acc = jnp.dot(a_hi, w_hi, preferred_element_type=F32)
    acc += jnp.dot(a_lo, w_hi, preferred_element_type=F32)
    acc += jnp.dot(a_hi, w_lo, preferred_element_type=F32)
    o_ref[0] = acc + b_ref[0]


def _adaln(cond8, w_ada, b_ada):
    tn = 1536
    return pl.pallas_call(
        _ada_kernel,
        grid=(DEPTH, 6 * D_MODEL // tn),
        in_specs=[
            pl.BlockSpec((8, D_MODEL), lambda l, j: (0, 0)),
            pl.BlockSpec((1, D_MODEL, tn), lambda l, j: (l, 0, j)),
            pl.BlockSpec((1, 1, tn), lambda l, j: (l, 0, j)),
        ],
        out_specs=pl.BlockSpec((1, 8, tn), lambda l, j: (l, 0, j)),
        out_shape=jax.ShapeDtypeStruct((DEPTH, 8, 6 * D_MODEL), F32),
        compiler_params=_cparams(("parallel", "parallel")),
        name="adaln",
    )(cond8, w_ada, b_ada.reshape(DEPTH, 1, 6 * D_MODEL))


def _group_sum_matrix():
    r = lax.broadcasted_iota(jnp.int32, (LANES, LANES), 0) // HEAD_DIM
    c = lax.broadcasted_iota(jnp.int32, (LANES, LANES), 1) // HEAD_DIM
    return jnp.where(r == c, 1.0, 0.0).astype(BF16)


def _rms_slab(t, g, gmat):
    sq = t * t
    hi = sq.astype(BF16)
    lo = (sq - hi.astype(F32)).astype(BF16)
    ss = jnp.dot(hi, gmat, preferred_element_type=F32) + jnp.dot(lo, gmat, preferred_element_type=F32)
    return t * lax.rsqrt(ss * (1.0 / HEAD_DIM) + RMS_EPS) * g


def _rope_slab(t, cos, sin_signed, first):
    r = jnp.where(first, pltpu.roll(t, LANES - 16, 1), pltpu.roll(t, 16, 1))
    return t * cos + r * sin_signed


def _proj_kernel(x_ref, mod_ref, w_ref, qn_ref, kn_ref, cos_ref, sin_ref, q_ref, k_ref, v_ref,
                 *, dq, dk, q_norm, k_norm, rope):
    x = x_ref[...]
    h = (x * (1.0 + mod_ref[1]) + mod_ref[0]).astype(BF16)
    y = jnp.dot(h, w_ref[...], preferred_element_type=F32)
    tm = x.shape[0]
    gmat = _group_sum_matrix() if (q_norm or k_norm) else None
    if rope:
        cos = cos_ref[...]
        sin_signed = sin_ref[...]
        first = (lax.broadcasted_iota(jnp.int32, (tm, LANES), 1) % 32) < 16
    for s in range(dq // LANES):
        t = y[:, s * LANES:(s + 1) * LANES]
        if s * LANES < q_norm:
            t = _rms_slab(t, qn_ref[...], gmat)
        if rope:
            t = _rope_slab(t, cos, sin_signed, first)
        q_ref[:, s * LANES:(s + 1) * LANES] = (t * (HEAD_DIM ** -0.5 * LOG2E)).astype(q_ref.dtype)
    for s in range(dk // LANES):
        t = y[:, dq + s * LANES:dq + (s + 1) * LANES]
        if s * LANES < k_norm:
            t = _rms_slab(t, kn_ref[...], gmat)
        if rope:
            t = _rope_slab(t, cos, sin_signed, first)
        k_ref[:, s * LANES:(s + 1) * LANES] = t.astype(k_ref.dtype)
    v_ref[...] = y[:, dq + dk:].astype(v_ref.dtype)


def _project(x2d, mod, w_bf16, qn2, kn2, cos, sin_signed, *, dq, dk, dv, q_norm, k_norm, rope, latent):
    n = x2d.shape[0]
    per_batch = DEC_SEQ // TM
    if latent:
        mod_map = lambda i: (0, 1 + i // per_batch, 0, 0)
        pos_map = lambda i: (i % per_batch, 0)
        kv_dtype = BF16
    else:
        mod_map = lambda i: (0, 0, 0, 0)
        pos_map = lambda i: (0, 0)
        kv_dtype = F32
    kern = functools.partial(_proj_kernel, dq=dq, dk=dk, q_norm=q_norm, k_norm=k_norm, rope=rope)
    return pl.pallas_call(
        kern,
        grid=(n // TM,),
        in_specs=[
            pl.BlockSpec((TM, D_MODEL), lambda i: (i, 0)),
            pl.BlockSpec((6, None, 1, D_MODEL), mod_map),
            pl.BlockSpec((D_MODEL, dq + dk + dv), lambda i: (0, 0)),
            pl.BlockSpec((1, LANES), lambda i: (0, 0)),
            pl.BlockSpec((1, LANES), lambda i: (0, 0)),
            pl.BlockSpec((TM, LANES), pos_map),
            pl.BlockSpec((TM, LANES), pos_map),
        ],
        out_specs=[
            pl.BlockSpec((TM, dq), lambda i: (i, 0)),
            pl.BlockSpec((TM, dk), lambda i: (i, 0)),
            pl.BlockSpec((TM, dv), lambda i: (i, 0)),
        ],
        out_shape=[
            jax.ShapeDtypeStruct((n, dq), BF16),
            jax.ShapeDtypeStruct((n, dk), kv_dtype),
            jax.ShapeDtypeStruct((n, dv), kv_dtype),
        ],
        compiler_params=_cparams(("parallel",)),
        name="in_proj",
    )(x2d, mod, w_bf16, qn2, kn2, cos, sin_signed)


def _stack_heads(q_ref, slabs, lo, hi):
    qs = [q_ref[:, j * LANES:(j + 1) * LANES] for j in slabs]
    return jnp.concatenate([q * lo for q in qs] + [q * hi for q in qs], axis=0)


def _tile_lanes(x, n):
    return jnp.concatenate([x] * n, axis=1)


def _pv_with_denominator(p, v, lo, hi):
    half = p.shape[0] // 2
    pv_lo = jnp.dot(p[:half], v * lo + hi, preferred_element_type=F32)
    pv_hi = jnp.dot(p[half:], v * hi + lo, preferred_element_type=F32)
    return jnp.concatenate([pv_lo, pv_hi], axis=0)


def _normalize_store(o, o_ref, slabs, tq, lane_lo):
    half = len(slabs) * tq
    for n, j in enumerate(slabs):
        o_lo = o[n * tq:(n + 1) * tq]
        o_hi = o[half + n * tq:half + (n + 1) * tq]
        o_lo = o_lo * (1.0 / pltpu.roll(o_lo, HEAD_DIM, 1))
        o_hi = o_hi * (1.0 / pltpu.roll(o_hi, HEAD_DIM, 1))
        o_ref[:, j * LANES:(j + 1) * LANES] = jnp.where(lane_lo, o_lo, o_hi).astype(o_ref.dtype)


def _sink_rows(sink_ref, heads, tq):
    return jnp.concatenate([jnp.full((tq, LANES), sink_ref[h] * LOG2E, F32) for h in heads], axis=0)


def _softmax_pv(parts, sink, lo, hi):
    rows = parts[0][0].shape[0]
    m = jnp.full((rows, LANES), NEG_BIG, F32) if sink is None else sink
    for s, _ in parts:
        m = jnp.maximum(m, s.max(axis=1, keepdims=True))
    o = None
    for s, v in parts:
        p = jnp.exp2(s - _tile_lanes(m, s.shape[1] // LANES)).astype(BF16)
        pv = _pv_with_denominator(p, v, lo, hi)
        o = pv if o is None else o + pv
    if sink is not None:
        lo32, hi32 = _half_masks(F32)
        e = jnp.exp2(sink - m)
        half = rows // 2
        o = o + jnp.concatenate([e[:half] * hi32, e[half:] * lo32], axis=0)
    return o


def _ctx_attn_kernel(sink_ref, q_ref, k_ref, v_ref, o_ref, *, groups):
    lo, hi = _half_masks(BF16)
    lo32, hi32 = _half_masks(F32)
    lane_lo = lax.broadcasted_iota(jnp.int32, (SEQ, LANES), 1) < HEAD_DIM
    units = [(b, grp) for b in range(q_ref.shape[0] // SEQ) for grp in groups]

    def scores(unit):
        b, (slabs, ks, _) = unit
        kk = k_ref[b * SEQ:(b + 1) * SEQ, ks * LANES:(ks + 1) * LANES].astype(BF16)
        return lax.dot_general(_stack_heads(q_ref.at[b * SEQ:(b + 1) * SEQ], slabs, lo, hi), kk, _NT,
                               preferred_element_type=F32)

    def probs(unit, s):
        sink_heads = unit[1][2]
        if sink_heads is None:
            m = jnp.maximum(jnp.full((s.shape[0], LANES), NEG_BIG, F32), s.max(axis=1, keepdims=True))
            extra = None
        else:
            sink = _sink_rows(sink_ref, sink_heads, SEQ)
            m = jnp.maximum(sink, s.max(axis=1, keepdims=True))
            e = jnp.exp2(sink - m)
            half = s.shape[0] // 2
            extra = jnp.concatenate([e[:half] * hi32, e[half:] * lo32], axis=0)
        return jnp.exp2(s - _tile_lanes(m, s.shape[1] // LANES)).astype(BF16), extra

    def output(unit, p, extra):
        b, (slabs, ks, _) = unit
        vv = v_ref[b * SEQ:(b + 1) * SEQ, ks * LANES:(ks + 1) * LANES].astype(BF16)
        o = _pv_with_denominator(p, vv, lo, hi)
        if extra is not None:
            o = o + extra
        _normalize_store(o, o_ref.at[b * SEQ:(b + 1) * SEQ], slabs, SEQ, lane_lo)

    s_live, p_live = {}, {}
    for t in range(len(units) + 2):
        if t < len(units):
            s_live[t] = scores(units[t])
        if 1 <= t <= len(units):
            p_live[t - 1] = probs(units[t - 1], s_live.pop(t - 1))
        if t >= 2:
            output(units[t - 2], *p_live.pop(t - 2))


def _ctx_attention(q, k, v, sink, *, groups):
    kw = k.shape[1]
    rows = CTX_SEQS_PER_STEP * SEQ
    return pl.pallas_call(
        functools.partial(_ctx_attn_kernel, groups=groups),
        grid=(BATCH // CTX_SEQS_PER_STEP,),
        in_specs=[
            pl.BlockSpec(memory_space=pltpu.SMEM),
            pl.BlockSpec((rows, D_MODEL), lambda b: (b, 0)),
            pl.BlockSpec((rows, kw), lambda b: (b, 0)),
            pl.BlockSpec((rows, kw), lambda b: (b, 0)),
        ],
        out_specs=pl.BlockSpec((rows, D_MODEL), lambda b: (b, 0)),
        out_shape=jax.ShapeDtypeStruct((P_TOK, D_MODEL), BF16),
        compiler_params=_cparams(("parallel",)),
        name="ctx_attn",
    )(sink, q, k, v)


def _global_attn_kernel(q_ref, k_ref, v_ref, o_ref, s0, s1, p0, p1, a0, a1, m_ref, acc_ref, *, tk):
    tq = q_ref.shape[0]
    slabs = tuple(range(q_ref.shape[1] // LANES))
    nk = k_ref.shape[0] // tk
    lo, hi = _half_masks(BF16)
    lane_lo = lax.broadcasted_iota(jnp.int32, (tq, LANES), 1) < HEAD_DIM
    qstack = _stack_heads(q_ref, slabs, lo, hi)
    m_ref[...] = jnp.full(m_ref.shape, NEG_BIG, F32)
    acc_ref[...] = jnp.zeros(acc_ref.shape, F32)
    s_bufs, p_bufs, a_bufs = (s0, s1), (p0, p1), (a0, a1)

    def stage_a(c):
        s_bufs[c % 2][...] = lax.dot_general(qstack, k_ref[c * tk:(c + 1) * tk, :], _NT, preferred_element_type=F32)

    def stage_b(c):
        s = s_bufs[c % 2][...]
        m_old = m_ref[...]
        m_new = jnp.maximum(m_old, s.max(axis=1, keepdims=True))
        a_bufs[c % 2][...] = jnp.exp2(m_old - m_new)
        p_bufs[c % 2][...] = jnp.exp2(s - _tile_lanes(m_new, tk // LANES)).astype(BF16)
        m_ref[...] = m_new

    def stage_c(c):
        pv = _pv_with_denominator(p_bufs[c % 2][...], v_ref[c * tk:(c + 1) * tk, :], lo, hi)
        acc_ref[...] = a_bufs[c % 2][...] * acc_ref[...] + pv

    for t in range(nk + 2):
        if t < nk:
            stage_a(t)
        if 1 <= t <= nk:
            stage_b(t - 1)
        if t >= 2:
            stage_c(t - 2)
    _normalize_store(acc_ref[...], o_ref, slabs, tq, lane_lo)


def _global_attention(q, kcat, vcat):
    tq, tk = 256, 768
    t = kcat.shape[1]
    per_batch = DEC_SEQ // tq
    rows = N_HEADS_A * tq
    kv_spec = pl.BlockSpec((None, t, LANES), lambda b, i: (b, 0, 0))
    return pl.pallas_call(
        functools.partial(_global_attn_kernel, tk=tk),
        grid=(DEC_BATCH, per_batch),
        in_specs=[pl.BlockSpec((tq, 4 * LANES), lambda b, i: (b * per_batch + i, 0)), kv_spec, kv_spec],
        out_specs=pl.BlockSpec((tq, 4 * LANES), lambda b, i: (b * per_batch + i, 0)),
        out_shape=jax.ShapeDtypeStruct((S_TOK, 4 * LANES), BF16),
        scratch_shapes=[pltpu.VMEM((rows, tk), F32), pltpu.VMEM((rows, tk), F32),
                        pltpu.VMEM((rows, tk), BF16), pltpu.VMEM((rows, tk), BF16),
                        pltpu.VMEM((rows, LANES), F32), pltpu.VMEM((rows, LANES), F32),
                        pltpu.VMEM((rows, LANES), F32), pltpu.VMEM((rows, LANES), F32)],
        compiler_params=_cparams(("parallel", "parallel")),
        name="global_attn",
    )(q, kcat, vcat)


def _window_attn_kernel(sink_ref, q_ref, k_ref, v_ref, o_ref):
    i = pl.program_id(1)
    tq = q_ref.shape[0]
    slabs = tuple(range(q_ref.shape[1] // LANES))
    n_heads = 2 * len(slabs)
    span = tq + 2 * WINDOW
    lo, hi = _half_masks(BF16)
    lane_lo = lax.broadcasted_iota(jnp.int32, (tq, LANES), 1) < HEAD_DIM
    off = pl.multiple_of(PAST_LEN - WINDOW + i * tq, LANES)
    k_loc = k_ref[pl.ds(off, span), :]
    v_loc = v_ref[pl.ds(off, span), :]
    row = lax.broadcasted_iota(jnp.int32, (tq, span), 0)
    col = lax.broadcasted_iota(jnp.int32, (tq, span), 1)
    kpos = i * tq - WINDOW + col
    valid = (col >= row) & (col <= row + 2 * WINDOW) & (kpos >= 0) & (kpos < DEC_SEQ)
    qstack = _stack_heads(q_ref, slabs, lo, hi)
    s_loc = lax.dot_general(qstack, k_loc, _NT, preferred_element_type=F32)
    s_loc = jnp.where(valid[None], s_loc.reshape(n_heads, tq, span), NEG_BIG).reshape(n_heads * tq, span)
    s_ctx = lax.dot_general(qstack, k_ref[0:PAST_LEN, :], _NT, preferred_element_type=F32)
    sink = _sink_rows(sink_ref, range(n_heads), tq)
    o = _softmax_pv([(s_loc, v_loc), (s_ctx, v_ref[0:PAST_LEN, :])], sink, lo, hi)
    _normalize_store(o, o_ref, slabs, tq, lane_lo)


def _window_attention(q, kcat, vcat, sink):
    tq = 128
    t = kcat.shape[1]
    per_batch = DEC_SEQ // tq
    return pl.pallas_call(
        _window_attn_kernel,
        grid=(DEC_BATCH, per_batch),
        in_specs=[
            pl.BlockSpec(memory_space=pltpu.SMEM),
            pl.BlockSpec((tq, 4 * LANES), lambda b, i: (b * per_batch + i, 1)),
            pl.BlockSpec((None, t, LANES), lambda b, i: (b, 0, 0)),
            pl.BlockSpec((None, t, LANES), lambda b, i: (b, 0, 0)),
        ],
        out_specs=pl.BlockSpec((tq, 4 * LANES), lambda b, i: (b * per_batch + i, 0)),
        out_shape=jax.ShapeDtypeStruct((S_TOK, 4 * LANES), BF16),
        compiler_params=_cparams(("parallel", "parallel")),
        name="window_attn",
    )(sink, q, kcat, vcat)


def _na_start_row(g):
    return jnp.clip(NA_QROWS * g - NA_ROWS // 2, 0, DEC_SEQ // GRID_W - NA_KROWS)


def _na_step_bias(t_ref, half, g, lane_lo64):
    rows = DEC_SEQ // GRID_W
    start = _na_start_row(g)
    d0 = start - NA_QROWS * g + NA_ROWS - 1
    row_blocks = []
    for qr in range(NA_QROWS):
        rs = jnp.clip(NA_QROWS * g + qr - NA_ROWS // 2, 0, rows - NA_ROWS)
        blocks = []
        for m in range(NA_KROWS // 2):
            tiles = []
            for kr in (2 * m, 2 * m + 1):
                krow = start + kr
                inside = (krow >= rs) & (krow < rs + NA_ROWS)
                dr = jnp.clip(kr - qr + d0, 0, 2 * NA_ROWS - 2)
                tiles.append(t_ref[half, dr] + jnp.where(inside, 0.0, NEG_BIG))
            blocks.append(jnp.where(lane_lo64, tiles[0], tiles[1]))
        row_blocks.append(jnp.concatenate(blocks, axis=1))
    return jnp.concatenate(row_blocks, axis=0)


def _na_attn_kernel(q_ref, k_ref, v_ref, kc_ref, vc_ref, t_ref, o_ref):
    g = pl.program_id(2)
    tq = q_ref.shape[0]
    lo, hi = _half_masks(BF16)
    lane_lo = lax.broadcasted_iota(jnp.int32, (tq, LANES), 1) < HEAD_DIM
    lane_lo64 = lax.broadcasted_iota(jnp.int32, (GRID_W, LANES), 1) < HEAD_DIM
    off = pl.multiple_of(_na_start_row(g) * GRID_W, GRID_W)
    n_slabs = q_ref.shape[1] // LANES
    cols = [slice(n * LANES, (n + 1) * LANES) for n in range(n_slabs)]

    def scores(n):
        qstack = _stack_heads(q_ref, (n,), lo, hi)
        bias = jnp.concatenate([_na_step_bias(t_ref, 2 * n + half, g, lane_lo64) for half in range(2)], axis=0)
        k_loc = k_ref[pl.ds(off, NA_KROWS * GRID_W), cols[n]]
        s_loc = lax.dot_general(qstack, k_loc, _NT, preferred_element_type=F32) + bias
        return [s_loc, lax.dot_general(qstack, kc_ref[:, cols[n]], _NT, preferred_element_type=F32)]

    def probs(s_parts):
        m = jnp.full((2 * tq, LANES), NEG_BIG, F32)
        for s in s_parts:
            m = jnp.maximum(m, s.max(axis=1, keepdims=True))
        return [jnp.exp2(s - _tile_lanes(m, s.shape[1] // LANES)).astype(BF16) for s in s_parts]

    def output(n, p_parts):
        v_loc = v_ref[pl.ds(off, NA_KROWS * GRID_W), cols[n]]
        o = _pv_with_denominator(p_parts[0], v_loc, lo, hi) + _pv_with_denominator(p_parts[1], vc_ref[:, cols[n]], lo, hi)
        _normalize_store(o, o_ref, (n,), tq, lane_lo)

    s_live, p_live = {}, {}
    for t in range(n_slabs + 2):
        if t < n_slabs:
            s_live[t] = scores(t)
        if 1 <= t <= n_slabs:
            p_live[t - 1] = probs(s_live.pop(t - 1))
        if t >= 2:
            output(t - 2, p_live.pop(t - 2))


def _na_bias_table(rpb):
    cols = np.arange(GRID_W)
    cs = np.clip(cols - NA_COLS // 2, 0, GRID_W - NA_COLS)
    dc = cols[None, :] - cols[:, None] + NA_COLS - 1
    inside = (cols[None, :] >= cs[:, None]) & (cols[None, :] < cs[:, None] + NA_COLS)
    onehot = (np.arange(2 * NA_COLS - 1)[:, None, None] == dc[None]) & inside[None]
    t = jnp.einsum("hdc,cqk->hdqk", rpb.astype(F32), jnp.asarray(onehot, F32), precision=lax.Precision.HIGHEST)
    t = jnp.where(inside[None, None], t * LOG2E, NEG_BIG)
    return jnp.concatenate([t, t], axis=-1)


def _na_attention(q, k, v, kc, vc, bias):
    tq = NA_QROWS * GRID_W
    n_groups = DEC_SEQ // tq
    w = NA_SLABS_PER_STEP * LANES
    k3 = k.reshape(DEC_BATCH, DEC_SEQ, D_MODEL)
    v3 = v.reshape(DEC_BATCH, DEC_SEQ, D_MODEL)
    tok = lambda s, b, g: (b * n_groups + g, s)
    per_batch = lambda s, b, g: (b, 0, s)
    return pl.pallas_call(
        _na_attn_kernel,
        grid=(D_MODEL // w, DEC_BATCH, n_groups),
        in_specs=[
            pl.BlockSpec((tq, w), tok),
            pl.BlockSpec((None, DEC_SEQ, w), per_batch),
            pl.BlockSpec((None, DEC_SEQ, w), per_batch),
            pl.BlockSpec((None, PAST_LEN, w), per_batch),
            pl.BlockSpec((None, PAST_LEN, w), per_batch),
            pl.BlockSpec((2 * NA_SLABS_PER_STEP, 2 * NA_ROWS - 1, GRID_W, LANES), lambda s, b, g: (s, 0, 0, 0)),
        ],
        out_specs=pl.BlockSpec((tq, w), tok),
        out_shape=jax.ShapeDtypeStruct((S_TOK, D_MODEL), BF16),
        compiler_params=_cparams(("parallel", "parallel", "parallel")),
        name="na_attn",
    )(q, k3, v3, kc, vc, bias)


def _layer_norm(z, g, b):
    mu = jnp.mean(z, axis=-1, keepdims=True)
    zc = z - mu
    var = jnp.mean(zc * zc, axis=-1, keepdims=True)
    return zc * lax.rsqrt(var + LN_EPS) * g + b


def _post_kernel(*refs, n_parts):
    o_refs = refs[:n_parts]
    (y_ref, mod_ref, w_ref, lng_ref, lnb_ref, wr_hi_ref, wr_lo_ref, br_ref,
     y_out_ref, h_out_ref, xg_ref, slot_ref, gate_ref, spill_ref) = refs[n_parts:]
    pw = D_MODEL // n_parts
    mix = None
    for p, o_ref in enumerate(o_refs):
        part = jnp.dot(o_ref[...], w_ref[p * pw:(p + 1) * pw, :], preferred_element_type=F32)
        mix = part if mix is None else mix + part
    y = _layer_norm(ALPHA * y_ref[...] + mod_ref[2] * mix, lng_ref[...], lnb_ref[...])
    y_out_ref[...] = y
    h = y * (1.0 + mod_ref[4]) + mod_ref[3]
    h_hi = h.astype(BF16)
    h_out_ref[...] = h_hi
    h_lo = (h - h_hi.astype(F32)).astype(BF16)
    logits = lax.dot_general(wr_hi_ref[...], h_hi, _NT, preferred_element_type=F32)
    logits += lax.dot_general(wr_hi_ref[...], h_lo, _NT, preferred_element_type=F32)
    logits += lax.dot_general(wr_lo_ref[...], h_hi, _NT, preferred_element_type=F32)
    scores = 1.0 / (1.0 + jnp.exp(-logits))
    work = scores + br_ref[...]
    expert = lax.broadcasted_iota(jnp.int32, work.shape, 0).astype(F32)
    chosen = jnp.zeros(work.shape, F32)
    sel = jnp.zeros(work.shape, F32)
    for _ in range(TOP_K):
        mx = work.max(axis=0, keepdims=True)
        first = jnp.where(work == mx, expert, float(N_EXPERTS)).min(axis=0, keepdims=True)
        pick = expert == first
        chosen = jnp.where(pick, scores, chosen)
        sel = jnp.where(pick, 1.0, sel)
        work = jnp.where(pick, NEG_BIG, work)
    gates_t = chosen / chosen.sum(axis=0, keepdims=True) * ROUTED_SCALE

    tm = h.shape[0]
    earlier = lax.broadcasted_iota(jnp.int32, (tm, tm), 0) < lax.broadcasted_iota(jnp.int32, (tm, tm), 1)
    rank = jnp.dot(sel.astype(BF16), jnp.where(earlier, 1.0, 0.0).astype(BF16), preferred_element_type=F32)
    kept = jnp.where(rank < float(MOE_CAP), sel, 0.0)
    slot = jnp.where(kept > 0.0, rank, -1.0)
    slot_ref[...] = slot
    gate_ref[...] = gates_t * kept
    spilled = gates_t * (sel - kept)
    padded = jnp.concatenate([spilled, jnp.zeros((LANES - N_EXPERTS, tm), F32)], axis=0)
    spill_ref[...] = padded.T[:, :N_EXPERTS]
    c = lax.broadcasted_iota(jnp.int32, (MOE_CAP, tm), 0).astype(F32)
    p_rows = jnp.concatenate(
        [jnp.where(c == slot[e:e + 1, :], 1.0, 0.0).astype(BF16) for e in range(N_EXPERTS)], axis=0)
    half = D_MODEL // 2
    for cols in (slice(0, half), slice(half, D_MODEL)):
        xg = jnp.dot(p_rows, h_hi[:, cols], preferred_element_type=F32).astype(BF16)
        xg_ref[:, :, cols] = xg.reshape(N_EXPERTS, MOE_CAP, half)


def _post_mixer(o_parts, y2d, mod, w_out_bf16, ln_g, ln_b, wr_hi, wr_lo, b_router, *, latent):
    n = y2d.shape[0]
    n_parts = len(o_parts)
    pw = D_MODEL // n_parts
    TM = TM_ROUTE
    per_batch = DEC_SEQ // TM
    mod_map = (lambda i: (0, 1 + i // per_batch, 0, 0)) if latent else (lambda i: (0, 0, 0, 0))
    o_specs = [pl.BlockSpec((TM, pw), functools.partial(lambda i, c: (i, c), c=col)) for _, col in o_parts]
    full = lambda i: (0, 0)
    return pl.pallas_call(
        functools.partial(_post_kernel, n_parts=n_parts),
        grid=(n // TM,),
        in_specs=o_specs + [
            pl.BlockSpec((TM, D_MODEL), lambda i: (i, 0)),
            pl.BlockSpec((6, None, 1, D_MODEL), mod_map),
            pl.BlockSpec((D_MODEL, D_MODEL), full),
            pl.BlockSpec((1, D_MODEL), full),
            pl.BlockSpec((1, D_MODEL), full),
            pl.BlockSpec((N_EXPERTS, D_MODEL), full),
            pl.BlockSpec((N_EXPERTS, D_MODEL), full),
            pl.BlockSpec((N_EXPERTS, 1), full),
        ],
        out_specs=[
            pl.BlockSpec((TM, D_MODEL), lambda i: (i, 0)),
            pl.BlockSpec((TM, D_MODEL), lambda i: (i, 0)),
            pl.BlockSpec((None, N_EXPERTS, MOE_CAP, D_MODEL), lambda i: (i, 0, 0, 0)),
            pl.BlockSpec((N_EXPERTS, TM), lambda i: (0, i)),
            pl.BlockSpec((N_EXPERTS, TM), lambda i: (0, i)),
            pl.BlockSpec((TM, N_EXPERTS), lambda i: (i, 0)),
        ],
        out_shape=[
            jax.ShapeDtypeStruct((n, D_MODEL), F32),
            jax.ShapeDtypeStruct((n, D_MODEL), BF16),
            jax.ShapeDtypeStruct((n // TM, N_EXPERTS, MOE_CAP, D_MODEL), BF16),
            jax.ShapeDtypeStruct((N_EXPERTS, n), F32),
            jax.ShapeDtypeStruct((N_EXPERTS, n), F32),
            jax.ShapeDtypeStruct((n, N_EXPERTS), F32),
        ],
        compiler_params=_cparams(("parallel",)),
        name="post_mixer",
    )(*[a for a, _ in o_parts], y2d, mod, w_out_bf16, ln_g, ln_b, wr_hi, wr_lo, b_router)


def _silu(x):
    return x / (1.0 + jnp.exp(-x))


def _expert_kernel(x_ref, wg_ref, wu_ref, wd_ref, o_ref):
    tiles, cap, d = x_ref.shape
    w1 = jnp.concatenate([wg_ref[...].astype(BF16), wu_ref[...].astype(BF16)], axis=1)
    w2 = wd_ref[...].astype(BF16)
    chunk = 8
    n_chunks = tiles // chunk

    def up(c):
        x = x_ref[c * chunk:(c + 1) * chunk].reshape(chunk * cap, d)
        return jnp.dot(x, w1, preferred_element_type=F32)

    def act(hcat):
        return (_silu(hcat[:, :D_EXPERT]) * hcat[:, D_EXPERT:]).astype(BF16)

    def down(c, a):
        y = jnp.dot(a, w2, preferred_element_type=F32)
        o_ref[c * chunk:(c + 1) * chunk] = y.astype(BF16).reshape(chunk, cap, d)

    h_live, a_live = {}, {}
    for t in range(n_chunks + 2):
        if t < n_chunks:
            h_live[t] = up(t)
        if 1 <= t <= n_chunks:
            a_live[t - 1] = act(h_live.pop(t - 1))
        if t >= 2:
            down(t - 2, a_live.pop(t - 2))


def _routed_experts(xg, w_gate, w_up, w_down, *, layer):
    tiles = xg.shape[0]
    slots = pl.BlockSpec((tiles, None, MOE_CAP, D_MODEL), lambda e: (0, e, 0, 0))
    return pl.pallas_call(
        _expert_kernel,
        grid=(N_EXPERTS,),
        in_specs=[
            slots,
            pl.BlockSpec((None, None, D_MODEL, D_EXPERT), lambda e: (layer, e, 0, 0)),
            pl.BlockSpec((None, None, D_MODEL, D_EXPERT), lambda e: (layer, e, 0, 0)),
            pl.BlockSpec((None, None, D_EXPERT, D_MODEL), lambda e: (layer, e, 0, 0)),
        ],
        out_specs=slots,
        out_shape=jax.ShapeDtypeStruct(xg.shape, BF16),
        compiler_params=_cparams(("parallel",)),
        name="routed_experts",
    )(xg, w_gate, w_up, w_down)


def _combine_kernel(cnt_ref, lst_ref, yg_ref, slot_ref, gate_ref, spill_ref, x_ref, y_ref, mod_ref,
                    sg_ref, su_ref, sd_ref, lng_ref, lnb_ref, wg_hbm, wu_hbm, wd_hbm, o_ref,
                    f_scr, wg_buf, wu_buf, wd_buf, sems, *, layer):
    i = pl.program_id(0)
    tm = x_ref.shape[0]
    slot = slot_ref[...]
    gate = gate_ref[...]
    c = lax.broadcasted_iota(jnp.int32, (MOE_CAP, tm), 0).astype(F32)
    g_rows = jnp.concatenate(
        [jnp.where(c == slot[e:e + 1, :], gate[e:e + 1, :], 0.0) for e in range(N_EXPERTS)], axis=0)
    g = g_rows.T.astype(BF16)
    f = jnp.dot(g, yg_ref[...].reshape(N_EXPERTS * MOE_CAP, D_MODEL), preferred_element_type=F32)
    x = x_ref[...]
    a = _silu(jnp.dot(x, sg_ref[...].astype(BF16), preferred_element_type=F32))
    a = a * jnp.dot(x, su_ref[...].astype(BF16), preferred_element_type=F32)
    f_scr[...] = f + jnp.dot(a.astype(BF16), sd_ref[...].astype(BF16), preferred_element_type=F32)

    def spilled_expert(j, carry):
        e = lst_ref[i, j]
        copies = [pltpu.make_async_copy(src.at[layer, e], dst, sems.at[k])
                  for k, (src, dst) in enumerate(((wg_hbm, wg_buf), (wu_hbm, wu_buf), (wd_hbm, wd_buf)))]
        for cp in copies:
            cp.start()
        for cp in copies:
            cp.wait()
        spill = spill_ref[...]
        lane = lax.broadcasted_iota(jnp.int32, spill.shape, 1)
        ge = jnp.sum(jnp.where(lane == e, spill, 0.0), axis=1, keepdims=True)
        w1 = jnp.concatenate([wg_buf[...].astype(BF16), wu_buf[...].astype(BF16)], axis=1)
        hcat = jnp.dot(x, w1, preferred_element_type=F32)
        act = (_silu(hcat[:, :D_EXPERT]) * hcat[:, D_EXPERT:] * ge).astype(BF16)
        f_scr[...] += jnp.dot(act, wd_buf[...].astype(BF16), preferred_element_type=F32)
        return carry

    lax.fori_loop(0, cnt_ref[i], spilled_expert, 0)
    o_ref[...] = _layer_norm(ALPHA * y_ref[...] + mod_ref[5] * f_scr[...], lng_ref[...], lnb_ref[...])


def _combine(yg, slot, gate, spill, x_bf16, y2d, mod, w_gate, w_up, w_down, ws_gate, ws_up, ws_down, ln_g, ln_b,
             *, layer, latent):
    n = x_bf16.shape[0]
    TM = TM_ROUTE
    per_batch = DEC_SEQ // TM
    spilled = jnp.any(spill.reshape(n // TM, TM, N_EXPERTS) != 0.0, axis=1)
    count = jnp.sum(spilled, axis=1, dtype=jnp.int32)
    order = jnp.argsort(jnp.logical_not(spilled), axis=1, stable=True).astype(jnp.int32)
    mod_map = (lambda i, *_: (0, 1 + i // per_batch, 0, 0)) if latent else (lambda i, *_: (0, 0, 0, 0))
    tok = lambda i, *_: (i, 0)
    full = lambda i, *_: (0, 0)
    shared = lambda i, *_: (layer, 0, 0)
    in_specs = [
        pl.BlockSpec((None, N_EXPERTS, MOE_CAP, D_MODEL), lambda i, *_: (i, 0, 0, 0)),
        pl.BlockSpec((N_EXPERTS, TM), lambda i, *_: (0, i)),
        pl.BlockSpec((N_EXPERTS, TM), lambda i, *_: (0, i)),
        pl.BlockSpec((TM, N_EXPERTS), tok),
        pl.BlockSpec((TM, D_MODEL), tok),
        pl.BlockSpec((TM, D_MODEL), tok),
        pl.BlockSpec((6, None, 1, D_MODEL), mod_map),
        pl.BlockSpec((None, D_MODEL, D_EXPERT), shared),
        pl.BlockSpec((None, D_MODEL, D_EXPERT), shared),
        pl.BlockSpec((None, D_EXPERT, D_MODEL), shared),
        pl.BlockSpec((1, D_MODEL), full),
        pl.BlockSpec((1, D_MODEL), full),
        pl.BlockSpec(memory_space=pl.ANY),
        pl.BlockSpec(memory_space=pl.ANY),
        pl.BlockSpec(memory_space=pl.ANY),
    ]
    return pl.pallas_call(
        functools.partial(_combine_kernel, layer=layer),
        grid_spec=pltpu.PrefetchScalarGridSpec(
            num_scalar_prefetch=2,
            grid=(n // TM,),
            in_specs=in_specs,
            out_specs=pl.BlockSpec((TM, D_MODEL), tok),
            scratch_shapes=[
                pltpu.VMEM((TM, D_MODEL), F32),
                pltpu.VMEM((D_MODEL, D_EXPERT), F32),
                pltpu.VMEM((D_MODEL, D_EXPERT), F32),
                pltpu.VMEM((D_EXPERT, D_MODEL), F32),
                pltpu.SemaphoreType.DMA((3,)),
            ],
        ),
        out_shape=jax.ShapeDtypeStruct((n, D_MODEL), F32),
        compiler_params=_cparams(("arbitrary",)),
        name="moe_combine",
    )(count, order, yg, slot, gate, spill, x_bf16, y2d, mod, ws_gate, ws_up, ws_down, ln_g, ln_b,
      w_gate, w_up, w_down)


def _moe(xg, slot, gate, spill, x_bf16, y2d, mod, w_gate, w_up, w_down, ws_gate, ws_up, ws_down, ln_g, ln_b,
         *, layer, latent):
    yg = _routed_experts(xg, w_gate, w_up, w_down, layer=layer)
    return _combine(yg, slot, gate, spill, x_bf16, y2d, mod, w_gate, w_up, w_down, ws_gate, ws_up, ws_down,
                    ln_g, ln_b, layer=layer, latent=latent)


def _slab_perm():
    idx = []
    for j in range(4):
        for half in range(2):
            head = j + 4 * half
            idx.extend(range(head * HEAD_DIM, (head + 1) * HEAD_DIM))
    return np.asarray(idx, np.int32)


def _rope_tables():
    t = np.arange(DEC_SEQ)
    quarter = HEAD_DIM // 4
    inv = ROPE_THETA ** (-np.arange(quarter, dtype=np.float64) / quarter)
    ar = (t // GRID_W)[:, None] * inv
    ac = (t % GRID_W)[:, None] * inv
    ang = np.concatenate([ar, ar, ac, ac] * 2, axis=-1)
    sign = np.where((np.arange(LANES) % 32) < 16, -1.0, 1.0)
    return jnp.asarray(np.cos(ang), F32), jnp.asarray(np.sin(ang) * sign, F32)


def kernel(x_prompt, x_sample, cache_ka, cache_va, cache_kb, cache_vb, cache_kc, cache_vc, c, c_ctx, w_ada, b_ada, ln_g, ln_b, w_in_even, w_out_even, qnorm_a, knorm_a, sink_b, w_in_odd, w_out_odd, rpb_c, w_router, b_router, w_gate, w_up, w_down, ws_gate, ws_up, ws_down):
    D = D_MODEL
    y_p = x_prompt.reshape(P_TOK, D)
    y_s = x_sample.reshape(S_TOK, D)

    cond8 = jnp.concatenate([c_ctx[None, :], c, jnp.zeros((8 - 1 - DEC_BATCH, D), F32)], axis=0)
    ada = _adaln(cond8, w_ada, b_ada)
    mods = ada.reshape(DEPTH, 8, 6, 1, D).transpose(0, 2, 1, 3, 4)

    cos, sin_signed = _rope_tables()
    perm = _slab_perm()
    in_perm = np.concatenate([perm, 512 + perm, np.arange(1024, EVEN_IN)])
    out_perm = np.concatenate([perm, 512 + perm])
    ones_row = jnp.ones((1, LANES), F32)
    no_sink = jnp.zeros((N_HEADS_B,), F32)

    new = {}
    for l in range(DEPTH):
        mod = mods[l]
        if l % 2 == 0:
            e = l // 2
            w_in = w_in_even[e][:, in_perm].astype(BF16)
            w_out = w_out_even[e][out_perm, :].astype(BF16)
            qn2 = jnp.tile(qnorm_a[e], 2)[None, :]
            kn2 = jnp.tile(knorm_a[e], 2)[None, :]
            proj = functools.partial(_project, w_bf16=w_in, qn2=qn2, kn2=kn2, cos=cos, sin_signed=sin_signed,
                                     dq=1024, dk=256, dv=256, q_norm=512, k_norm=128)
            q_p, k_p, v_p = proj(y_p, mod, rope=False, latent=False)
            q_s, k_s, v_s = proj(y_s, mod, rope=True, latent=True)
            new["ka"], new["kb"] = k_p[:, :LANES], k_p[:, LANES:]
            new["va"], new["vb"] = v_p[:, :LANES], v_p[:, LANES:]
            sink = sink_b[e].astype(F32)
            groups = (((0, 1, 2, 3), 0, None), ((4, 5, 6, 7), 1, tuple(range(N_HEADS_B))))
            o_p = _ctx_attention(q_p, k_p, v_p, sink, groups=groups)

            k_s3 = k_s.reshape(DEC_BATCH, DEC_SEQ, 2 * LANES)
            v_s3 = v_s.reshape(DEC_BATCH, DEC_SEQ, 2 * LANES)
            cka = cache_ka[:, e].reshape(DEC_BATCH, PAST_LEN, LANES).astype(BF16)
            cva = cache_va[:, e].reshape(DEC_BATCH, PAST_LEN, LANES).astype(BF16)
            ckb = cache_kb[:, e].reshape(DEC_BATCH, PAST_LEN, LANES).astype(BF16)
            cvb = cache_vb[:, e].reshape(DEC_BATCH, PAST_LEN, LANES).astype(BF16)
            pad = jnp.zeros((DEC_BATCH, WINDOW, LANES), BF16)
            o_a = _global_attention(q_s, jnp.concatenate([cka, k_s3[:, :, :LANES]], axis=1),
                                    jnp.concatenate([cva, v_s3[:, :, :LANES]], axis=1))
            o_b = _window_attention(q_s, jnp.concatenate([ckb, k_s3[:, :, LANES:], pad], axis=1),
                                    jnp.concatenate([cvb, v_s3[:, :, LANES:], pad], axis=1), sink)
            parts_p = [(o_p, 0), (o_p, 1)]
            parts_s = [(o_a, 0), (o_b, 0)]
        else:
            o = l // 2
            w_in = w_in_odd[o].astype(BF16)
            w_out = w_out_odd[o].astype(BF16)
            proj = functools.partial(_project, w_bf16=w_in, qn2=ones_row, kn2=ones_row, cos=cos,
                                     sin_signed=sin_signed, dq=1024, dk=1024, dv=1024, q_norm=0, k_norm=0,
                                     rope=False)
            q_p, k_p, v_p = proj(y_p, mod, latent=False)
            q_s, k_s, v_s = proj(y_s, mod, latent=True)
            new["kc"], new["vc"] = k_p, v_p
            groups = tuple(((j,), j, None) for j in range(D // LANES))
            o_p = _ctx_attention(q_p, k_p, v_p, no_sink, groups=groups)
            kc = cache_kc[:, o].reshape(DEC_BATCH, PAST_LEN, D).astype(BF16)
            vc = cache_vc[:, o].reshape(DEC_BATCH, PAST_LEN, D).astype(BF16)
            o_s = _na_attention(q_s, k_s, v_s, kc, vc, _na_bias_table(rpb_c[o]))
            parts_p = [(o_p, 0)]
            parts_s = [(o_s, 0)]

        wr = w_router[l].T
        wr_hi = wr.astype(BF16)
        wr_lo = (wr - wr_hi.astype(F32)).astype(BF16)
        post = functools.partial(_post_mixer, mod=mod, w_out_bf16=w_out, ln_g=ln_g[l, 0][None, :],
                                 ln_b=ln_b[l, 0][None, :], wr_hi=wr_hi, wr_lo=wr_lo, b_router=b_router[l][:, None])
        y_p, h_p, *route_p = post(parts_p, y_p, latent=False)
        y_s, h_s, *route_s = post(parts_s, y_s, latent=True)
        moe = functools.partial(_moe, mod=mod, w_gate=w_gate, w_up=w_up, w_down=w_down,
                                ws_gate=ws_gate, ws_up=ws_up, ws_down=ws_down,
                                ln_g=ln_g[l, 1][None, :], ln_b=ln_b[l, 1][None, :], layer=l)
        y_p = moe(*route_p, h_p, y_p, latent=False)
        y_s = moe(*route_s, h_s, y_s, latent=True)

    kv_a = (BATCH, 1, SEQ, N_KV_A, HEAD_DIM)
    kv_c = (BATCH, 1, SEQ, N_HEADS_C, HEAD_DIM)
    return (y_p.reshape(BATCH, SEQ, D), y_s.reshape(DEC_BATCH, DEC_SEQ, D),
            new["ka"].reshape(kv_a), new["va"].reshape(kv_a), new["kb"].reshape(kv_a), new["vb"].reshape(kv_a),
            new["kc"].reshape(kv_c), new["vc"].reshape(kv_c))
```

```python
import functools
import math

import numpy as np
import jax
import jax.numpy as jnp
from jax import lax
from jax.experimental import pallas as pl
from jax.experimental.pallas import tpu as pltpu

F32 = jnp.float32
BF16 = jnp.bfloat16

D_MODEL = 1024
BATCH = 16
SEQ = 256
DEPTH = 2
DEC_BATCH = 2
DEC_SEQ = 4096
PAST_LEN = 512
GRID_W = 64
HEAD_DIM = 64
N_HEADS_A = 8
N_KV_A = 2
N_HEADS_B = 8
N_KV_B = 2
N_HEADS_C = 16
EVEN_IN = 1536
ODD_IN = 3072
WINDOW = 128
NA_ROWS = 8
NA_COLS = 16
ROPE_THETA = 10000.0
N_EXPERTS = 64
TOP_K = 8
D_EXPERT = 128
ROUTED_SCALE = 2.5
ALPHA = (2 * DEPTH) ** 0.25
LN_EPS = 1e-6
RMS_EPS = 1e-6
NEG_BIG = -1e30
LOG2E = math.log2(math.e)

LANES = 128
P_TOK = BATCH * SEQ
S_TOK = DEC_BATCH * DEC_SEQ
TM = 512
TM_MOE = 1024
EXPERTS_PER_STEP = 8
NA_QROWS = 4
NA_KROWS = 12
NA_SLABS_PER_STEP = 8
CTX_SEQS_PER_STEP = 2
WINDOW_BLOCKS_PER_STEP = 4
VMEM_LIMIT = 56 * 1024 * 1024

_NT = (((1,), (1,)), ((), ()))


def _cparams(sem):
    return pltpu.CompilerParams(dimension_semantics=sem, vmem_limit_bytes=VMEM_LIMIT)


def _half_masks(dtype):
    lane = lax.broadcasted_iota(jnp.int32, (1, LANES), 1)
    lo = jnp.where(lane < HEAD_DIM, 1.0, 0.0).astype(dtype)
    hi = jnp.where(lane < HEAD_DIM, 0.0, 1.0).astype(dtype)
    return lo, hi


def _ada_kernel(c_ref, w_ref, b_ref, o_ref):
    c = c_ref[...]
    a = c / (1.0 + jnp.exp(-c))
    w = w_ref[0]
    a_hi = a.astype(BF16)
    a_lo = (a - a_hi.astype(F32)).astype(BF16)
    w_hi = w.astype(BF16)
    w_lo = (w - w_hi.astype(F32)).astype(BF16)
    acc = jnp.dot(a_hi, w_hi, preferred_element_type=F32)
    acc += jnp.dot(a_lo, w_hi, preferred_element_type=F32)
    acc += jnp.dot(a_hi, w_lo, preferred_element_type=F32)
    o_ref[0] = acc + b_ref[0]


def _adaln(cond8, w_ada, b_ada):
    tn = 1536
    return pl.pallas_call(
        _ada_kernel,
        grid=(DEPTH, 6 * D_MODEL // tn),
        in_specs=[
            pl.BlockSpec((8, D_MODEL), lambda l, j: (0, 0)),
            pl.BlockSpec((1, D_MODEL, tn), lambda l, j: (l, 0, j)),
            pl.BlockSpec((1, 1, tn), lambda l, j: (l, 0, j)),
        ],
        out_specs=pl.BlockSpec((1, 8, tn), lambda l, j: (l, 0, j)),
        out_shape=jax.ShapeDtypeStruct((DEPTH, 8, 6 * D_MODEL), F32),
        compiler_params=_cparams(("parallel", "parallel")),
        name="adaln",
    )(cond8, w_ada, b_ada.reshape(DEPTH, 1, 6 * D_MODEL))


def _group_sum_matrix():
    r = lax.broadcasted_iota(jnp.int32, (LANES, LANES), 0) // HEAD_DIM
    c = lax.broadcasted_iota(jnp.int32, (LANES, LANES), 1) // HEAD_DIM
    return jnp.where(r == c, 1.0, 0.0).astype(BF16)


def _rms_slab(t, g, gmat):
    sq = t * t
    hi = sq.astype(BF16)
    lo = (sq - hi.astype(F32)).astype(BF16)
    ss = jnp.dot(hi, gmat, preferred_element_type=F32) + jnp.dot(lo, gmat, preferred_element_type=F32)
    return t * lax.rsqrt(ss * (1.0 / HEAD_DIM) + RMS_EPS) * g


def _rope_slab(t, cos, sin_signed, first):
    r = jnp.where(first, pltpu.roll(t, LANES - 16, 1), pltpu.roll(t, 16, 1))
    return t * cos + r * sin_signed


def _proj_kernel(x_ref, mod_ref, w_ref, qn_ref, kn_ref, cos_ref, sin_ref, q_ref, k_ref, v_ref,
                 *, dq, dk, q_norm, k_norm, rope):
    x = x_ref[...]
    h = (x * (1.0 + mod_ref[1]) + mod_ref[0]).astype(BF16)
    y = jnp.dot(h, w_ref[...], preferred_element_type=F32)
    tm = x.shape[0]
    gmat = _group_sum_matrix() if (q_norm or k_norm) else None
    if rope:
        cos = cos_ref[...]
        sin_signed = sin_ref[...]
        first = (lax.broadcasted_iota(jnp.int32, (tm, LANES), 1) % 32) < 16
    for s in range(dq // LANES):
        t = y[:, s * LANES:(s + 1) * LANES]
        if s * LANES < q_norm:
            t = _rms_slab(t, qn_ref[...], gmat)
        if rope:
            t = _rope_slab(t, cos, sin_signed, first)
        q_ref[:, s * LANES:(s + 1) * LANES] = (t * (HEAD_DIM ** -0.5 * LOG2E)).astype(q_ref.dtype)
    for s in range(dk // LANES):
        t = y[:, dq + s * LANES:dq + (s + 1) * LANES]
        if s * LANES < k_norm:
            t = _rms_slab(t, kn_ref[...], gmat)
        if rope:
            t = _rope_slab(t, cos, sin_signed, first)
        k_ref[:, s * LANES:(s + 1) * LANES] = t.astype(k_ref.dtype)
    v_ref[...] = y[:, dq + dk:].astype(v_ref.dtype)


def _project(x2d, mod, w_bf16, qn2, kn2, cos, sin_signed, *, dq, dk, dv, q_norm, k_norm, rope, latent):
    n = x2d.shape[0]
    per_batch = DEC_SEQ // TM
    if latent:
        mod_map = lambda i: (0, 1 + i // per_batch, 0, 0)
        pos_map = lambda i: (i % per_batch, 0)
        kv_dtype = BF16
    else:
        mod_map = lambda i: (0, 0, 0, 0)
        pos_map = lambda i: (0, 0)
        kv_dtype = F32
    kern = functools.partial(_proj_kernel, dq=dq, dk=dk, q_norm=q_norm, k_norm=k_norm, rope=rope)
    return pl.pallas_call(
        kern,
        grid=(n // TM,),
        in_specs=[
            pl.BlockSpec((TM, D_MODEL), lambda i: (i, 0)),
            pl.BlockSpec((6, None, 1, D_MODEL), mod_map),
            pl.BlockSpec((D_MODEL, dq + dk + dv), lambda i: (0, 0)),
            pl.BlockSpec((1, LANES), lambda i: (0, 0)),
            pl.BlockSpec((1, LANES), lambda i: (0, 0)),
            pl.BlockSpec((TM, LANES), pos_map),
            pl.BlockSpec((TM, LANES), pos_map),
        ],
        out_specs=[
            pl.BlockSpec((TM, dq), lambda i: (i, 0)),
            pl.BlockSpec((TM, dk), lambda i: (i, 0)),
            pl.BlockSpec((TM, dv), lambda i: (i, 0)),
        ],
        out_shape=[
            jax.ShapeDtypeStruct((n, dq), BF16),
            jax.ShapeDtypeStruct((n, dk), kv_dtype),
            jax.ShapeDtypeStruct((n, dv), kv_dtype),
        ],
        compiler_params=_cparams(("parallel",)),
        name="in_proj",
    )(x2d, mod, w_bf16, qn2, kn2, cos, sin_signed)


def _stack_heads(q_ref, slabs, lo, hi):
    qs = [q_ref[:, j * LANES:(j + 1) * LANES] for j in slabs]
    return jnp.concatenate([q * lo for q in qs] + [q * hi for q in qs], axis=0)


def _tile_lanes(x, n):
    return jnp.concatenate([x] * n, axis=1)


def _pv_with_denominator(p, v, lo, hi):
    half = p.shape[0] // 2
    pv_lo = jnp.dot(p[:half], v * lo + hi, preferred_element_type=F32)
    pv_hi = jnp.dot(p[half:], v * hi + lo, preferred_element_type=F32)
    return jnp.concatenate([pv_lo, pv_hi], axis=0)


def _normalize_store(o, o_ref, slabs, tq, lane_lo):
    half = len(slabs) * tq
    for n, j in enumerate(slabs):
        o_lo = o[n * tq:(n + 1) * tq]
        o_hi = o[half + n * tq:half + (n + 1) * tq]
        o_lo = o_lo * (1.0 / pltpu.roll(o_lo, HEAD_DIM, 1))
        o_hi = o_hi * (1.0 / pltpu.roll(o_hi, HEAD_DIM, 1))
        o_ref[:, j * LANES:(j + 1) * LANES] = jnp.where(lane_lo, o_lo, o_hi).astype(o_ref.dtype)


def _sink_rows(sink_ref, heads, tq):
    return jnp.concatenate([jnp.full((tq, LANES), sink_ref[h] * LOG2E, F32) for h in heads], axis=0)


def _softmax_pv(parts, sink, lo, hi):
    rows = parts[0][0].shape[0]
    m = jnp.full((rows, LANES), NEG_BIG, F32) if sink is None else sink
    for s, _ in parts:
        m = jnp.maximum(m, s.max(axis=1, keepdims=True))
    o = None
    for s, v in parts:
        p = jnp.exp2(s - _tile_lanes(m, s.shape[1] // LANES)).astype(BF16)
        pv = _pv_with_denominator(p, v, lo, hi)
        o = pv if o is None else o + pv
    if sink is not None:
        lo32, hi32 = _half_masks(F32)
        e = jnp.exp2(sink - m)
        half = rows // 2
        o = o + jnp.concatenate([e[:half] * hi32, e[half:] * lo32], axis=0)
    return o


def _ctx_attn_kernel(sink_ref, q_ref, k_ref, v_ref, o_ref, *, groups):
    lo, hi = _half_masks(BF16)
    lo32, hi32 = _half_masks(F32)
    lane_lo = lax.broadcasted_iota(jnp.int32, (SEQ, LANES), 1) < HEAD_DIM
    units = [(b, grp) for b in range(q_ref.shape[0] // SEQ) for grp in groups]

    def scores(unit):
        b, (slabs, ks, _) = unit
        kk = k_ref[b * SEQ:(b + 1) * SEQ, ks * LANES:(ks + 1) * LANES].astype(BF16)
        return lax.dot_general(_stack_heads(q_ref.at[b * SEQ:(b + 1) * SEQ], slabs, lo, hi), kk, _NT,
                               preferred_element_type=F32)

    def probs(unit, s):
        sink_heads = unit[1][2]
        if sink_heads is None:
            m = jnp.maximum(jnp.full((s.shape[0], LANES), NEG_BIG, F32), s.max(axis=1, keepdims=True))
            extra = None
        else:
            sink = _sink_rows(sink_ref, sink_heads, SEQ)
            m = jnp.maximum(sink, s.max(axis=1, keepdims=True))
            e = jnp.exp2(sink - m)
            half = s.shape[0] // 2
            extra = jnp.concatenate([e[:half] * hi32, e[half:] * lo32], axis=0)
        return jnp.exp2(s - _tile_lanes(m, s.shape[1] // LANES)).astype(BF16), extra

    def output(unit, p, extra):
        b, (slabs, ks, _) = unit
        vv = v_ref[b * SEQ:(b + 1) * SEQ, ks * LANES:(ks + 1) * LANES].astype(BF16)
        o = _pv_with_denominator(p, vv, lo, hi)
        if extra is not None:
            o = o + extra
        _normalize_store(o, o_ref.at[b * SEQ:(b + 1) * SEQ], slabs, SEQ, lane_lo)

    s_live, p_live = {}, {}
    for t in range(len(units) + 2):
        if t < len(units):
            s_live[t] = scores(units[t])
        if 1 <= t <= len(units):
            p_live[t - 1] = probs(units[t - 1], s_live.pop(t - 1))
        if t >= 2:
            output(units[t - 2], *p_live.pop(t - 2))


def _ctx_attention(q, k, v, sink, *, groups):
    kw = k.shape[1]
    rows = CTX_SEQS_PER_STEP * SEQ
    return pl.pallas_call(
        functools.partial(_ctx_attn_kernel, groups=groups),
        grid=(BATCH // CTX_SEQS_PER_STEP,),
        in_specs=[
            pl.BlockSpec(memory_space=pltpu.SMEM),
            pl.BlockSpec((rows, D_MODEL), lambda b: (b, 0)),
            pl.BlockSpec((rows, kw), lambda b: (b, 0)),
            pl.BlockSpec((rows, kw), lambda b: (b, 0)),
        ],
        out_specs=pl.BlockSpec((rows, D_MODEL), lambda b: (b, 0)),
        out_shape=jax.ShapeDtypeStruct((P_TOK, D_MODEL), BF16),
        compiler_params=_cparams(("parallel",)),
        name="ctx_attn",
    )(sink, q, k, v)


def _global_attn_kernel(q_ref, k_ref, v_ref, o_ref, s0, s1, p0, p1, a0, a1, m_ref, acc_ref, *, tk):
    tq = q_ref.shape[0]
    slabs = tuple(range(q_ref.shape[1] // LANES))
    nk = k_ref.shape[0] // tk
    lo, hi = _half_masks(BF16)
    lane_lo = lax.broadcasted_iota(jnp.int32, (tq, LANES), 1) < HEAD_DIM
    qstack = _stack_heads(q_ref, slabs, lo, hi)
    m_ref[...] = jnp.full(m_ref.shape, NEG_BIG, F32)
    acc_ref[...] = jnp.zeros(acc_ref.shape, F32)
    s_bufs, p_bufs, a_bufs = (s0, s1), (p0, p1), (a0, a1)

    def stage_a(c):
        s_bufs[c % 2][...] = lax.dot_general(qstack, k_ref[c * tk:(c + 1) * tk, :], _NT, preferred_element_type=F32)

    def stage_b(c):
        s = s_bufs[c % 2][...]
        m_old = m_ref[...]
        m_new = jnp.maximum(m_old, s.max(axis=1, keepdims=True))
        a_bufs[c % 2][...] = jnp.exp2(m_old - m_new)
        p_bufs[c % 2][...] = jnp.exp2(s - _tile_lanes(m_new, tk // LANES)).astype(BF16)
        m_ref[...] = m_new

    def stage_c(c):
        pv = _pv_with_denominator(p_bufs[c % 2][...], v_ref[c * tk:(c + 1) * tk, :], lo, hi)
        acc_ref[...] = a_bufs[c % 2][...] * acc_ref[...] + pv

    for t in range(nk + 2):
        if t < nk:
            stage_a(t)
        if 1 <= t <= nk:
            stage_b(t - 1)
        if t >= 2:
            stage_c(t - 2)
    _normalize_store(acc_ref[...], o_ref, slabs, tq, lane_lo)


def _global_attention(q, kcat, vcat):
    tq, tk = 256, 768
    t = kcat.shape[1]
    per_batch = DEC_SEQ // tq
    rows = N_HEADS_A * tq
    kv_spec = pl.BlockSpec((None, t, LANES), lambda b, i: (b, 0, 0))
    return pl.pallas_call(
        functools.partial(_global_attn_kernel, tk=tk),
        grid=(DEC_BATCH, per_batch),
        in_specs=[pl.BlockSpec((tq, 4 * LANES), lambda b, i: (b * per_batch + i, 0)), kv_spec, kv_spec],
        out_specs=pl.BlockSpec((tq, 4 * LANES), lambda b, i: (b * per_batch + i, 0)),
        out_shape=jax.ShapeDtypeStruct((S_TOK, 4 * LANES), BF16),
        scratch_shapes=[pltpu.VMEM((rows, tk), F32), pltpu.VMEM((rows, tk), F32),
                        pltpu.VMEM((rows, tk), BF16), pltpu.VMEM((rows, tk), BF16),
                        pltpu.VMEM((rows, LANES), F32), pltpu.VMEM((rows, LANES), F32),
                        pltpu.VMEM((rows, LANES), F32), pltpu.VMEM((rows, LANES), F32)],
        compiler_params=_cparams(("parallel", "parallel")),
        name="global_attn",
    )(q, kcat, vcat)


def _window_attn_kernel(sink_ref, q_ref, k_ref, v_ref, o_ref):
    tq = WINDOW
    n_blocks = q_ref.shape[0] // tq
    slabs = tuple(range(q_ref.shape[1] // LANES))
    n_heads = 2 * len(slabs)
    span = tq + 2 * WINDOW
    lo, hi = _half_masks(BF16)
    lo32, hi32 = _half_masks(F32)
    lane_lo = lax.broadcasted_iota(jnp.int32, (tq, LANES), 1) < HEAD_DIM
    row = lax.broadcasted_iota(jnp.int32, (tq, span), 0)
    col = lax.broadcasted_iota(jnp.int32, (tq, span), 1)
    band = (col >= row) & (col <= row + 2 * WINDOW)
    sink = _sink_rows(sink_ref, range(n_heads), tq)

    def local_offset(u):
        i = pl.program_id(1) * n_blocks + u
        return i, pl.multiple_of(PAST_LEN - WINDOW + i * tq, LANES)

    def scores(u):
        i, off = local_offset(u)
        kpos = i * tq - WINDOW + col
        valid = band & (kpos >= 0) & (kpos < DEC_SEQ)
        qstack = _stack_heads(q_ref.at[u * tq:(u + 1) * tq], slabs, lo, hi)
        s_loc = lax.dot_general(qstack, k_ref[pl.ds(off, span), :], _NT, preferred_element_type=F32)
        s_loc = jnp.where(valid[None], s_loc.reshape(n_heads, tq, span), NEG_BIG).reshape(n_heads * tq, span)
        return [s_loc, lax.dot_general(qstack, k_ref[0:PAST_LEN, :], _NT, preferred_element_type=F32)]

    def probs(s_parts):
        m = sink
        for s in s_parts:
            m = jnp.maximum(m, s.max(axis=1, keepdims=True))
        e = jnp.exp2(sink - m)
        half = e.shape[0] // 2
        extra = jnp.concatenate([e[:half] * hi32, e[half:] * lo32], axis=0)
        return [jnp.exp2(s - _tile_lanes(m, s.shape[1] // LANES)).astype(BF16) for s in s_parts], extra

    def output(u, p_parts, extra):
        _, off = local_offset(u)
        o = (_pv_with_denominator(p_parts[0], v_ref[pl.ds(off, span), :], lo, hi)
             + _pv_with_denominator(p_parts[1], v_ref[0:PAST_LEN, :], lo, hi) + extra)
        _normalize_store(o, o_ref.at[u * tq:(u + 1) * tq], slabs, tq, lane_lo)

    s_live, p_live = {}, {}
    for t in range(n_blocks + 2):
        if t < n_blocks:
            s_live[t] = scores(t)
        if 1 <= t <= n_blocks:
            p_live[t - 1] = probs(s_live.pop(t - 1))
        if t >= 2:
            output(t - 2, *p_live.pop(t - 2))


def _window_attention(q, kcat, vcat, sink):
    tq = WINDOW_BLOCKS_PER_STEP * WINDOW
    t = kcat.shape[1]
    per_batch = DEC_SEQ // tq
    return pl.pallas_call(
        _window_attn_kernel,
        grid=(DEC_BATCH, per_batch),
        in_specs=[
            pl.BlockSpec(memory_space=pltpu.SMEM),
            pl.BlockSpec((tq, 4 * LANES), lambda b, i: (b * per_batch + i, 1)),
            pl.BlockSpec((None, t, LANES), lambda b, i: (b, 0, 0)),
            pl.BlockSpec((None, t, LANES), lambda b, i: (b, 0, 0)),
        ],
        out_specs=pl.BlockSpec((tq, 4 * LANES), lambda b, i: (b * per_batch + i, 0)),
        out_shape=jax.ShapeDtypeStruct((S_TOK, 4 * LANES), BF16),
        compiler_params=_cparams(("parallel", "parallel")),
        name="window_attn",
    )(sink, q, kcat, vcat)


def _na_start_row(g):
    return jnp.clip(NA_QROWS * g - NA_ROWS // 2, 0, DEC_SEQ // GRID_W - NA_KROWS)


def _na_step_bias(t_ref, half, g, lane_lo64):
    rows = DEC_SEQ // GRID_W
    start = _na_start_row(g)
    d0 = start - NA_QROWS * g + NA_ROWS - 1
    row_blocks = []
    for qr in range(NA_QROWS):
        rs = jnp.clip(NA_QROWS * g + qr - NA_ROWS // 2, 0, rows - NA_ROWS)
        blocks = []
        for m in range(NA_KROWS // 2):
            tiles = []
            for kr in (2 * m, 2 * m + 1):
                krow = start + kr
                inside = (krow >= rs) & (krow < rs + NA_ROWS)
                dr = jnp.clip(kr - qr + d0, 0, 2 * NA_ROWS - 2)
                tiles.append(t_ref[half, dr] + jnp.where(inside, 0.0, NEG_BIG))
            blocks.append(jnp.where(lane_lo64, tiles[0], tiles[1]))
        row_blocks.append(jnp.concatenate(blocks, axis=1))
    return jnp.concatenate(row_blocks, axis=0)


def _na_attn_kernel(q_ref, k_ref, v_ref, kc_ref, vc_ref, t_ref, o_ref):
    g = pl.program_id(2)
    tq = q_ref.shape[0]
    lo, hi = _half_masks(BF16)
    lane_lo = lax.broadcasted_iota(jnp.int32, (tq, LANES), 1) < HEAD_DIM
    lane_lo64 = lax.broadcasted_iota(jnp.int32, (GRID_W, LANES), 1) < HEAD_DIM
    off = pl.multiple_of(_na_start_row(g) * GRID_W, GRID_W)
    n_slabs = q_ref.shape[1] // LANES
    cols = [slice(n * LANES, (n + 1) * LANES) for n in range(n_slabs)]

    def scores(n):
        qstack = _stack_heads(q_ref, (n,), lo, hi)
        bias = jnp.concatenate([_na_step_bias(t_ref, 2 * n + half, g, lane_lo64) for half in range(2)], axis=0)
        k_loc = k_ref[pl.ds(off, NA_KROWS * GRID_W), cols[n]]
        s_loc = lax.dot_general(qstack, k_loc, _NT, preferred_element_type=F32) + bias
        return [s_loc, lax.dot_general(qstack, kc_ref[:, cols[n]], _NT, preferred_element_type=F32)]

    def probs(s_parts):
        m = jnp.full((2 * tq, LANES), NEG_BIG, F32)
        for s in s_parts:
            m = jnp.maximum(m, s.max(axis=1, keepdims=True))
        return [jnp.exp2(s - _tile_lanes(m, s.shape[1] // LANES)).astype(BF16) for s in s_parts]

    def output(n, p_parts):
        v_loc = v_ref[pl.ds(off, NA_KROWS * GRID_W), cols[n]]
        o = _pv_with_denominator(p_parts[0], v_loc, lo, hi) + _pv_with_denominator(p_parts[1], vc_ref[:, cols[n]], lo, hi)
        _normalize_store(o, o_ref, (n,), tq, lane_lo)

    s_live, p_live = {}, {}
    for t in range(n_slabs + 2):
        if t < n_slabs:
            s_live[t] = scores(t)
        if 1 <= t <= n_slabs:
            p_live[t - 1] = probs(s_live.pop(t - 1))
        if t >= 2:
            output(t - 2, p_live.pop(t - 2))


def _na_bias_table(rpb):
    cols = np.arange(GRID_W)
    cs = np.clip(cols - NA_COLS // 2, 0, GRID_W - NA_COLS)
    dc = cols[None, :] - cols[:, None] + NA_COLS - 1
    inside = (cols[None, :] >= cs[:, None]) & (cols[None, :] < cs[:, None] + NA_COLS)
    onehot = (np.arange(2 * NA_COLS - 1)[:, None, None] == dc[None]) & inside[None]
    t = jnp.einsum("hdc,cqk->hdqk", rpb.astype(F32), jnp.asarray(onehot, F32), precision=lax.Precision.HIGHEST)
    t = jnp.where(inside[None, None], t * LOG2E, NEG_BIG)
    return jnp.concatenate([t, t], axis=-1)


def _na_attention(q, k, v, kc, vc, bias):
    tq = NA_QROWS * GRID_W
    n_groups = DEC_SEQ // tq
    w = NA_SLABS_PER_STEP * LANES
    k3 = k.reshape(DEC_BATCH, DEC_SEQ, D_MODEL)
    v3 = v.reshape(DEC_BATCH, DEC_SEQ, D_MODEL)
    tok = lambda s, b, g: (b * n_groups + g, s)
    per_batch = lambda s, b, g: (b, 0, s)
    once = pl.Buffered(1)
    return pl.pallas_call(
        _na_attn_kernel,
        grid=(D_MODEL // w, DEC_BATCH, n_groups),
        in_specs=[
            pl.BlockSpec((tq, w), tok),
            pl.BlockSpec((None, DEC_SEQ, w), per_batch, pipeline_mode=once),
            pl.BlockSpec((None, DEC_SEQ, w), per_batch, pipeline_mode=once),
            pl.BlockSpec((None, PAST_LEN, w), per_batch, pipeline_mode=once),
            pl.BlockSpec((None, PAST_LEN, w), per_batch, pipeline_mode=once),
            pl.BlockSpec((2 * NA_SLABS_PER_STEP, 2 * NA_ROWS - 1, GRID_W, LANES), lambda s, b, g: (s, 0, 0, 0),
                         pipeline_mode=once),
        ],
        out_specs=pl.BlockSpec((tq, w), tok),
        out_shape=jax.ShapeDtypeStruct((S_TOK, D_MODEL), BF16),
        compiler_params=_cparams(("parallel", "parallel", "parallel")),
        name="na_attn",
    )(q, k3, v3, kc, vc, bias)


def _layer_norm(z, g, b):
    mu = jnp.mean(z, axis=-1, keepdims=True)
    zc = z - mu
    var = jnp.mean(zc * zc, axis=-1, keepdims=True)
    return zc * lax.rsqrt(var + LN_EPS) * g + b


def _post_kernel(*refs, n_parts):
    o_refs = refs[:n_parts]
    (y_ref, mod_ref, w_ref, lng_ref, lnb_ref, wr_hi_ref, wr_lo_ref, br_ref,
     y_out_ref, h_out_ref, gates_ref) = refs[n_parts:]
    pw = D_MODEL // n_parts
    mix = None
    for p, o_ref in enumerate(o_refs):
        part = jnp.dot(o_ref[...], w_ref[p * pw:(p + 1) * pw, :], preferred_element_type=F32)
        mix = part if mix is None else mix + part
    y = _layer_norm(ALPHA * y_ref[...] + mod_ref[2] * mix, lng_ref[...], lnb_ref[...])
    y_out_ref[...] = y
    h = y * (1.0 + mod_ref[4]) + mod_ref[3]
    h_hi = h.astype(BF16)
    h_out_ref[...] = h_hi
    h_lo = (h - h_hi.astype(F32)).astype(BF16)
    logits = lax.dot_general(wr_hi_ref[...], h_hi, _NT, preferred_element_type=F32)
    logits += lax.dot_general(wr_hi_ref[...], h_lo, _NT, preferred_element_type=F32)
    logits += lax.dot_general(wr_lo_ref[...], h_hi, _NT, preferred_element_type=F32)
    scores = 1.0 / (1.0 + jnp.exp(-logits))
    work = scores + br_ref[...]
    expert = lax.broadcasted_iota(jnp.int32, work.shape, 0).astype(F32)
    chosen = jnp.zeros(work.shape, F32)
    for _ in range(TOP_K):
        mx = work.max(axis=0, keepdims=True)
        first = jnp.where(work == mx, expert, float(N_EXPERTS)).min(axis=0, keepdims=True)
        pick = expert == first
        chosen = jnp.where(pick, scores, chosen)
        work = jnp.where(pick, NEG_BIG, work)
    gates_t = chosen / chosen.sum(axis=0, keepdims=True) * ROUTED_SCALE
    padded = jnp.concatenate([gates_t, jnp.zeros((LANES - N_EXPERTS, gates_t.shape[1]), F32)], axis=0)
    gates_ref[...] = padded.T[:, :N_EXPERTS]


def _post_mixer(o_parts, y2d, mod, w_out_bf16, ln_g, ln_b, wr_hi, wr_lo, b_router, *, latent):
    n = y2d.shape[0]
    n_parts = len(o_parts)
    pw = D_MODEL // n_parts
    per_batch = DEC_SEQ // TM
    mod_map = (lambda i: (0, 1 + i // per_batch, 0, 0)) if latent else (lambda i: (0, 0, 0, 0))
    o_specs = [pl.BlockSpec((TM, pw), functools.partial(lambda i, c: (i, c), c=col)) for _, col in o_parts]
    full = lambda i: (0, 0)
    return pl.pallas_call(
        functools.partial(_post_kernel, n_parts=n_parts),
        grid=(n // TM,),
        in_specs=o_specs + [
            pl.BlockSpec((TM, D_MODEL), lambda i: (i, 0)),
            pl.BlockSpec((6, None, 1, D_MODEL), mod_map),
            pl.BlockSpec((D_MODEL, D_MODEL), full),
            pl.BlockSpec((1, D_MODEL), full),
            pl.BlockSpec((1, D_MODEL), full),
            pl.BlockSpec((N_EXPERTS, D_MODEL), full),
            pl.BlockSpec((N_EXPERTS, D_MODEL), full),
            pl.BlockSpec((N_EXPERTS, 1), full),
        ],
        out_specs=[
            pl.BlockSpec((TM, D_MODEL), lambda i: (i, 0)),
            pl.BlockSpec((TM, D_MODEL), lambda i: (i, 0)),
            pl.BlockSpec((TM, N_EXPERTS), lambda i: (i, 0)),
        ],
        out_shape=[
            jax.ShapeDtypeStruct((n, D_MODEL), F32),
            jax.ShapeDtypeStruct((n, D_MODEL), BF16),
            jax.ShapeDtypeStruct((n, N_EXPERTS), F32),
        ],
        compiler_params=_cparams(("parallel",)),
        name="post_mixer",
    )(*[a for a, _ in o_parts], y2d, mod, w_out_bf16, ln_g, ln_b, wr_hi, wr_lo, b_router)


def _silu(x):
    return x / (1.0 + jnp.exp(-x))


def _moe_kernel(x_ref, gates_ref, y_ref, mod_ref, wg_ref, wu_ref, wd_ref, sg_ref, su_ref, sd_ref,
                lng_ref, lnb_ref, o_ref, acc_ref):
    e = pl.program_id(1)
    x = x_ref[...]

    @pl.when(e == 0)
    def _():
        a = _silu(jnp.dot(x, sg_ref[...].astype(BF16), preferred_element_type=F32))
        a = a * jnp.dot(x, su_ref[...].astype(BF16), preferred_element_type=F32)
        acc_ref[...] = jnp.dot(a.astype(BF16), sd_ref[...].astype(BF16), preferred_element_type=F32)

    gates = gates_ref[...]
    lane = lax.broadcasted_iota(jnp.int32, gates.shape, 1)
    for pair in range(EXPERTS_PER_STEP // 2):
        acts = []
        for j in (2 * pair, 2 * pair + 1):
            w1 = jnp.concatenate([wg_ref[j].astype(BF16), wu_ref[j].astype(BF16)], axis=1)
            hcat = jnp.dot(x, w1, preferred_element_type=F32)
            ge = jnp.sum(jnp.where(lane == e * EXPERTS_PER_STEP + j, gates, 0.0), axis=1, keepdims=True)
            acts.append((_silu(hcat[:, :D_EXPERT]) * hcat[:, D_EXPERT:] * ge).astype(BF16))
        w2 = jnp.concatenate([wd_ref[2 * pair].astype(BF16), wd_ref[2 * pair + 1].astype(BF16)], axis=0)
        acc_ref[...] += jnp.dot(jnp.concatenate(acts, axis=1), w2, preferred_element_type=F32)

    @pl.when(e == pl.num_programs(1) - 1)
    def _():
        o_ref[...] = _layer_norm(ALPHA * y_ref[...] + mod_ref[5] * acc_ref[...], lng_ref[...], lnb_ref[...])


def _moe(x_bf16, gates, y2d, mod, w_gate, w_up, w_down, ws_gate, ws_up, ws_down, ln_g, ln_b, *, layer, latent):
    n = x_bf16.shape[0]
    per_batch = DEC_SEQ // TM_MOE
    mod_map = (lambda i, e: (0, 1 + i // per_batch, 0, 0)) if latent else (lambda i, e: (0, 0, 0, 0))
    ep = EXPERTS_PER_STEP
    tok = lambda i, e: (i, 0)
    full = lambda i, e: (0, 0)
    routed = lambda i, e: (layer, e, 0, 0)
    shared = lambda i, e: (layer, 0, 0)
    return pl.pallas_call(
        _moe_kernel,
        grid=(n // TM_MOE, N_EXPERTS // ep),
        in_specs=[
            pl.BlockSpec((TM_MOE, D_MODEL), tok),
            pl.BlockSpec((TM_MOE, N_EXPERTS), tok),
            pl.BlockSpec((TM_MOE, D_MODEL), tok, pipeline_mode=pl.Buffered(1)),
            pl.BlockSpec((6, None, 1, D_MODEL), mod_map),
            pl.BlockSpec((None, ep, D_MODEL, D_EXPERT), routed),
            pl.BlockSpec((None, ep, D_MODEL, D_EXPERT), routed),
            pl.BlockSpec((None, ep, D_EXPERT, D_MODEL), routed),
            pl.BlockSpec((None, D_MODEL, D_EXPERT), shared),
            pl.BlockSpec((None, D_MODEL, D_EXPERT), shared),
            pl.BlockSpec((None, D_EXPERT, D_MODEL), shared),
            pl.BlockSpec((1, D_MODEL), full),
            pl.BlockSpec((1, D_MODEL), full),
        ],
        out_specs=pl.BlockSpec((TM_MOE, D_MODEL), tok),
        out_shape=jax.ShapeDtypeStruct((n, D_MODEL), F32),
        scratch_shapes=[pltpu.VMEM((TM_MOE, D_MODEL), F32)],
        compiler_params=_cparams(("parallel", "arbitrary")),
        name="moe",
    )(x_bf16, gates, y2d, mod, w_gate, w_up, w_down, ws_gate, ws_up, ws_down, ln_g, ln_b)


def _slab_perm():
    idx = []
    for j in range(4):
        for half in range(2):
            head = j + 4 * half
            idx.extend(range(head * HEAD_DIM, (head + 1) * HEAD_DIM))
    return np.asarray(idx, np.int32)


def _rope_tables():
    t = np.arange(DEC_SEQ)
    quarter = HEAD_DIM // 4
    inv = ROPE_THETA ** (-np.arange(quarter, dtype=np.float64) / quarter)
    ar = (t // GRID_W)[:, None] * inv
    ac = (t % GRID_W)[:, None] * inv
    ang = np.concatenate([ar, ar, ac, ac] * 2, axis=-1)
    sign = np.where((np.arange(LANES) % 32) < 16, -1.0, 1.0)
    return jnp.asarray(np.cos(ang), F32), jnp.asarray(np.sin(ang) * sign, F32)


def kernel(x_prompt, x_sample, cache_ka, cache_va, cache_kb, cache_vb, cache_kc, cache_vc, c, c_ctx, w_ada, b_ada, ln_g, ln_b, w_in_even, w_out_even, qnorm_a, knorm_a, sink_b, w_in_odd, w_out_odd, rpb_c, w_router, b_router, w_gate, w_up, w_down, ws_gate, ws_up, ws_down):
    D = D_MODEL
    y_p = x_prompt.reshape(P_TOK, D)
    y_s = x_sample.reshape(S_TOK, D)

    cond8 = jnp.concatenate([c_ctx[None, :], c, jnp.zeros((8 - 1 - DEC_BATCH, D), F32)], axis=0)
    ada = _adaln(cond8, w_ada, b_ada)
    mods = ada.reshape(DEPTH, 8, 6, 1, D).transpose(0, 2, 1, 3, 4)

    cos, sin_signed = _rope_tables()
    perm = _slab_perm()
    in_perm = np.concatenate([perm, 512 + perm, np.arange(1024, EVEN_IN)])
    out_perm = np.concatenate([perm, 512 + perm])
    ones_row = jnp.ones((1, LANES), F32)
    no_sink = jnp.zeros((N_HEADS_B,), F32)

    new = {}
    for l in range(DEPTH):
        mod = mods[l]
        if l % 2 == 0:
            e = l // 2
            w_in = w_in_even[e][:, in_perm].astype(BF16)
            w_out = w_out_even[e][out_perm, :].astype(BF16)
            qn2 = jnp.tile(qnorm_a[e], 2)[None, :]
            kn2 = jnp.tile(knorm_a[e], 2)[None, :]
            proj = functools.partial(_project, w_bf16=w_in, qn2=qn2, kn2=kn2, cos=cos, sin_signed=sin_signed,
                                     dq=1024, dk=256, dv=256, q_norm=512, k_norm=128)
            q_p, k_p, v_p = proj(y_p, mod, rope=False, latent=False)
            q_s, k_s, v_s = proj(y_s, mod, rope=True, latent=True)
            new["ka"], new["kb"] = k_p[:, :LANES], k_p[:, LANES:]
            new["va"], new["vb"] = v_p[:, :LANES], v_p[:, LANES:]
            sink = sink_b[e].astype(F32)
            groups = (((0, 1, 2, 3), 0, None), ((4, 5, 6, 7), 1, tuple(range(N_HEADS_B))))
            o_p = _ctx_attention(q_p, k_p, v_p, sink, groups=groups)

            k_s3 = k_s.reshape(DEC_BATCH, DEC_SEQ, 2 * LANES)
            v_s3 = v_s.reshape(DEC_BATCH, DEC_SEQ, 2 * LANES)
            cka = cache_ka[:, e].reshape(DEC_BATCH, PAST_LEN, LANES).astype(BF16)
            cva = cache_va[:, e].reshape(DEC_BATCH, PAST_LEN, LANES).astype(BF16)
            ckb = cache_kb[:, e].reshape(DEC_BATCH, PAST_LEN, LANES).astype(BF16)
            cvb = cache_vb[:, e].reshape(DEC_BATCH, PAST_LEN, LANES).astype(BF16)
            pad = jnp.zeros((DEC_BATCH, WINDOW, LANES), BF16)
            o_a = _global_attention(q_s, jnp.concatenate([cka, k_s3[:, :, :LANES]], axis=1),
                                    jnp.concatenate([cva, v_s3[:, :, :LANES]], axis=1))
            o_b = _window_attention(q_s, jnp.concatenate([ckb, k_s3[:, :, LANES:], pad], axis=1),
                                    jnp.concatenate([cvb, v_s3[:, :, LANES:], pad], axis=1), sink)
            parts_p = [(o_p, 0), (o_p, 1)]
            parts_s = [(o_a, 0), (o_b, 0)]
        else:
            o = l // 2
            w_in = w_in_odd[o].astype(BF16)
            w_out = w_out_odd[o].astype(BF16)
            proj = functools.partial(_project, w_bf16=w_in, qn2=ones_row, kn2=ones_row, cos=cos,
                                     sin_signed=sin_signed, dq=1024, dk=1024, dv=1024, q_norm=0, k_norm=0,
                                     rope=False)
            q_p, k_p, v_p = proj(y_p, mod, latent=False)
            q_s, k_s, v_s = proj(y_s, mod, latent=True)
            new["kc"], new["vc"] = k_p, v_p
            groups = tuple(((j,), j, None) for j in range(D // LANES))
            o_p = _ctx_attention(q_p, k_p, v_p, no_sink, groups=groups)
            kc = cache_kc[:, o].reshape(DEC_BATCH, PAST_LEN, D).astype(BF16)
            vc = cache_vc[:, o].reshape(DEC_BATCH, PAST_LEN, D).astype(BF16)
            o_s = _na_attention(q_s, k_s, v_s, kc, vc, _na_bias_table(rpb_c[o]))
            parts_p = [(o_p, 0)]
            parts_s = [(o_s, 0)]

        wr = w_router[l].T
        wr_hi = wr.astype(BF16)
        wr_lo = (wr - wr_hi.astype(F32)).astype(BF16)
        post = functools.partial(_post_mixer, mod=mod, w_out_bf16=w_out, ln_g=ln_g[l, 0][None, :],
                                 ln_b=ln_b[l, 0][None, :], wr_hi=wr_hi, wr_lo=wr_lo, b_router=b_router[l][:, None])
        y_p, h_p, g_p = post(parts_p, y_p, latent=False)
        y_s, h_s, g_s = post(parts_s, y_s, latent=True)
        moe = functools.partial(_moe, mod=mod, w_gate=w_gate, w_up=w_up, w_down=w_down,
                                ws_gate=ws_gate, ws_up=ws_up, ws_down=ws_down,
                                ln_g=ln_g[l, 1][None, :], ln_b=ln_b[l, 1][None, :], layer=l)
        y_p = moe(h_p, g_p, y_p, latent=False)
        y_s = moe(h_s, g_s, y_s, latent=True)

    kv_a = (BATCH, 1, SEQ, N_KV_A, HEAD_DIM)
    kv_c = (BATCH, 1, SEQ, N_HEADS_C, HEAD_DIM)
    return (y_p.reshape(BATCH, SEQ, D), y_s.reshape(DEC_BATCH, DEC_SEQ, D),
            new["ka"].reshape(kv_a), new["va"].reshape(kv_a), new["kb"].reshape(kv_a), new["vb"].reshape(kv_a),
            new["kc"].reshape(kv_c), new["vc"].reshape(kv_c))
```

```python
import functools
import math

import numpy as np
import jax
import jax.numpy as jnp
from jax import lax
from jax.experimental import pallas as pl
from jax.experimental.pallas import tpu as pltpu

F32 = jnp.float32
BF16 = jnp.bfloat16

D_MODEL = 1024
BATCH = 16
SEQ = 256
DEPTH = 2
DEC_BATCH = 2
DEC_SEQ = 4096
PAST_LEN = 512
GRID_W = 64
HEAD_DIM = 64
N_HEADS_A = 8
N_KV_A = 2
N_HEADS_B = 8
N_KV_B = 2
N_HEADS_C = 16
EVEN_IN = 1536
ODD_IN = 3072
WINDOW = 128
NA_ROWS = 8
NA_COLS = 16
ROPE_THETA = 10000.0
N_EXPERTS = 64
TOP_K = 8
D_EXPERT = 128
ROUTED_SCALE = 2.5
ALPHA = (2 * DEPTH) ** 0.25
LN_EPS = 1e-6
RMS_EPS = 1e-6
NEG_BIG = -1e30
LOG2E = math.log2(math.e)

LANES = 128
P_TOK = BATCH * SEQ
S_TOK = DEC_BATCH * DEC_SEQ
TM = 512
TM_MOE = 1024
EXPERTS_PER_STEP = 8
NA_QROWS = 4
NA_KROWS = 12
NA_SLABS_PER_STEP = 8
WINDOW_BLOCKS_PER_STEP = 4
VMEM_LIMIT = 56 * 1024 * 1024

_NT = (((1,), (1,)), ((), ()))


def _cparams(sem):
    return pltpu.CompilerParams(dimension_semantics=sem, vmem_limit_bytes=VMEM_LIMIT)


def _half_masks(dtype):
    lane = lax.broadcasted_iota(jnp.int32, (1, LANES), 1)
    lo = jnp.where(lane < HEAD_DIM, 1.0, 0.0).astype(dtype)
    hi = jnp.where(lane < HEAD_DIM, 0.0, 1.0).astype(dtype)
    return lo, hi


def _ada_kernel(c_ref, w_ref, b_ref, o_ref):
    c = c_ref[...]
    a = c / (1.0 + jnp.exp(-c))
    w = w_ref[0]
    a_hi = a.astype(BF16)
    a_lo = (a - a_hi.astype(F32)).astype(BF16)
    w_hi = w.astype(BF16)
    w_lo = (w - w_hi.astype(F32)).astype(BF16)
    acc = jnp.dot(a_hi, w_hi, preferred_element_type=F32)
    acc += jnp.dot(a_lo, w_hi, preferred_element_type=F32)
    acc += jnp.dot(a_hi, w_lo, preferred_element_type=F32)
    o_ref[0] = acc + b_ref[0]


def _adaln(cond8, w_ada, b_ada):
    tn = 1536
    return pl.pallas_call(
        _ada_kernel,
        grid=(DEPTH, 6 * D_MODEL // tn),
        in_specs=[
            pl.BlockSpec((8, D_MODEL), lambda l, j: (0, 0)),
            pl.BlockSpec((1, D_MODEL, tn), lambda l, j: (l, 0, j)),
            pl.BlockSpec((1, 1, tn), lambda l, j: (l, 0, j)),
        ],
        out_specs=pl.BlockSpec((1, 8, tn), lambda l, j: (l, 0, j)),
        out_shape=jax.ShapeDtypeStruct((DEPTH, 8, 6 * D_MODEL), F32),
        compiler_params=_cparams(("parallel", "parallel")),
        name="adaln",
    )(cond8, w_ada, b_ada.reshape(DEPTH, 1, 6 * D_MODEL))


def _group_sum_matrix():
    r = lax.broadcasted_iota(jnp.int32, (LANES, LANES), 0) // HEAD_DIM
    c = lax.broadcasted_iota(jnp.int32, (LANES, LANES), 1) // HEAD_DIM
    return jnp.where(r == c, 1.0, 0.0).astype(BF16)


def _rms_slab(t, g, gmat):
    sq = t * t
    hi = sq.astype(BF16)
    lo = (sq - hi.astype(F32)).astype(BF16)
    ss = jnp.dot(hi, gmat, preferred_element_type=F32) + jnp.dot(lo, gmat, preferred_element_type=F32)
    return t * lax.rsqrt(ss * (1.0 / HEAD_DIM) + RMS_EPS) * g


def _rope_slab(t, cos, sin_signed, first):
    r = jnp.where(first, pltpu.roll(t, LANES - 16, 1), pltpu.roll(t, 16, 1))
    return t * cos + r * sin_signed


def _proj_kernel(x_ref, mod_ref, w_ref, qn_ref, kn_ref, cos_ref, sin_ref, q_ref, k_ref, v_ref,
                 *, dq, dk, q_norm, k_norm, rope):
    x = x_ref[...]
    h = (x * (1.0 + mod_ref[1]) + mod_ref[0]).astype(BF16)
    y = jnp.dot(h, w_ref[...], preferred_element_type=F32)
    tm = x.shape[0]
    gmat = _group_sum_matrix() if (q_norm or k_norm) else None
    if rope:
        cos = cos_ref[...]
        sin_signed = sin_ref[...]
        first = (lax.broadcasted_iota(jnp.int32, (tm, LANES), 1) % 32) < 16
    for s in range(dq // LANES):
        t = y[:, s * LANES:(s + 1) * LANES]
        if s * LANES < q_norm:
            t = _rms_slab(t, qn_ref[...], gmat)
        if rope:
            t = _rope_slab(t, cos, sin_signed, first)
        q_ref[:, s * LANES:(s + 1) * LANES] = (t * (HEAD_DIM ** -0.5 * LOG2E)).astype(q_ref.dtype)
    for s in range(dk // LANES):
        t = y[:, dq + s * LANES:dq + (s + 1) * LANES]
        if s * LANES < k_norm:
            t = _rms_slab(t, kn_ref[...], gmat)
        if rope:
            t = _rope_slab(t, cos, sin_signed, first)
        k_ref[:, s * LANES:(s + 1) * LANES] = t.astype(k_ref.dtype)
    v_ref[...] = y[:, dq + dk:].astype(v_ref.dtype)


def _project(x2d, mod, w_bf16, qn2, kn2, cos, sin_signed, *, dq, dk, dv, q_norm, k_norm, rope, latent):
    n = x2d.shape[0]
    per_batch = DEC_SEQ // TM
    if latent:
        mod_map = lambda i: (0, 1 + i // per_batch, 0, 0)
        pos_map = lambda i: (i % per_batch, 0)
        kv_dtype = BF16
    else:
        mod_map = lambda i: (0, 0, 0, 0)
        pos_map = lambda i: (0, 0)
        kv_dtype = F32
    kern = functools.partial(_proj_kernel, dq=dq, dk=dk, q_norm=q_norm, k_norm=k_norm, rope=rope)
    return pl.pallas_call(
        kern,
        grid=(n // TM,),
        in_specs=[
            pl.BlockSpec((TM, D_MODEL), lambda i: (i, 0)),
            pl.BlockSpec((6, None, 1, D_MODEL), mod_map),
            pl.BlockSpec((D_MODEL, dq + dk + dv), lambda i: (0, 0)),
            pl.BlockSpec((1, LANES), lambda i: (0, 0)),
            pl.BlockSpec((1, LANES), lambda i: (0, 0)),
            pl.BlockSpec((TM, LANES), pos_map),
            pl.BlockSpec((TM, LANES), pos_map),
        ],
        out_specs=[
            pl.BlockSpec((TM, dq), lambda i: (i, 0)),
            pl.BlockSpec((TM, dk), lambda i: (i, 0)),
            pl.BlockSpec((TM, dv), lambda i: (i, 0)),
        ],
        out_shape=[
            jax.ShapeDtypeStruct((n, dq), BF16),
            jax.ShapeDtypeStruct((n, dk), kv_dtype),
            jax.ShapeDtypeStruct((n, dv), kv_dtype),
        ],
        compiler_params=_cparams(("parallel",)),
        name="in_proj",
    )(x2d, mod, w_bf16, qn2, kn2, cos, sin_signed)


def _stack_heads(q_ref, slabs, lo, hi):
    qs = [q_ref[:, j * LANES:(j + 1) * LANES] for j in slabs]
    return jnp.concatenate([q * lo for q in qs] + [q * hi for q in qs], axis=0)


def _tile_lanes(x, n):
    return jnp.concatenate([x] * n, axis=1)


def _pv_with_denominator(p, v, lo, hi):
    half = p.shape[0] // 2
    pv_lo = jnp.dot(p[:half], v * lo + hi, preferred_element_type=F32)
    pv_hi = jnp.dot(p[half:], v * hi + lo, preferred_element_type=F32)
    return jnp.concatenate([pv_lo, pv_hi], axis=0)


def _normalize_store(o, o_ref, slabs, tq, lane_lo):
    half = len(slabs) * tq
    for n, j in enumerate(slabs):
        o_lo = o[n * tq:(n + 1) * tq]
        o_hi = o[half + n * tq:half + (n + 1) * tq]
        o_lo = o_lo * (1.0 / pltpu.roll(o_lo, HEAD_DIM, 1))
        o_hi = o_hi * (1.0 / pltpu.roll(o_hi, HEAD_DIM, 1))
        o_ref[:, j * LANES:(j + 1) * LANES] = jnp.where(lane_lo, o_lo, o_hi).astype(o_ref.dtype)


def _sink_rows(sink_ref, heads, tq):
    return jnp.concatenate([jnp.full((tq, LANES), sink_ref[h] * LOG2E, F32) for h in heads], axis=0)


def _softmax_pv(parts, sink, lo, hi):
    rows = parts[0][0].shape[0]
    m = jnp.full((rows, LANES), NEG_BIG, F32) if sink is None else sink
    for s, _ in parts:
        m = jnp.maximum(m, s.max(axis=1, keepdims=True))
    o = None
    for s, v in parts:
        p = jnp.exp2(s - _tile_lanes(m, s.shape[1] // LANES)).astype(BF16)
        pv = _pv_with_denominator(p, v, lo, hi)
        o = pv if o is None else o + pv
    if sink is not None:
        lo32, hi32 = _half_masks(F32)
        e = jnp.exp2(sink - m)
        half = rows // 2
        o = o + jnp.concatenate([e[:half] * hi32, e[half:] * lo32], axis=0)
    return o


def _ctx_attn_kernel(sink_ref, q_ref, k_ref, v_ref, o_ref, *, groups):
    lo, hi = _half_masks(BF16)
    lo32, hi32 = _half_masks(F32)
    lane_lo = lax.broadcasted_iota(jnp.int32, (SEQ, LANES), 1) < HEAD_DIM
    units = [(b, grp) for b in range(q_ref.shape[0] // SEQ) for grp in groups]

    def scores(unit):
        b, (slabs, ks, _) = unit
        kk = k_ref[b * SEQ:(b + 1) * SEQ, ks * LANES:(ks + 1) * LANES].astype(BF16)
        return lax.dot_general(_stack_heads(q_ref.at[b * SEQ:(b + 1) * SEQ], slabs, lo, hi), kk, _NT,
                               preferred_element_type=F32)

    def probs(unit, s):
        sink_heads = unit[1][2]
        if sink_heads is None:
            m = jnp.maximum(jnp.full((s.shape[0], LANES), NEG_BIG, F32), s.max(axis=1, keepdims=True))
            extra = None
        else:
            sink = _sink_rows(sink_ref, sink_heads, SEQ)
            m = jnp.maximum(sink, s.max(axis=1, keepdims=True))
            e = jnp.exp2(sink - m)
            half = s.shape[0] // 2
            extra = jnp.concatenate([e[:half] * hi32, e[half:] * lo32], axis=0)
        return jnp.exp2(s - _tile_lanes(m, s.shape[1] // LANES)).astype(BF16), extra

    def output(unit, p, extra):
        b, (slabs, ks, _) = unit
        vv = v_ref[b * SEQ:(b + 1) * SEQ, ks * LANES:(ks + 1) * LANES].astype(BF16)
        o = _pv_with_denominator(p, vv, lo, hi)
        if extra is not None:
            o = o + extra
        _normalize_store(o, o_ref.at[b * SEQ:(b + 1) * SEQ], slabs, SEQ, lane_lo)

    s_live, p_live = {}, {}
    for t in range(len(units) + 2):
        if t < len(units):
            s_live[t] = scores(units[t])
        if 1 <= t <= len(units):
            p_live[t - 1] = probs(units[t - 1], s_live.pop(t - 1))
        if t >= 2:
            output(units[t - 2], *p_live.pop(t - 2))


def _ctx_attention(q, k, v, sink, *, groups, seqs):
    kw = k.shape[1]
    rows = seqs * SEQ
    return pl.pallas_call(
        functools.partial(_ctx_attn_kernel, groups=groups),
        grid=(BATCH // seqs,),
        in_specs=[
            pl.BlockSpec(memory_space=pltpu.SMEM),
            pl.BlockSpec((rows, D_MODEL), lambda b: (b, 0)),
            pl.BlockSpec((rows, kw), lambda b: (b, 0)),
            pl.BlockSpec((rows, kw), lambda b: (b, 0)),
        ],
        out_specs=pl.BlockSpec((rows, D_MODEL), lambda b: (b, 0)),
        out_shape=jax.ShapeDtypeStruct((P_TOK, D_MODEL), BF16),
        compiler_params=_cparams(("parallel",)),
        name="ctx_attn",
    )(sink, q, k, v)


def _global_attn_kernel(q_ref, k_ref, v_ref, o_ref, s0, s1, p0, p1, a0, a1, m_ref, acc_ref, *, tk):
    tq = q_ref.shape[0]
    slabs = tuple(range(q_ref.shape[1] // LANES))
    nk = k_ref.shape[0] // tk
    lo, hi = _half_masks(BF16)
    lane_lo = lax.broadcasted_iota(jnp.int32, (tq, LANES), 1) < HEAD_DIM
    qstack = _stack_heads(q_ref, slabs, lo, hi)
    m_ref[...] = jnp.full(m_ref.shape, NEG_BIG, F32)
    acc_ref[...] = jnp.zeros(acc_ref.shape, F32)
    s_bufs, p_bufs, a_bufs = (s0, s1), (p0, p1), (a0, a1)

    def stage_a(c):
        s_bufs[c % 2][...] = lax.dot_general(qstack, k_ref[c * tk:(c + 1) * tk, :], _NT, preferred_element_type=F32)

    def stage_b(c):
        s = s_bufs[c % 2][...]
        m_old = m_ref[...]
        m_new = jnp.maximum(m_old, s.max(axis=1, keepdims=True))
        a_bufs[c % 2][...] = jnp.exp2(m_old - m_new)
        p_bufs[c % 2][...] = jnp.exp2(s - _tile_lanes(m_new, tk // LANES)).astype(BF16)
        m_ref[...] = m_new

    def stage_c(c):
        pv = _pv_with_denominator(p_bufs[c % 2][...], v_ref[c * tk:(c + 1) * tk, :], lo, hi)
        acc_ref[...] = a_bufs[c % 2][...] * acc_ref[...] + pv

    for t in range(nk + 2):
        if t < nk:
            stage_a(t)
        if 1 <= t <= nk:
            stage_b(t - 1)
        if t >= 2:
            stage_c(t - 2)
    _normalize_store(acc_ref[...], o_ref, slabs, tq, lane_lo)


def _global_attention(q, kcat, vcat):
    tq, tk = 256, 768
    t = kcat.shape[1]
    per_batch = DEC_SEQ // tq
    rows = N_HEADS_A * tq
    kv_spec = pl.BlockSpec((None, t, LANES), lambda b, i: (b, 0, 0))
    return pl.pallas_call(
        functools.partial(_global_attn_kernel, tk=tk),
        grid=(DEC_BATCH, per_batch),
        in_specs=[pl.BlockSpec((tq, 4 * LANES), lambda b, i: (b * per_batch + i, 0)), kv_spec, kv_spec],
        out_specs=pl.BlockSpec((tq, 4 * LANES), lambda b, i: (b * per_batch + i, 0)),
        out_shape=jax.ShapeDtypeStruct((S_TOK, 4 * LANES), BF16),
        scratch_shapes=[pltpu.VMEM((rows, tk), F32), pltpu.VMEM((rows, tk), F32),
                        pltpu.VMEM((rows, tk), BF16), pltpu.VMEM((rows, tk), BF16),
                        pltpu.VMEM((rows, LANES), F32), pltpu.VMEM((rows, LANES), F32),
                        pltpu.VMEM((rows, LANES), F32), pltpu.VMEM((rows, LANES), F32)],
        compiler_params=_cparams(("parallel", "parallel")),
        name="global_attn",
    )(q, kcat, vcat)


def _window_attn_kernel(sink_ref, q_ref, k_ref, v_ref, o_ref):
    tq = WINDOW
    n_blocks = q_ref.shape[0] // tq
    slabs = tuple(range(q_ref.shape[1] // LANES))
    n_heads = 2 * len(slabs)
    span = tq + 2 * WINDOW
    lo, hi = _half_masks(BF16)
    lo32, hi32 = _half_masks(F32)
    lane_lo = lax.broadcasted_iota(jnp.int32, (tq, LANES), 1) < HEAD_DIM
    row = lax.broadcasted_iota(jnp.int32, (tq, span), 0)
    col = lax.broadcasted_iota(jnp.int32, (tq, span), 1)
    band = (col >= row) & (col <= row + 2 * WINDOW)
    sink = _sink_rows(sink_ref, range(n_heads), tq)

    def local_offset(u):
        i = pl.program_id(1) * n_blocks + u
        return i, pl.multiple_of(PAST_LEN - WINDOW + i * tq, LANES)

    def scores(u):
        i, off = local_offset(u)
        kpos = i * tq - WINDOW + col
        valid = band & (kpos >= 0) & (kpos < DEC_SEQ)
        qstack = _stack_heads(q_ref.at[u * tq:(u + 1) * tq], slabs, lo, hi)
        s_loc = lax.dot_general(qstack, k_ref[pl.ds(off, span), :], _NT, preferred_element_type=F32)
        s_loc = jnp.where(valid[None], s_loc.reshape(n_heads, tq, span), NEG_BIG).reshape(n_heads * tq, span)
        return [s_loc, lax.dot_general(qstack, k_ref[0:PAST_LEN, :], _NT, preferred_element_type=F32)]

    def probs(s_parts):
        m = sink
        for s in s_parts:
            m = jnp.maximum(m, s.max(axis=1, keepdims=True))
        e = jnp.exp2(sink - m)
        half = e.shape[0] // 2
        extra = jnp.concatenate([e[:half] * hi32, e[half:] * lo32], axis=0)
        return [jnp.exp2(s - _tile_lanes(m, s.shape[1] // LANES)).astype(BF16) for s in s_parts], extra

    def output(u, p_parts, extra):
        _, off = local_offset(u)
        o = (_pv_with_denominator(p_parts[0], v_ref[pl.ds(off, span), :], lo, hi)
             + _pv_with_denominator(p_parts[1], v_ref[0:PAST_LEN, :], lo, hi) + extra)
        _normalize_store(o, o_ref.at[u * tq:(u + 1) * tq], slabs, tq, lane_lo)

    s_live, p_live = {}, {}
    for t in range(n_blocks + 2):
        if t < n_blocks:
            s_live[t] = scores(t)
        if 1 <= t <= n_blocks:
            p_live[t - 1] = probs(s_live.pop(t - 1))
        if t >= 2:
            output(t - 2, *p_live.pop(t - 2))


def _window_attention(q, kcat, vcat, sink):
    tq = WINDOW_BLOCKS_PER_STEP * WINDOW
    t = kcat.shape[1]
    per_batch = DEC_SEQ // tq
    return pl.pallas_call(
        _window_attn_kernel,
        grid=(DEC_BATCH, per_batch),
        in_specs=[
            pl.BlockSpec(memory_space=pltpu.SMEM),
            pl.BlockSpec((tq, 4 * LANES), lambda b, i: (b * per_batch + i, 1)),
            pl.BlockSpec((None, t, LANES), lambda b, i: (b, 0, 0)),
            pl.BlockSpec((None, t, LANES), lambda b, i: (b, 0, 0)),
        ],
        out_specs=pl.BlockSpec((tq, 4 * LANES), lambda b, i: (b * per_batch + i, 0)),
        out_shape=jax.ShapeDtypeStruct((S_TOK, 4 * LANES), BF16),
        compiler_params=_cparams(("parallel", "parallel")),
        name="window_attn",
    )(sink, q, kcat, vcat)


def _na_start_row(g):
    return jnp.clip(NA_QROWS * g - NA_ROWS // 2, 0, DEC_SEQ // GRID_W - NA_KROWS)


def _na_step_bias(t_ref, half, g, lane_lo64):
    rows = DEC_SEQ // GRID_W
    start = _na_start_row(g)
    d0 = start - NA_QROWS * g + NA_ROWS - 1
    row_blocks = []
    for qr in range(NA_QROWS):
        rs = jnp.clip(NA_QROWS * g + qr - NA_ROWS // 2, 0, rows - NA_ROWS)
        blocks = []
        for m in range(NA_KROWS // 2):
            tiles = []
            for kr in (2 * m, 2 * m + 1):
                krow = start + kr
                inside = (krow >= rs) & (krow < rs + NA_ROWS)
                dr = jnp.clip(kr - qr + d0, 0, 2 * NA_ROWS - 2)
                tiles.append(t_ref[half, dr] + jnp.where(inside, 0.0, NEG_BIG))
            blocks.append(jnp.where(lane_lo64, tiles[0], tiles[1]))
        row_blocks.append(jnp.concatenate(blocks, axis=1))
    return jnp.concatenate(row_blocks, axis=0)


def _na_attn_kernel(q_ref, k_ref, v_ref, kc_ref, vc_ref, t_ref, o_ref):
    g = pl.program_id(2)
    tq = q_ref.shape[0]
    lo, hi = _half_masks(BF16)
    lane_lo = lax.broadcasted_iota(jnp.int32, (tq, LANES), 1) < HEAD_DIM
    lane_lo64 = lax.broadcasted_iota(jnp.int32, (GRID_W, LANES), 1) < HEAD_DIM
    off = pl.multiple_of(_na_start_row(g) * GRID_W, GRID_W)
    n_slabs = q_ref.shape[1] // LANES
    cols = [slice(n * LANES, (n + 1) * LANES) for n in range(n_slabs)]

    def scores(n):
        qstack = _stack_heads(q_ref, (n,), lo, hi)
        bias = jnp.concatenate([_na_step_bias(t_ref, 2 * n + half, g, lane_lo64) for half in range(2)], axis=0)
        k_loc = k_ref[pl.ds(off, NA_KROWS * GRID_W), cols[n]]
        s_loc = lax.dot_general(qstack, k_loc, _NT, preferred_element_type=F32) + bias
        return [s_loc, lax.dot_general(qstack, kc_ref[:, cols[n]], _NT, preferred_element_type=F32)]

    def probs(s_parts):
        m = jnp.full((2 * tq, LANES), NEG_BIG, F32)
        for s in s_parts:
            m = jnp.maximum(m, s.max(axis=1, keepdims=True))
        return [jnp.exp2(s - _tile_lanes(m, s.shape[1] // LANES)).astype(BF16) for s in s_parts]

    def output(n, p_parts):
        v_loc = v_ref[pl.ds(off, NA_KROWS * GRID_W), cols[n]]
        o = _pv_with_denominator(p_parts[0], v_loc, lo, hi) + _pv_with_denominator(p_parts[1], vc_ref[:, cols[n]], lo, hi)
        _normalize_store(o, o_ref, (n,), tq, lane_lo)

    s_live, p_live = {}, {}
    for t in range(n_slabs + 2):
        if t < n_slabs:
            s_live[t] = scores(t)
        if 1 <= t <= n_slabs:
            p_live[t - 1] = probs(s_live.pop(t - 1))
        if t >= 2:
            output(t - 2, p_live.pop(t - 2))


def _na_bias_table(rpb):
    cols = np.arange(GRID_W)
    cs = np.clip(cols - NA_COLS // 2, 0, GRID_W - NA_COLS)
    dc = cols[None, :] - cols[:, None] + NA_COLS - 1
    inside = (cols[None, :] >= cs[:, None]) & (cols[None, :] < cs[:, None] + NA_COLS)
    onehot = (np.arange(2 * NA_COLS - 1)[:, None, None] == dc[None]) & inside[None]
    t = jnp.einsum("hdc,cqk->hdqk", rpb.astype(F32), jnp.asarray(onehot, F32), precision=lax.Precision.HIGHEST)
    t = jnp.where(inside[None, None], t * LOG2E, NEG_BIG)
    return jnp.concatenate([t, t], axis=-1)


def _na_attention(q, k, v, kc, vc, bias):
    tq = NA_QROWS * GRID_W
    n_groups = DEC_SEQ // tq
    w = NA_SLABS_PER_STEP * LANES
    k3 = k.reshape(DEC_BATCH, DEC_SEQ, D_MODEL)
    v3 = v.reshape(DEC_BATCH, DEC_SEQ, D_MODEL)
    tok = lambda s, b, g: (b * n_groups + g, s)
    per_batch = lambda s, b, g: (b, 0, s)
    once = pl.Buffered(1)
    return pl.pallas_call(
        _na_attn_kernel,
        grid=(D_MODEL // w, DEC_BATCH, n_groups),
        in_specs=[
            pl.BlockSpec((tq, w), tok),
            pl.BlockSpec((None, DEC_SEQ, w), per_batch, pipeline_mode=once),
            pl.BlockSpec((None, DEC_SEQ, w), per_batch, pipeline_mode=once),
            pl.BlockSpec((None, PAST_LEN, w), per_batch, pipeline_mode=once),
            pl.BlockSpec((None, PAST_LEN, w), per_batch, pipeline_mode=once),
            pl.BlockSpec((2 * NA_SLABS_PER_STEP, 2 * NA_ROWS - 1, GRID_W, LANES), lambda s, b, g: (s, 0, 0, 0),
                         pipeline_mode=once),
        ],
        out_specs=pl.BlockSpec((tq, w), tok),
        out_shape=jax.ShapeDtypeStruct((S_TOK, D_MODEL), BF16),
        compiler_params=_cparams(("parallel", "parallel", "parallel")),
        name="na_attn",
    )(q, k3, v3, kc, vc, bias)


def _layer_norm(z, g, b):
    mu = jnp.mean(z, axis=-1, keepdims=True)
    zc = z - mu
    var = jnp.mean(zc * zc, axis=-1, keepdims=True)
    return zc * lax.rsqrt(var + LN_EPS) * g + b


def _post_kernel(*refs, n_parts):
    o_refs = refs[:n_parts]
    (y_ref, mod_ref, w_ref, lng_ref, lnb_ref, wr_hi_ref, wr_lo_ref, br_ref,
     y_out_ref, h_out_ref, gates_ref) = refs[n_parts:]
    pw = D_MODEL // n_parts
    mix = None
    for p, o_ref in enumerate(o_refs):
        part = jnp.dot(o_ref[...], w_ref[p * pw:(p + 1) * pw, :], preferred_element_type=F32)
        mix = part if mix is None else mix + part
    y = _layer_norm(ALPHA * y_ref[...] + mod_ref[2] * mix, lng_ref[...], lnb_ref[...])
    y_out_ref[...] = y
    h = y * (1.0 + mod_ref[4]) + mod_ref[3]
    h_hi = h.astype(BF16)
    h_out_ref[...] = h_hi
    h_lo = (h - h_hi.astype(F32)).astype(BF16)
    logits = lax.dot_general(wr_hi_ref[...], h_hi, _NT, preferred_element_type=F32)
    logits += lax.dot_general(wr_hi_ref[...], h_lo, _NT, preferred_element_type=F32)
    logits += lax.dot_general(wr_lo_ref[...], h_hi, _NT, preferred_element_type=F32)
    scores = 1.0 / (1.0 + jnp.exp(-logits))
    work = scores + br_ref[...]
    expert = lax.broadcasted_iota(jnp.int32, work.shape, 0).astype(F32)
    chosen = jnp.zeros(work.shape, F32)
    for _ in range(TOP_K):
        mx = work.max(axis=0, keepdims=True)
        first = jnp.where(work == mx, expert, float(N_EXPERTS)).min(axis=0, keepdims=True)
        pick = expert == first
        chosen = jnp.where(pick, scores, chosen)
        work = jnp.where(pick, NEG_BIG, work)
    gates_t = chosen / chosen.sum(axis=0, keepdims=True) * ROUTED_SCALE
    padded = jnp.concatenate([gates_t, jnp.zeros((LANES - N_EXPERTS, gates_t.shape[1]), F32)], axis=0)
    gates_ref[...] = padded.T[:, :N_EXPERTS]


def _post_mixer(o_parts, y2d, mod, w_out_bf16, ln_g, ln_b, wr_hi, wr_lo, b_router, *, latent):
    n = y2d.shape[0]
    n_parts = len(o_parts)
    pw = D_MODEL // n_parts
    per_batch = DEC_SEQ // TM
    mod_map = (lambda i: (0, 1 + i // per_batch, 0, 0)) if latent else (lambda i: (0, 0, 0, 0))
    o_specs = [pl.BlockSpec((TM, pw), functools.partial(lambda i, c: (i, c), c=col)) for _, col in o_parts]
    full = lambda i: (0, 0)
    return pl.pallas_call(
        functools.partial(_post_kernel, n_parts=n_parts),
        grid=(n // TM,),
        in_specs=o_specs + [
            pl.BlockSpec((TM, D_MODEL), lambda i: (i, 0)),
            pl.BlockSpec((6, None, 1, D_MODEL), mod_map),
            pl.BlockSpec((D_MODEL, D_MODEL), full),
            pl.BlockSpec((1, D_MODEL), full),
            pl.BlockSpec((1, D_MODEL), full),
            pl.BlockSpec((N_EXPERTS, D_MODEL), full),
            pl.BlockSpec((N_EXPERTS, D_MODEL), full),
            pl.BlockSpec((N_EXPERTS, 1), full),
        ],
        out_specs=[
            pl.BlockSpec((TM, D_MODEL), lambda i: (i, 0)),
            pl.BlockSpec((TM, D_MODEL), lambda i: (i, 0)),
            pl.BlockSpec((TM, N_EXPERTS), lambda i: (i, 0)),
        ],
        out_shape=[
            jax.ShapeDtypeStruct((n, D_MODEL), F32),
            jax.ShapeDtypeStruct((n, D_MODEL), BF16),
            jax.ShapeDtypeStruct((n, N_EXPERTS), F32),
        ],
        compiler_params=_cparams(("parallel",)),
        name="post_mixer",
    )(*[a for a, _ in o_parts], y2d, mod, w_out_bf16, ln_g, ln_b, wr_hi, wr_lo, b_router)


def _silu(x):
    return x / (1.0 + jnp.exp(-x))


def _moe_kernel(x_ref, gates_ref, y_ref, mod_ref, wg_ref, wu_ref, wd_ref, sg_ref, su_ref, sd_ref,
                lng_ref, lnb_ref, o_ref, acc_ref):
    e = pl.program_id(1)
    x = x_ref[...]

    @pl.when(e == 0)
    def _():
        a = _silu(jnp.dot(x, sg_ref[...].astype(BF16), preferred_element_type=F32))
        a = a * jnp.dot(x, su_ref[...].astype(BF16), preferred_element_type=F32)
        acc_ref[...] = jnp.dot(a.astype(BF16), sd_ref[...].astype(BF16), preferred_element_type=F32)

    gates = gates_ref[...]
    lane = lax.broadcasted_iota(jnp.int32, gates.shape, 1)
    acts = []
    for j in range(EXPERTS_PER_STEP):
        w1 = jnp.concatenate([wg_ref[j].astype(BF16), wu_ref[j].astype(BF16)], axis=1)
        hcat = jnp.dot(x, w1, preferred_element_type=F32)
        ge = jnp.sum(jnp.where(lane == e * EXPERTS_PER_STEP + j, gates, 0.0), axis=1, keepdims=True)
        acts.append((_silu(hcat[:, :D_EXPERT]) * hcat[:, D_EXPERT:] * ge).astype(BF16))
    w2 = jnp.concatenate([wd_ref[j].astype(BF16) for j in range(EXPERTS_PER_STEP)], axis=0)
    acc_ref[...] += jnp.dot(jnp.concatenate(acts, axis=1), w2, preferred_element_type=F32)

    @pl.when(e == pl.num_programs(1) - 1)
    def _():
        o_ref[...] = _layer_norm(ALPHA * y_ref[...] + mod_ref[5] * acc_ref[...], lng_ref[...], lnb_ref[...])


def _moe(x_bf16, gates, y2d, mod, w_gate, w_up, w_down, ws_gate, ws_up, ws_down, ln_g, ln_b, *, layer, latent):
    n = x_bf16.shape[0]
    per_batch = DEC_SEQ // TM_MOE
    mod_map = (lambda i, e: (0, 1 + i // per_batch, 0, 0)) if latent else (lambda i, e: (0, 0, 0, 0))
    ep = EXPERTS_PER_STEP
    tok = lambda i, e: (i, 0)
    full = lambda i, e: (0, 0)
    routed = lambda i, e: (layer, e, 0, 0)
    shared = lambda i, e: (layer, 0, 0)
    return pl.pallas_call(
        _moe_kernel,
        grid=(n // TM_MOE, N_EXPERTS // ep),
        in_specs=[
            pl.BlockSpec((TM_MOE, D_MODEL), tok),
            pl.BlockSpec((TM_MOE, N_EXPERTS), tok),
            pl.BlockSpec((TM_MOE, D_MODEL), tok, pipeline_mode=pl.Buffered(1)),
            pl.BlockSpec((6, None, 1, D_MODEL), mod_map),
            pl.BlockSpec((None, ep, D_MODEL, D_EXPERT), routed),
            pl.BlockSpec((None, ep, D_MODEL, D_EXPERT), routed),
            pl.BlockSpec((None, ep, D_EXPERT, D_MODEL), routed),
            pl.BlockSpec((None, D_MODEL, D_EXPERT), shared),
            pl.BlockSpec((None, D_MODEL, D_EXPERT), shared),
            pl.BlockSpec((None, D_EXPERT, D_MODEL), shared),
            pl.BlockSpec((1, D_MODEL), full),
            pl.BlockSpec((1, D_MODEL), full),
        ],
        out_specs=pl.BlockSpec((TM_MOE, D_MODEL), tok),
        out_shape=jax.ShapeDtypeStruct((n, D_MODEL), F32),
        scratch_shapes=[pltpu.VMEM((TM_MOE, D_MODEL), F32)],
        compiler_params=_cparams(("parallel", "arbitrary")),
        name="moe",
    )(x_bf16, gates, y2d, mod, w_gate, w_up, w_down, ws_gate, ws_up, ws_down, ln_g, ln_b)


def _slab_perm():
    idx = []
    for j in range(4):
        for half in range(2):
            head = j + 4 * half
            idx.extend(range(head * HEAD_DIM, (head + 1) * HEAD_DIM))
    return np.asarray(idx, np.int32)


def _rope_tables():
    t = np.arange(DEC_SEQ)
    quarter = HEAD_DIM // 4
    inv = ROPE_THETA ** (-np.arange(quarter, dtype=np.float64) / quarter)
    ar = (t // GRID_W)[:, None] * inv
    ac = (t % GRID_W)[:, None] * inv
    ang = np.concatenate([ar, ar, ac, ac] * 2, axis=-1)
    sign = np.where((np.arange(LANES) % 32) < 16, -1.0, 1.0)
    return jnp.asarray(np.cos(ang), F32), jnp.asarray(np.sin(ang) * sign, F32)


def kernel(x_prompt, x_sample, cache_ka, cache_va, cache_kb, cache_vb, cache_kc, cache_vc, c, c_ctx, w_ada, b_ada, ln_g, ln_b, w_in_even, w_out_even, qnorm_a, knorm_a, sink_b, w_in_odd, w_out_odd, rpb_c, w_router, b_router, w_gate, w_up, w_down, ws_gate, ws_up, ws_down):
    D = D_MODEL
    y_p = x_prompt.reshape(P_TOK, D)
    y_s = x_sample.reshape(S_TOK, D)

    cond8 = jnp.concatenate([c_ctx[None, :], c, jnp.zeros((8 - 1 - DEC_BATCH, D), F32)], axis=0)
    ada = _adaln(cond8, w_ada, b_ada)
    mods = ada.reshape(DEPTH, 8, 6, 1, D).transpose(0, 2, 1, 3, 4)

    cos, sin_signed = _rope_tables()
    perm = _slab_perm()
    in_perm = np.concatenate([perm, 512 + perm, np.arange(1024, EVEN_IN)])
    out_perm = np.concatenate([perm, 512 + perm])
    ones_row = jnp.ones((1, LANES), F32)
    no_sink = jnp.zeros((N_HEADS_B,), F32)

    new = {}
    for l in range(DEPTH):
        mod = mods[l]
        if l % 2 == 0:
            e = l // 2
            w_in = w_in_even[e][:, in_perm].astype(BF16)
            w_out = w_out_even[e][out_perm, :].astype(BF16)
            qn2 = jnp.tile(qnorm_a[e], 2)[None, :]
            kn2 = jnp.tile(knorm_a[e], 2)[None, :]
            proj = functools.partial(_project, w_bf16=w_in, qn2=qn2, kn2=kn2, cos=cos, sin_signed=sin_signed,
                                     dq=1024, dk=256, dv=256, q_norm=512, k_norm=128)
            q_p, k_p, v_p = proj(y_p, mod, rope=False, latent=False)
            q_s, k_s, v_s = proj(y_s, mod, rope=True, latent=True)
            new["ka"], new["kb"] = k_p[:, :LANES], k_p[:, LANES:]
            new["va"], new["vb"] = v_p[:, :LANES], v_p[:, LANES:]
            sink = sink_b[e].astype(F32)
            groups = (tuple(((j,), 0, None) for j in range(4))
                      + tuple(((4 + j,), 1, (j, j + 4)) for j in range(4)))
            o_p = _ctx_attention(q_p, k_p, v_p, sink, groups=groups, seqs=4)

            k_s3 = k_s.reshape(DEC_BATCH, DEC_SEQ, 2 * LANES)
            v_s3 = v_s.reshape(DEC_BATCH, DEC_SEQ, 2 * LANES)
            cka = cache_ka[:, e].reshape(DEC_BATCH, PAST_LEN, LANES).astype(BF16)
            cva = cache_va[:, e].reshape(DEC_BATCH, PAST_LEN, LANES).astype(BF16)
            ckb = cache_kb[:, e].reshape(DEC_BATCH, PAST_LEN, LANES).astype(BF16)
            cvb = cache_vb[:, e].reshape(DEC_BATCH, PAST_LEN, LANES).astype(BF16)
            pad = jnp.zeros((DEC_BATCH, WINDOW, LANES), BF16)
            o_a = _global_attention(q_s, jnp.concatenate([cka, k_s3[:, :, :LANES]], axis=1),
                                    jnp.concatenate([cva, v_s3[:, :, :LANES]], axis=1))
            o_b = _window_attention(q_s, jnp.concatenate([ckb, k_s3[:, :, LANES:], pad], axis=1),
                                    jnp.concatenate([cvb, v_s3[:, :, LANES:], pad], axis=1), sink)
            parts_p = [(o_p, 0), (o_p, 1)]
            parts_s = [(o_a, 0), (o_b, 0)]
        else:
            o = l // 2
            w_in = w_in_odd[o].astype(BF16)
            w_out = w_out_odd[o].astype(BF16)
            proj = functools.partial(_project, w_bf16=w_in, qn2=ones_row, kn2=ones_row, cos=cos,
                                     sin_signed=sin_signed, dq=1024, dk=1024, dv=1024, q_norm=0, k_norm=0,
                                     rope=False)
            q_p, k_p, v_p = proj(y_p, mod, latent=False)
            q_s, k_s, v_s = proj(y_s, mod, latent=True)
            new["kc"], new["vc"] = k_p, v_p
            groups = tuple(((j,), j, None) for j in range(D // LANES))
            o_p = _ctx_attention(q_p, k_p, v_p, no_sink, groups=groups, seqs=4)
            kc = cache_kc[:, o].reshape(DEC_BATCH, PAST_LEN, D).astype(BF16)
            vc = cache_vc[:, o].reshape(DEC_BATCH, PAST_LEN, D).astype(BF16)
            o_s = _na_attention(q_s, k_s, v_s, kc, vc, _na_bias_table(rpb_c[o]))
            parts_p = [(o_p, 0)]
            parts_s = [(o_s, 0)]

        wr = w_router[l].T
        wr_hi = wr.astype(BF16)
        wr_lo = (wr - wr_hi.astype(F32)).astype(BF16)
        post = functools.partial(_post_mixer, mod=mod, w_out_bf16=w_out, ln_g=ln_g[l, 0][None, :],
                                 ln_b=ln_b[l, 0][None, :], wr_hi=wr_hi, wr_lo=wr_lo, b_router=b_router[l][:, None])
        y_p, h_p, g_p = post(parts_p, y_p, latent=False)
        y_s, h_s, g_s = post(parts_s, y_s, latent=True)
        moe = functools.partial(_moe, mod=mod, w_gate=w_gate, w_up=w_up, w_down=w_down,
                                ws_gate=ws_gate, ws_up=ws_up, ws_down=ws_down,
                                ln_g=ln_g[l, 1][None, :], ln_b=ln_b[l, 1][None, :], layer=l)
        y_p = moe(h_p, g_p, y_p, latent=False)
        y_s = moe(h_s, g_s, y_s, latent=True)

    kv_a = (BATCH, 1, SEQ, N_KV_A, HEAD_DIM)
    kv_c = (BATCH, 1, SEQ, N_HEADS_C, HEAD_DIM)
    return (y_p.reshape(BATCH, SEQ, D), y_s.reshape(DEC_BATCH, DEC_SEQ, D),
            new["ka"].reshape(kv_a), new["va"].reshape(kv_a), new["kb"].reshape(kv_a), new["vb"].reshape(kv_a),
            new["kc"].reshape(kv_c), new["vc"].reshape(kv_c))
```

```python
import functools
import math

import numpy as np
import jax
import jax.numpy as jnp
from jax import lax
from jax.experimental import pallas as pl
from jax.experimental.pallas import tpu as pltpu

F32 = jnp.float32
BF16 = jnp.bfloat16

D_MODEL = 1024
BATCH = 16
SEQ = 256
DEPTH = 2
DEC_BATCH = 2
DEC_SEQ = 4096
PAST_LEN = 512
GRID_W = 64
HEAD_DIM = 64
N_HEADS_A = 8
N_KV_A = 2
N_HEADS_B = 8
N_KV_B = 2
N_HEADS_C = 16
EVEN_IN = 1536
ODD_IN = 3072
WINDOW = 128
NA_ROWS = 8
NA_COLS = 16
ROPE_THETA = 10000.0
N_EXPERTS = 64
TOP_K = 8
D_EXPERT = 128
ROUTED_SCALE = 2.5
ALPHA = (2 * DEPTH) ** 0.25
LN_EPS = 1e-6
RMS_EPS = 1e-6
NEG_BIG = -1e30
LOG2E = math.log2(math.e)

LANES = 128
P_TOK = BATCH * SEQ
S_TOK = DEC_BATCH * DEC_SEQ
TM = 512
TM_MOE = 1024
EXPERTS_PER_STEP = 8
NA_QROWS = 4
NA_KROWS = 12
NA_SLABS_PER_STEP = 8
WINDOW_BLOCKS_PER_STEP = 4
VMEM_LIMIT = 56 * 1024 * 1024

_NT = (((1,), (1,)), ((), ()))


def _cparams(sem):
    return pltpu.CompilerParams(dimension_semantics=sem, vmem_limit_bytes=VMEM_LIMIT)


def _half_masks(dtype):
    lane = lax.broadcasted_iota(jnp.int32, (1, LANES), 1)
    lo = jnp.where(lane < HEAD_DIM, 1.0, 0.0).astype(dtype)
    hi = jnp.where(lane < HEAD_DIM, 0.0, 1.0).astype(dtype)
    return lo, hi


def _ada_kernel(c_ref, w_ref, b_ref, o_ref):
    c = c_ref[...]
    a = c / (1.0 + jnp.exp(-c))
    w = w_ref[0]
    a_hi = a.astype(BF16)
    a_lo = (a - a_hi.astype(F32)).astype(BF16)
    w_hi = w.astype(BF16)
    w_lo = (w - w_hi.astype(F32)).astype(BF16)
    acc = jnp.dot(a_hi, w_hi, preferred_element_type=F32)
    acc += jnp.dot(a_lo, w_hi, preferred_element_type=F32)
    acc += jnp.dot(a_hi, w_lo, preferred_element_type=F32)
    o_ref[0] = acc + b_ref[0]


def _adaln(cond8, w_ada, b_ada):
    tn = 1536
    return pl.pallas_call(
        _ada_kernel,
        grid=(DEPTH, 6 * D_MODEL // tn),
        in_specs=[
            pl.BlockSpec((8, D_MODEL), lambda l, j: (0, 0)),
            pl.BlockSpec((1, D_MODEL, tn), lambda l, j: (l, 0, j)),
            pl.BlockSpec((1, 1, tn), lambda l, j: (l, 0, j)),
        ],
        out_specs=pl.BlockSpec((1, 8, tn), lambda l, j: (l, 0, j)),
        out_shape=jax.ShapeDtypeStruct((DEPTH, 8, 6 * D_MODEL), F32),
        compiler_params=_cparams(("parallel", "parallel")),
        name="adaln",
    )(cond8, w_ada, b_ada.reshape(DEPTH, 1, 6 * D_MODEL))


def _group_sum_matrix():
    r = (lax.broadcasted_iota(jnp.int32, (2 * LANES, LANES), 0) % LANES) // HEAD_DIM
    c = lax.broadcasted_iota(jnp.int32, (2 * LANES, LANES), 1) // HEAD_DIM
    return jnp.where(r == c, 1.0, 0.0).astype(BF16)


def _rms_slab(t, g, gmat):
    sq = t * t
    hi = sq.astype(BF16)
    lo = (sq - hi.astype(F32)).astype(BF16)
    ss = jnp.dot(jnp.concatenate([hi, lo], axis=1), gmat, preferred_element_type=F32)
    return t * lax.rsqrt(ss * (1.0 / HEAD_DIM) + RMS_EPS) * g


def _rope_slab(t, cos, sin_signed, first):
    r = jnp.where(first, pltpu.roll(t, LANES - 16, 1), pltpu.roll(t, 16, 1))
    return t * cos + r * sin_signed


def _proj_kernel(x_ref, mod_ref, w_ref, qn_ref, kn_ref, cos_ref, sin_ref, q_ref, k_ref, v_ref,
                 *, dq, dk, q_norm, k_norm, rope):
    x = x_ref[...]
    h = (x * (1.0 + mod_ref[1]) + mod_ref[0]).astype(BF16)
    y = jnp.dot(h, w_ref[...], preferred_element_type=F32)
    tm = x.shape[0]
    gmat = _group_sum_matrix() if (q_norm or k_norm) else None
    if rope:
        cos = cos_ref[...]
        sin_signed = sin_ref[...]
        first = (lax.broadcasted_iota(jnp.int32, (tm, LANES), 1) % 32) < 16
    for s in range(dq // LANES):
        t = y[:, s * LANES:(s + 1) * LANES]
        if s * LANES < q_norm:
            t = _rms_slab(t, qn_ref[...], gmat)
        if rope:
            t = _rope_slab(t, cos, sin_signed, first)
        q_ref[:, s * LANES:(s + 1) * LANES] = (t * (HEAD_DIM ** -0.5 * LOG2E)).astype(q_ref.dtype)
    for s in range(dk // LANES):
        t = y[:, dq + s * LANES:dq + (s + 1) * LANES]
        if s * LANES < k_norm:
            t = _rms_slab(t, kn_ref[...], gmat)
        if rope:
            t = _rope_slab(t, cos, sin_signed, first)
        k_ref[:, s * LANES:(s + 1) * LANES] = t.astype(k_ref.dtype)
    v_ref[...] = y[:, dq + dk:].astype(v_ref.dtype)


def _project(x2d, mod, w_bf16, qn2, kn2, cos, sin_signed, *, dq, dk, dv, q_norm, k_norm, rope, latent):
    n = x2d.shape[0]
    per_batch = DEC_SEQ // TM
    if latent:
        mod_map = lambda i: (0, 1 + i // per_batch, 0, 0)
        pos_map = lambda i: (i % per_batch, 0)
        kv_dtype = BF16
    else:
        mod_map = lambda i: (0, 0, 0, 0)
        pos_map = lambda i: (0, 0)
        kv_dtype = F32
    kern = functools.partial(_proj_kernel, dq=dq, dk=dk, q_norm=q_norm, k_norm=k_norm, rope=rope)
    return pl.pallas_call(
        kern,
        grid=(n // TM,),
        in_specs=[
            pl.BlockSpec((TM, D_MODEL), lambda i: (i, 0)),
            pl.BlockSpec((6, None, 1, D_MODEL), mod_map),
            pl.BlockSpec((D_MODEL, dq + dk + dv), lambda i: (0, 0)),
            pl.BlockSpec((1, LANES), lambda i: (0, 0)),
            pl.BlockSpec((1, LANES), lambda i: (0, 0)),
            pl.BlockSpec((TM, LANES), pos_map),
            pl.BlockSpec((TM, LANES), pos_map),
        ],
        out_specs=[
            pl.BlockSpec((TM, dq), lambda i: (i, 0)),
            pl.BlockSpec((TM, dk), lambda i: (i, 0)),
            pl.BlockSpec((TM, dv), lambda i: (i, 0)),
        ],
        out_shape=[
            jax.ShapeDtypeStruct((n, dq), BF16),
            jax.ShapeDtypeStruct((n, dk), kv_dtype),
            jax.ShapeDtypeStruct((n, dv), kv_dtype),
        ],
        compiler_params=_cparams(("parallel",)),
        name="in_proj",
    )(x2d, mod, w_bf16, qn2, kn2, cos, sin_signed)


def _stack_heads(q_ref, slabs, lo, hi):
    qs = [q_ref[:, j * LANES:(j + 1) * LANES] for j in slabs]
    return jnp.concatenate([q * lo for q in qs] + [q * hi for q in qs], axis=0)


def _tile_lanes(x, n):
    return jnp.concatenate([x] * n, axis=1)


def _pv_with_denominator(p, v, lo, hi):
    half = p.shape[0] // 2
    pv_lo = jnp.dot(p[:half], v * lo + hi, preferred_element_type=F32)
    pv_hi = jnp.dot(p[half:], v * hi + lo, preferred_element_type=F32)
    return jnp.concatenate([pv_lo, pv_hi], axis=0)


def _normalize_store(o, o_ref, slabs, tq, lane_lo):
    half = len(slabs) * tq
    for n, j in enumerate(slabs):
        o_lo = o[n * tq:(n + 1) * tq]
        o_hi = o[half + n * tq:half + (n + 1) * tq]
        o_lo = o_lo * (1.0 / pltpu.roll(o_lo, HEAD_DIM, 1))
        o_hi = o_hi * (1.0 / pltpu.roll(o_hi, HEAD_DIM, 1))
        o_ref[:, j * LANES:(j + 1) * LANES] = jnp.where(lane_lo, o_lo, o_hi).astype(o_ref.dtype)


def _sink_rows(sink_ref, heads, tq):
    return jnp.concatenate([jnp.full((tq, LANES), sink_ref[h] * LOG2E, F32) for h in heads], axis=0)


def _softmax_pv(parts, sink, lo, hi):
    rows = parts[0][0].shape[0]
    m = jnp.full((rows, LANES), NEG_BIG, F32) if sink is None else sink
    for s, _ in parts:
        m = jnp.maximum(m, s.max(axis=1, keepdims=True))
    o = None
    for s, v in parts:
        p = jnp.exp2(s - _tile_lanes(m, s.shape[1] // LANES)).astype(BF16)
        pv = _pv_with_denominator(p, v, lo, hi)
        o = pv if o is None else o + pv
    if sink is not None:
        lo32, hi32 = _half_masks(F32)
        e = jnp.exp2(sink - m)
        half = rows // 2
        o = o + jnp.concatenate([e[:half] * hi32, e[half:] * lo32], axis=0)
    return o


def _ctx_attn_kernel(sink_ref, q_ref, k_ref, v_ref, o_ref, *, groups):
    lo, hi = _half_masks(BF16)
    lo32, hi32 = _half_masks(F32)
    lane_lo = lax.broadcasted_iota(jnp.int32, (SEQ, LANES), 1) < HEAD_DIM
    units = [(b, grp) for b in range(q_ref.shape[0] // SEQ) for grp in groups]

    def scores(unit):
        b, (slabs, ks, _) = unit
        kk = k_ref[b * SEQ:(b + 1) * SEQ, ks * LANES:(ks + 1) * LANES].astype(BF16)
        return lax.dot_general(_stack_heads(q_ref.at[b * SEQ:(b + 1) * SEQ], slabs, lo, hi), kk, _NT,
                               preferred_element_type=F32)

    def probs(unit, s):
        sink_heads = unit[1][2]
        if sink_heads is None:
            m = jnp.maximum(jnp.full((s.shape[0], LANES), NEG_BIG, F32), s.max(axis=1, keepdims=True))
            extra = None
        else:
            sink = _sink_rows(sink_ref, sink_heads, SEQ)
            m = jnp.maximum(sink, s.max(axis=1, keepdims=True))
            e = jnp.exp2(sink - m)
            half = s.shape[0] // 2
            extra = jnp.concatenate([e[:half] * hi32, e[half:] * lo32], axis=0)
        return jnp.exp2(s - _tile_lanes(m, s.shape[1] // LANES)).astype(BF16), extra

    def output(unit, p, extra):
        b, (slabs, ks, _) = unit
        vv = v_ref[b * SEQ:(b + 1) * SEQ, ks * LANES:(ks + 1) * LANES].astype(BF16)
        o = _pv_with_denominator(p, vv, lo, hi)
        if extra is not None:
            o = o + extra
        _normalize_store(o, o_ref.at[b * SEQ:(b + 1) * SEQ], slabs, SEQ, lane_lo)

    s_live, p_live = {}, {}
    for t in range(len(units) + 2):
        if t < len(units):
            s_live[t] = scores(units[t])
        if 1 <= t <= len(units):
            p_live[t - 1] = probs(units[t - 1], s_live.pop(t - 1))
        if t >= 2:
            output(units[t - 2], *p_live.pop(t - 2))


def _ctx_attention(q, k, v, sink, *, groups, seqs):
    kw = k.shape[1]
    rows = seqs * SEQ
    return pl.pallas_call(
        functools.partial(_ctx_attn_kernel, groups=groups),
        grid=(BATCH // seqs,),
        in_specs=[
            pl.BlockSpec(memory_space=pltpu.SMEM),
            pl.BlockSpec((rows, D_MODEL), lambda b: (b, 0)),
            pl.BlockSpec((rows, kw), lambda b: (b, 0)),
            pl.BlockSpec((rows, kw), lambda b: (b, 0)),
        ],
        out_specs=pl.BlockSpec((rows, D_MODEL), lambda b: (b, 0)),
        out_shape=jax.ShapeDtypeStruct((P_TOK, D_MODEL), BF16),
        compiler_params=_cparams(("parallel",)),
        name="ctx_attn",
    )(sink, q, k, v)


def _global_attn_kernel(q_ref, k_ref, v_ref, o_ref, s0, s1, p0, p1, a0, a1, m_ref, acc_ref, *, tk):
    tq = q_ref.shape[0]
    slabs = tuple(range(q_ref.shape[1] // LANES))
    nk = k_ref.shape[0] // tk
    lo, hi = _half_masks(BF16)
    lane_lo = lax.broadcasted_iota(jnp.int32, (tq, LANES), 1) < HEAD_DIM
    qstack = _stack_heads(q_ref, slabs, lo, hi)
    m_ref[...] = jnp.full(m_ref.shape, NEG_BIG, F32)
    acc_ref[...] = jnp.zeros(acc_ref.shape, F32)
    s_bufs, p_bufs, a_bufs = (s0, s1), (p0, p1), (a0, a1)

    def stage_a(c):
        s_bufs[c % 2][...] = lax.dot_general(qstack, k_ref[c * tk:(c + 1) * tk, :], _NT, preferred_element_type=F32)

    def stage_b(c):
        s = s_bufs[c % 2][...]
        m_old = m_ref[...]
        m_new = jnp.maximum(m_old, s.max(axis=1, keepdims=True))
        a_bufs[c % 2][...] = jnp.exp2(m_old - m_new)
        p_bufs[c % 2][...] = jnp.exp2(s - _tile_lanes(m_new, tk // LANES)).astype(BF16)
        m_ref[...] = m_new

    def stage_c(c):
        pv = _pv_with_denominator(p_bufs[c % 2][...], v_ref[c * tk:(c + 1) * tk, :], lo, hi)
        acc_ref[...] = a_bufs[c % 2][...] * acc_ref[...] + pv

    for t in range(nk + 2):
        if t < nk:
            stage_a(t)
        if 1 <= t <= nk:
            stage_b(t - 1)
        if t >= 2:
            stage_c(t - 2)
    _normalize_store(acc_ref[...], o_ref, slabs, tq, lane_lo)


def _global_attention(q, kcat, vcat):
    tq, tk = 256, 512
    t = kcat.shape[1]
    per_batch = DEC_SEQ // tq
    rows = N_HEADS_A * tq
    kv_spec = pl.BlockSpec((None, t, LANES), lambda b, i: (b, 0, 0))
    return pl.pallas_call(
        functools.partial(_global_attn_kernel, tk=tk),
        grid=(DEC_BATCH, per_batch),
        in_specs=[pl.BlockSpec((tq, 4 * LANES), lambda b, i: (b * per_batch + i, 0)), kv_spec, kv_spec],
        out_specs=pl.BlockSpec((tq, 4 * LANES), lambda b, i: (b * per_batch + i, 0)),
        out_shape=jax.ShapeDtypeStruct((S_TOK, 4 * LANES), BF16),
        scratch_shapes=[pltpu.VMEM((rows, tk), F32), pltpu.VMEM((rows, tk), F32),
                        pltpu.VMEM((rows, tk), BF16), pltpu.VMEM((rows, tk), BF16),
                        pltpu.VMEM((rows, LANES), F32), pltpu.VMEM((rows, LANES), F32),
                        pltpu.VMEM((rows, LANES), F32), pltpu.VMEM((rows, LANES), F32)],
        compiler_params=_cparams(("parallel", "parallel")),
        name="global_attn",
    )(q, kcat, vcat)


def _window_attn_kernel(sink_ref, q_ref, k_ref, v_ref, o_ref):
    tq = WINDOW
    n_blocks = q_ref.shape[0] // tq
    slabs = tuple(range(q_ref.shape[1] // LANES))
    n_heads = 2 * len(slabs)
    span = tq + 2 * WINDOW
    lo, hi = _half_masks(BF16)
    lo32, hi32 = _half_masks(F32)
    lane_lo = lax.broadcasted_iota(jnp.int32, (tq, LANES), 1) < HEAD_DIM
    row = lax.broadcasted_iota(jnp.int32, (tq, span), 0)
    col = lax.broadcasted_iota(jnp.int32, (tq, span), 1)
    band = (col >= row) & (col <= row + 2 * WINDOW)
    sink = _sink_rows(sink_ref, range(n_heads), tq)

    def local_offset(u):
        i = pl.program_id(1) * n_blocks + u
        return i, pl.multiple_of(PAST_LEN - WINDOW + i * tq, LANES)

    def scores(u):
        i, off = local_offset(u)
        kpos = i * tq - WINDOW + col
        valid = band & (kpos >= 0) & (kpos < DEC_SEQ)
        qstack = _stack_heads(q_ref.at[u * tq:(u + 1) * tq], slabs, lo, hi)
        s_loc = lax.dot_general(qstack, k_ref[pl.ds(off, span), :], _NT, preferred_element_type=F32)
        s_loc = jnp.where(valid[None], s_loc.reshape(n_heads, tq, span), NEG_BIG).reshape(n_heads * tq, span)
        return [s_loc, lax.dot_general(qstack, k_ref[0:PAST_LEN, :], _NT, preferred_element_type=F32)]

    def probs(s_parts):
        m = sink
        for s in s_parts:
            m = jnp.maximum(m, s.max(axis=1, keepdims=True))
        e = jnp.exp2(sink - m)
        half = e.shape[0] // 2
        extra = jnp.concatenate([e[:half] * hi32, e[half:] * lo32], axis=0)
        return [jnp.exp2(s - _tile_lanes(m, s.shape[1] // LANES)).astype(BF16) for s in s_parts], extra

    def output(u, p_parts, extra):
        _, off = local_offset(u)
        o = (_pv_with_denominator(p_parts[0], v_ref[pl.ds(off, span), :], lo, hi)
             + _pv_with_denominator(p_parts[1], v_ref[0:PAST_LEN, :], lo, hi) + extra)
        _normalize_store(o, o_ref.at[u * tq:(u + 1) * tq], slabs, tq, lane_lo)

    s_live, p_live = {}, {}
    for t in range(n_blocks + 2):
        if t < n_blocks:
            s_live[t] = scores(t)
        if 1 <= t <= n_blocks:
            p_live[t - 1] = probs(s_live.pop(t - 1))
        if t >= 2:
            output(t - 2, *p_live.pop(t - 2))


def _window_attention(q, kcat, vcat, sink):
    tq = WINDOW_BLOCKS_PER_STEP * WINDOW
    t = kcat.shape[1]
    per_batch = DEC_SEQ // tq
    return pl.pallas_call(
        _window_attn_kernel,
        grid=(DEC_BATCH, per_batch),
        in_specs=[
            pl.BlockSpec(memory_space=pltpu.SMEM),
            pl.BlockSpec((tq, 4 * LANES), lambda b, i: (b * per_batch + i, 1)),
            pl.BlockSpec((None, t, LANES), lambda b, i: (b, 0, 0)),
            pl.BlockSpec((None, t, LANES), lambda b, i: (b, 0, 0)),
        ],
        out_specs=pl.BlockSpec((tq, 4 * LANES), lambda b, i: (b * per_batch + i, 0)),
        out_shape=jax.ShapeDtypeStruct((S_TOK, 4 * LANES), BF16),
        compiler_params=_cparams(("parallel", "parallel")),
        name="window_attn",
    )(sink, q, kcat, vcat)


def _na_start_row(g):
    return jnp.clip(NA_QROWS * g - NA_ROWS // 2, 0, DEC_SEQ // GRID_W - NA_KROWS)


def _na_step_bias(t_ref, half, g, lane_lo64):
    rows = DEC_SEQ // GRID_W
    start = _na_start_row(g)
    d0 = start - NA_QROWS * g + NA_ROWS - 1
    row_blocks = []
    for qr in range(NA_QROWS):
        rs = jnp.clip(NA_QROWS * g + qr - NA_ROWS // 2, 0, rows - NA_ROWS)
        blocks = []
        for m in range(NA_KROWS // 2):
            tiles = []
            for kr in (2 * m, 2 * m + 1):
                krow = start + kr
                inside = (krow >= rs) & (krow < rs + NA_ROWS)
                dr = jnp.clip(kr - qr + d0, 0, 2 * NA_ROWS - 2)
                tiles.append(t_ref[half, dr] + jnp.where(inside, 0.0, NEG_BIG))
            blocks.append(jnp.where(lane_lo64, tiles[0], tiles[1]))
        row_blocks.append(jnp.concatenate(blocks, axis=1))
    return jnp.concatenate(row_blocks, axis=0)


def _na_attn_kernel(q_ref, k_ref, v_ref, kc_ref, vc_ref, t_ref, o_ref):
    g = pl.program_id(2)
    tq = q_ref.shape[0]
    lo, hi = _half_masks(BF16)
    lane_lo = lax.broadcasted_iota(jnp.int32, (tq, LANES), 1) < HEAD_DIM
    lane_lo64 = lax.broadcasted_iota(jnp.int32, (GRID_W, LANES), 1) < HEAD_DIM
    off = pl.multiple_of(_na_start_row(g) * GRID_W, GRID_W)
    n_slabs = q_ref.shape[1] // LANES
    cols = [slice(n * LANES, (n + 1) * LANES) for n in range(n_slabs)]

    def scores(n):
        qstack = _stack_heads(q_ref, (n,), lo, hi)
        bias = jnp.concatenate([_na_step_bias(t_ref, 2 * n + half, g, lane_lo64) for half in range(2)], axis=0)
        k_loc = k_ref[pl.ds(off, NA_KROWS * GRID_W), cols[n]]
        s_loc = lax.dot_general(qstack, k_loc, _NT, preferred_element_type=F32) + bias
        return [s_loc, lax.dot_general(qstack, kc_ref[:, cols[n]], _NT, preferred_element_type=F32)]

    def probs(s_parts):
        m = jnp.full((2 * tq, LANES), NEG_BIG, F32)
        for s in s_parts:
            m = jnp.maximum(m, s.max(axis=1, keepdims=True))
        return [jnp.exp2(s - _tile_lanes(m, s.shape[1] // LANES)).astype(BF16) for s in s_parts]

    def output(n, p_parts):
        v_loc = v_ref[pl.ds(off, NA_KROWS * GRID_W), cols[n]]
        o = _pv_with_denominator(p_parts[0], v_loc, lo, hi) + _pv_with_denominator(p_parts[1], vc_ref[:, cols[n]], lo, hi)
        _normalize_store(o, o_ref, (n,), tq, lane_lo)

    s_live, p_live = {}, {}
    for t in range(n_slabs + 2):
        if t < n_slabs:
            s_live[t] = scores(t)
        if 1 <= t <= n_slabs:
            p_live[t - 1] = probs(s_live.pop(t - 1))
        if t >= 2:
            output(t - 2, p_live.pop(t - 2))


def _na_bias_table(rpb):
    cols = np.arange(GRID_W)
    cs = np.clip(cols - NA_COLS // 2, 0, GRID_W - NA_COLS)
    dc = cols[None, :] - cols[:, None] + NA_COLS - 1
    inside = (cols[None, :] >= cs[:, None]) & (cols[None, :] < cs[:, None] + NA_COLS)
    onehot = (np.arange(2 * NA_COLS - 1)[:, None, None] == dc[None]) & inside[None]
    t = jnp.einsum("hdc,cqk->hdqk", rpb.astype(F32), jnp.asarray(onehot, F32), precision=lax.Precision.HIGHEST)
    t = jnp.where(inside[None, None], t * LOG2E, NEG_BIG)
    return jnp.concatenate([t, t], axis=-1)


def _na_attention(q, k, v, kc, vc, bias):
    tq = NA_QROWS * GRID_W
    n_groups = DEC_SEQ // tq
    w = NA_SLABS_PER_STEP * LANES
    k3 = k.reshape(DEC_BATCH, DEC_SEQ, D_MODEL)
    v3 = v.reshape(DEC_BATCH, DEC_SEQ, D_MODEL)
    tok = lambda s, b, g: (b * n_groups + g, s)
    per_batch = lambda s, b, g: (b, 0, s)
    once = pl.Buffered(1)
    return pl.pallas_call(
        _na_attn_kernel,
        grid=(D_MODEL // w, DEC_BATCH, n_groups),
        in_specs=[
            pl.BlockSpec((tq, w), tok),
            pl.BlockSpec((None, DEC_SEQ, w), per_batch, pipeline_mode=once),
            pl.BlockSpec((None, DEC_SEQ, w), per_batch, pipeline_mode=once),
            pl.BlockSpec((None, PAST_LEN, w), per_batch, pipeline_mode=once),
            pl.BlockSpec((None, PAST_LEN, w), per_batch, pipeline_mode=once),
            pl.BlockSpec((2 * NA_SLABS_PER_STEP, 2 * NA_ROWS - 1, GRID_W, LANES), lambda s, b, g: (s, 0, 0, 0),
                         pipeline_mode=once),
        ],
        out_specs=pl.BlockSpec((tq, w), tok),
        out_shape=jax.ShapeDtypeStruct((S_TOK, D_MODEL), BF16),
        compiler_params=_cparams(("parallel", "parallel", "parallel")),
        name="na_attn",
    )(q, k3, v3, kc, vc, bias)


def _layer_norm(z, g, b):
    mu = jnp.mean(z, axis=-1, keepdims=True)
    zc = z - mu
    var = jnp.mean(zc * zc, axis=-1, keepdims=True)
    return zc * lax.rsqrt(var + LN_EPS) * g + b


def _post_kernel(*refs, n_parts):
    o_refs = refs[:n_parts]
    (y_ref, mod_ref, w_ref, lng_ref, lnb_ref, wr_hi_ref, wr_lo_ref, br_ref,
     y_out_ref, h_out_ref, gates_ref) = refs[n_parts:]
    pw = D_MODEL // n_parts
    mix = None
    for p, o_ref in enumerate(o_refs):
        part = jnp.dot(o_ref[...], w_ref[p * pw:(p + 1) * pw, :], preferred_element_type=F32)
        mix = part if mix is None else mix + part
    y = _layer_norm(ALPHA * y_ref[...] + mod_ref[2] * mix, lng_ref[...], lnb_ref[...])
    y_out_ref[...] = y
    h = y * (1.0 + mod_ref[4]) + mod_ref[3]
    h_hi = h.astype(BF16)
    h_out_ref[...] = h_hi
    h_lo = (h - h_hi.astype(F32)).astype(BF16)
    logits = lax.dot_general(wr_hi_ref[...], h_hi, _NT, preferred_element_type=F32)
    logits += lax.dot_general(wr_hi_ref[...], h_lo, _NT, preferred_element_type=F32)
    logits += lax.dot_general(wr_lo_ref[...], h_hi, _NT, preferred_element_type=F32)
    scores = 1.0 / (1.0 + jnp.exp(-logits))
    work = scores + br_ref[...]
    expert = lax.broadcasted_iota(jnp.int32, work.shape, 0).astype(F32)
    chosen = jnp.zeros(work.shape, F32)
    for _ in range(TOP_K):
        mx = work.max(axis=0, keepdims=True)
        first = jnp.where(work == mx, expert, float(N_EXPERTS)).min(axis=0, keepdims=True)
        pick = expert == first
        chosen = jnp.where(pick, scores, chosen)
        work = jnp.where(pick, NEG_BIG, work)
    gates_t = chosen / chosen.sum(axis=0, keepdims=True) * ROUTED_SCALE
    padded = jnp.concatenate([gates_t, jnp.zeros((LANES - N_EXPERTS, gates_t.shape[1]), F32)], axis=0)
    gates_ref[...] = padded.T[:, :N_EXPERTS]


def _post_mixer(o_parts, y2d, mod, w_out_bf16, ln_g, ln_b, wr_hi, wr_lo, b_router, *, latent):
    n = y2d.shape[0]
    n_parts = len(o_parts)
    pw = D_MODEL // n_parts
    per_batch = DEC_SEQ // TM
    mod_map = (lambda i: (0, 1 + i // per_batch, 0, 0)) if latent else (lambda i: (0, 0, 0, 0))
    o_specs = [pl.BlockSpec((TM, pw), functools.partial(lambda i, c: (i, c), c=col)) for _, col in o_parts]
    full = lambda i: (0, 0)
    return pl.pallas_call(
        functools.partial(_post_kernel, n_parts=n_parts),
        grid=(n // TM,),
        in_specs=o_specs + [
            pl.BlockSpec((TM, D_MODEL), lambda i: (i, 0)),
            pl.BlockSpec((6, None, 1, D_MODEL), mod_map),
            pl.BlockSpec((D_MODEL, D_MODEL), full),
            pl.BlockSpec((1, D_MODEL), full),
            pl.BlockSpec((1, D_MODEL), full),
            pl.BlockSpec((N_EXPERTS, D_MODEL), full),
            pl.BlockSpec((N_EXPERTS, D_MODEL), full),
            pl.BlockSpec((N_EXPERTS, 1), full),
        ],
        out_specs=[
            pl.BlockSpec((TM, D_MODEL), lambda i: (i, 0)),
            pl.BlockSpec((TM, D_MODEL), lambda i: (i, 0)),
            pl.BlockSpec((TM, N_EXPERTS), lambda i: (i, 0)),
        ],
        out_shape=[
            jax.ShapeDtypeStruct((n, D_MODEL), F32),
            jax.ShapeDtypeStruct((n, D_MODEL), BF16),
            jax.ShapeDtypeStruct((n, N_EXPERTS), F32),
        ],
        compiler_params=_cparams(("parallel",)),
        name="post_mixer",
    )(*[a for a, _ in o_parts], y2d, mod, w_out_bf16, ln_g, ln_b, wr_hi, wr_lo, b_router)


def _silu(x):
    return x / (1.0 + jnp.exp(-x))


def _moe_kernel(x_ref, gates_ref, y_ref, mod_ref, wg_ref, wu_ref, wd_ref, sg_ref, su_ref, sd_ref,
                lng_ref, lnb_ref, o_ref, acc_ref):
    e = pl.program_id(1)
    x = x_ref[...]

    @pl.when(e == 0)
    def _():
        a = _silu(jnp.dot(x, sg_ref[...].astype(BF16), preferred_element_type=F32))
        a = a * jnp.dot(x, su_ref[...].astype(BF16), preferred_element_type=F32)
        acc_ref[...] = jnp.dot(a.astype(BF16), sd_ref[...].astype(BF16), preferred_element_type=F32)

    gates = gates_ref[...]
    lane = lax.broadcasted_iota(jnp.int32, gates.shape, 1)
    acts = []
    for j in range(EXPERTS_PER_STEP):
        w1 = jnp.concatenate([wg_ref[j].astype(BF16), wu_ref[j].astype(BF16)], axis=1)
        hcat = jnp.dot(x, w1, preferred_element_type=F32)
        ge = jnp.sum(jnp.where(lane == e * EXPERTS_PER_STEP + j, gates, 0.0), axis=1, keepdims=True)
        acts.append((_silu(hcat[:, :D_EXPERT]) * hcat[:, D_EXPERT:] * ge).astype(BF16))
    w2 = jnp.concatenate([wd_ref[j].astype(BF16) for j in range(EXPERTS_PER_STEP)], axis=0)
    acc_ref[...] += jnp.dot(jnp.concatenate(acts, axis=1), w2, preferred_element_type=F32)

    @pl.when(e == pl.num_programs(1) - 1)
    def _():
        o_ref[...] = _layer_norm(ALPHA * y_ref[...] + mod_ref[5] * acc_ref[...], lng_ref[...], lnb_ref[...])


def _moe(x_bf16, gates, y2d, mod, w_gate, w_up, w_down, ws_gate, ws_up, ws_down, ln_g, ln_b, *, layer, latent):
    n = x_bf16.shape[0]
    per_batch = DEC_SEQ // TM_MOE
    mod_map = (lambda i, e: (0, 1 + i // per_batch, 0, 0)) if latent else (lambda i, e: (0, 0, 0, 0))
    ep = EXPERTS_PER_STEP
    tok = lambda i, e: (i, 0)
    full = lambda i, e: (0, 0)
    routed = lambda i, e: (layer, e, 0, 0)
    shared = lambda i, e: (layer, 0, 0)
    return pl.pallas_call(
        _moe_kernel,
        grid=(n // TM_MOE, N_EXPERTS // ep),
        in_specs=[
            pl.BlockSpec((TM_MOE, D_MODEL), tok),
            pl.BlockSpec((TM_MOE, N_EXPERTS), tok),
            pl.BlockSpec((TM_MOE, D_MODEL), tok, pipeline_mode=pl.Buffered(1)),
            pl.BlockSpec((6, None, 1, D_MODEL), mod_map),
            pl.BlockSpec((None, ep, D_MODEL, D_EXPERT), routed),
            pl.BlockSpec((None, ep, D_MODEL, D_EXPERT), routed),
            pl.BlockSpec((None, ep, D_EXPERT, D_MODEL), routed),
            pl.BlockSpec((None, D_MODEL, D_EXPERT), shared),
            pl.BlockSpec((None, D_MODEL, D_EXPERT), shared),
            pl.BlockSpec((None, D_EXPERT, D_MODEL), shared),
            pl.BlockSpec((1, D_MODEL), full),
            pl.BlockSpec((1, D_MODEL), full),
        ],
        out_specs=pl.BlockSpec((TM_MOE, D_MODEL), tok),
        out_shape=jax.ShapeDtypeStruct((n, D_MODEL), F32),
        scratch_shapes=[pltpu.VMEM((TM_MOE, D_MODEL), F32)],
        compiler_params=_cparams(("parallel", "arbitrary")),
        name="moe",
    )(x_bf16, gates, y2d, mod, w_gate, w_up, w_down, ws_gate, ws_up, ws_down, ln_g, ln_b)


def _slab_perm():
    idx = []
    for j in range(4):
        for half in range(2):
            head = j + 4 * half
            idx.extend(range(head * HEAD_DIM, (head + 1) * HEAD_DIM))
    return np.asarray(idx, np.int32)


def _rope_tables():
    t = np.arange(DEC_SEQ)
    quarter = HEAD_DIM // 4
    inv = ROPE_THETA ** (-np.arange(quarter, dtype=np.float64) / quarter)
    ar = (t // GRID_W)[:, None] * inv
    ac = (t % GRID_W)[:, None] * inv
    ang = np.concatenate([ar, ar, ac, ac] * 2, axis=-1)
    sign = np.where((np.arange(LANES) % 32) < 16, -1.0, 1.0)
    return jnp.asarray(np.cos(ang), F32), jnp.asarray(np.sin(ang) * sign, F32)


def kernel(x_prompt, x_sample, cache_ka, cache_va, cache_kb, cache_vb, cache_kc, cache_vc, c, c_ctx, w_ada, b_ada, ln_g, ln_b, w_in_even, w_out_even, qnorm_a, knorm_a, sink_b, w_in_odd, w_out_odd, rpb_c, w_router, b_router, w_gate, w_up, w_down, ws_gate, ws_up, ws_down):
    D = D_MODEL
    y_p = x_prompt.reshape(P_TOK, D)
    y_s = x_sample.reshape(S_TOK, D)

    cond8 = jnp.concatenate([c_ctx[None, :], c, jnp.zeros((8 - 1 - DEC_BATCH, D), F32)], axis=0)
    ada = _adaln(cond8, w_ada, b_ada)
    mods = ada.reshape(DEPTH, 8, 6, 1, D).transpose(0, 2, 1, 3, 4)

    cos, sin_signed = _rope_tables()
    perm = _slab_perm()
    in_perm = np.concatenate([perm, 512 + perm, np.arange(1024, EVEN_IN)])
    out_perm = np.concatenate([perm, 512 + perm])
    ones_row = jnp.ones((1, LANES), F32)
    no_sink = jnp.zeros((N_HEADS_B,), F32)

    new = {}
    for l in range(DEPTH):
        mod = mods[l]
        if l % 2 == 0:
            e = l // 2
            w_in = w_in_even[e][:, in_perm].astype(BF16)
            w_out = w_out_even[e][out_perm, :].astype(BF16)
            qn2 = jnp.tile(qnorm_a[e], 2)[None, :]
            kn2 = jnp.tile(knorm_a[e], 2)[None, :]
            proj = functools.partial(_project, w_bf16=w_in, qn2=qn2, kn2=kn2, cos=cos, sin_signed=sin_signed,
                                     dq=1024, dk=256, dv=256, q_norm=512, k_norm=128)
            q_p, k_p, v_p = proj(y_p, mod, rope=False, latent=False)
            q_s, k_s, v_s = proj(y_s, mod, rope=True, latent=True)
            new["ka"], new["kb"] = k_p[:, :LANES], k_p[:, LANES:]
            new["va"], new["vb"] = v_p[:, :LANES], v_p[:, LANES:]
            sink = sink_b[e].astype(F32)
            groups = (tuple(((j,), 0, None) for j in range(4))
                      + tuple(((4 + j,), 1, (j, j + 4)) for j in range(4)))
            o_p = _ctx_attention(q_p, k_p, v_p, sink, groups=groups, seqs=4)

            k_s3 = k_s.reshape(DEC_BATCH, DEC_SEQ, 2 * LANES)
            v_s3 = v_s.reshape(DEC_BATCH, DEC_SEQ, 2 * LANES)
            cka = cache_ka[:, e].reshape(DEC_BATCH, PAST_LEN, LANES).astype(BF16)
            cva = cache_va[:, e].reshape(DEC_BATCH, PAST_LEN, LANES).astype(BF16)
            ckb = cache_kb[:, e].reshape(DEC_BATCH, PAST_LEN, LANES).astype(BF16)
            cvb = cache_vb[:, e].reshape(DEC_BATCH, PAST_LEN, LANES).astype(BF16)
            pad = jnp.zeros((DEC_BATCH, WINDOW, LANES), BF16)
            o_a = _global_attention(q_s, jnp.concatenate([cka, k_s3[:, :, :LANES]], axis=1),
                                    jnp.concatenate([cva, v_s3[:, :, :LANES]], axis=1))
            o_b = _window_attention(q_s, jnp.concatenate([ckb, k_s3[:, :, LANES:], pad], axis=1),
                                    jnp.concatenate([cvb, v_s3[:, :, LANES:], pad], axis=1), sink)
            parts_p = [(o_p, 0), (o_p, 1)]
            parts_s = [(o_a, 0), (o_b, 0)]
        else:
            o = l // 2
            w_in = w_in_odd[o].astype(BF16)
            w_out = w_out_odd[o].astype(BF16)
            proj = functools.partial(_project, w_bf16=w_in, qn2=ones_row, kn2=ones_row, cos=cos,
                                     sin_signed=sin_signed, dq=1024, dk=1024, dv=1024, q_norm=0, k_norm=0,
                                     rope=False)
            q_p, k_p, v_p = proj(y_p, mod, latent=False)
            q_s, k_s, v_s = proj(y_s, mod, latent=True)
            new["kc"], new["vc"] = k_p, v_p
            groups = tuple(((j,), j, None) for j in range(D // LANES))
            o_p = _ctx_attention(q_p, k_p, v_p, no_sink, groups=groups, seqs=4)
            kc = cache_kc[:, o].reshape(DEC_BATCH, PAST_LEN, D).astype(BF16)
            vc = cache_vc[:, o].reshape(DEC_BATCH, PAST_LEN, D).astype(BF16)
            o_s = _na_attention(q_s, k_s, v_s, kc, vc, _na_bias_table(rpb_c[o]))
            parts_p = [(o_p, 0)]
            parts_s = [(o_s, 0)]

        wr = w_router[l].T
        wr_hi = wr.astype(BF16)
        wr_lo = (wr - wr_hi.astype(F32)).astype(BF16)
        post = functools.partial(_post_mixer, mod=mod, w_out_bf16=w_out, ln_g=ln_g[l, 0][None, :],
                                 ln_b=ln_b[l, 0][None, :], wr_hi=wr_hi, wr_lo=wr_lo, b_router=b_router[l][:, None])
        y_p, h_p, g_p = post(parts_p, y_p, latent=False)
        y_s, h_s, g_s = post(parts_s, y_s, latent=True)
        moe = functools.partial(_moe, mod=mod, w_gate=w_gate, w_up=w_up, w_down=w_down,
                                ws_gate=ws_gate, ws_up=ws_up, ws_down=ws_down,
                                ln_g=ln_g[l, 1][None, :], ln_b=ln_b[l, 1][None, :], layer=l)
        y_p = moe(h_p, g_p, y_p, latent=False)
        y_s = moe(h_s, g_s, y_s, latent=True)

    kv_a = (BATCH, 1, SEQ, N_KV_A, HEAD_DIM)
    kv_c = (BATCH, 1, SEQ, N_HEADS_C, HEAD_DIM)
    return (y_p.reshape(BATCH, SEQ, D), y_s.reshape(DEC_BATCH, DEC_SEQ, D),
            new["ka"].reshape(kv_a), new["va"].reshape(kv_a), new["kb"].reshape(kv_a), new["vb"].reshape(kv_a),
            new["kc"].reshape(kv_c), new["vc"].reshape(kv_c))
```

```python
import functools
import math

import numpy as np
import jax
import jax.numpy as jnp
from jax import lax
from jax.experimental import pallas as pl
from jax.experimental.pallas import tpu as pltpu

F32 = jnp.float32
BF16 = jnp.bfloat16

D_MODEL = 1024
BATCH = 16
SEQ = 256
DEPTH = 2
DEC_BATCH = 2
DEC_SEQ = 4096
PAST_LEN = 512
GRID_W = 64
HEAD_DIM = 64
N_HEADS_A = 8
N_KV_A = 2
N_HEADS_B = 8
N_KV_B = 2
N_HEADS_C = 16
EVEN_IN = 1536
ODD_IN = 3072
WINDOW = 128
NA_ROWS = 8
NA_COLS = 16
ROPE_THETA = 10000.0
N_EXPERTS = 64
TOP_K = 8
D_EXPERT = 128
ROUTED_SCALE = 2.5
ALPHA = (2 * DEPTH) ** 0.25
LN_EPS = 1e-6
RMS_EPS = 1e-6
NEG_BIG = -1e30
LOG2E = math.log2(math.e)

LANES = 128
P_TOK = BATCH * SEQ
S_TOK = DEC_BATCH * DEC_SEQ
TM = 512
TM_MOE = 1024
EXPERTS_PER_STEP = 8
NA_QROWS = 4
NA_KROWS = 12
NA_SLABS_PER_STEP = 8
WINDOW_BLOCKS_PER_STEP = 4
VMEM_LIMIT = 56 * 1024 * 1024

_NT = (((1,), (1,)), ((), ()))


def _cparams(sem):
    return pltpu.CompilerParams(dimension_semantics=sem, vmem_limit_bytes=VMEM_LIMIT)


def _half_masks(dtype):
    lane = lax.broadcasted_iota(jnp.int32, (1, LANES), 1)
    lo = jnp.where(lane < HEAD_DIM, 1.0, 0.0).astype(dtype)
    hi = jnp.where(lane < HEAD_DIM, 0.0, 1.0).astype(dtype)
    return lo, hi


def _ada_kernel(c_ref, w_ref, b_ref, o_ref):
    c = c_ref[...]
    a = c / (1.0 + jnp.exp(-c))
    w = w_ref[0]
    a_hi = a.astype(BF16)
    a_lo = (a - a_hi.astype(F32)).astype(BF16)
    w_hi = w.astype(BF16)
    w_lo = (w - w_hi.astype(F32)).astype(BF16)
    acc = jnp.dot(a_hi, w_hi, preferred_element_type=F32)
    acc += jnp.dot(a_lo, w_hi, preferred_element_type=F32)
    acc += jnp.dot(a_hi, w_lo, preferred_element_type=F32)
    o_ref[0] = acc + b_ref[0]


def _adaln(cond8, w_ada, b_ada):
    tn = 1536
    return pl.pallas_call(
        _ada_kernel,
        grid=(DEPTH, 6 * D_MODEL // tn),
        in_specs=[
            pl.BlockSpec((8, D_MODEL), lambda l, j: (0, 0)),
            pl.BlockSpec((1, D_MODEL, tn), lambda l, j: (l, 0, j)),
            pl.BlockSpec((1, 1, tn), lambda l, j: (l, 0, j)),
        ],
        out_specs=pl.BlockSpec((1, 8, tn), lambda l, j: (l, 0, j)),
        out_shape=jax.ShapeDtypeStruct((DEPTH, 8, 6 * D_MODEL), F32),
        compiler_params=_cparams(("parallel", "parallel")),
        name="adaln",
    )(cond8, w_ada, b_ada.reshape(DEPTH, 1, 6 * D_MODEL))


def _group_sum_matrix():
    r = (lax.broadcasted_iota(jnp.int32, (2 * LANES, LANES), 0) % LANES) // HEAD_DIM
    c = lax.broadcasted_iota(jnp.int32, (2 * LANES, LANES), 1) // HEAD_DIM
    return jnp.where(r == c, 1.0, 0.0).astype(BF16)


def _rms_slab(t, g, gmat):
    sq = t * t
    hi = sq.astype(BF16)
    lo = (sq - hi.astype(F32)).astype(BF16)
    ss = jnp.dot(jnp.concatenate([hi, lo], axis=1), gmat, preferred_element_type=F32)
    return t * lax.rsqrt(ss * (1.0 / HEAD_DIM) + RMS_EPS) * g


def _rope_slab(t, cos, sin_signed, first):
    r = jnp.where(first, pltpu.roll(t, LANES - 16, 1), pltpu.roll(t, 16, 1))
    return t * cos + r * sin_signed


def _proj_kernel(x_ref, mod_ref, w_ref, qn_ref, kn_ref, cos_ref, sin_ref, q_ref, k_ref, v_ref,
                 *, dq, dk, q_norm, k_norm, rope):
    x = x_ref[...]
    h = (x * (1.0 + mod_ref[1]) + mod_ref[0]).astype(BF16)
    y = jnp.dot(h, w_ref[...], preferred_element_type=F32)
    tm = x.shape[0]
    gmat = _group_sum_matrix() if (q_norm or k_norm) else None
    if rope:
        cos = cos_ref[...]
        sin_signed = sin_ref[...]
        first = (lax.broadcasted_iota(jnp.int32, (tm, LANES), 1) % 32) < 16
    for s in range(dq // LANES):
        t = y[:, s * LANES:(s + 1) * LANES]
        if s * LANES < q_norm:
            t = _rms_slab(t, qn_ref[...], gmat)
        if rope:
            t = _rope_slab(t, cos, sin_signed, first)
        q_ref[:, s * LANES:(s + 1) * LANES] = (t * (HEAD_DIM ** -0.5 * LOG2E)).astype(q_ref.dtype)
    for s in range(dk // LANES):
        t = y[:, dq + s * LANES:dq + (s + 1) * LANES]
        if s * LANES < k_norm:
            t = _rms_slab(t, kn_ref[...], gmat)
        if rope:
            t = _rope_slab(t, cos, sin_signed, first)
        k_ref[:, s * LANES:(s + 1) * LANES] = t.astype(k_ref.dtype)
    v_ref[...] = y[:, dq + dk:].astype(v_ref.dtype)


def _project(x2d, mod, w_bf16, qn2, kn2, cos, sin_signed, *, dq, dk, dv, q_norm, k_norm, rope, latent):
    n = x2d.shape[0]
    per_batch = DEC_SEQ // TM
    if latent:
        mod_map = lambda i: (0, 1 + i // per_batch, 0, 0)
        pos_map = lambda i: (i % per_batch, 0)
        kv_dtype = BF16
    else:
        mod_map = lambda i: (0, 0, 0, 0)
        pos_map = lambda i: (0, 0)
        kv_dtype = F32
    kern = functools.partial(_proj_kernel, dq=dq, dk=dk, q_norm=q_norm, k_norm=k_norm, rope=rope)
    return pl.pallas_call(
        kern,
        grid=(n // TM,),
        in_specs=[
            pl.BlockSpec((TM, D_MODEL), lambda i: (i, 0)),
            pl.BlockSpec((6, None, 1, D_MODEL), mod_map),
            pl.BlockSpec((D_MODEL, dq + dk + dv), lambda i: (0, 0)),
            pl.BlockSpec((1, LANES), lambda i: (0, 0)),
            pl.BlockSpec((1, LANES), lambda i: (0, 0)),
            pl.BlockSpec((TM, LANES), pos_map),
            pl.BlockSpec((TM, LANES), pos_map),
        ],
        out_specs=[
            pl.BlockSpec((TM, dq), lambda i: (i, 0)),
            pl.BlockSpec((TM, dk), lambda i: (i, 0)),
            pl.BlockSpec((TM, dv), lambda i: (i, 0)),
        ],
        out_shape=[
            jax.ShapeDtypeStruct((n, dq), BF16),
            jax.ShapeDtypeStruct((n, dk), kv_dtype),
            jax.ShapeDtypeStruct((n, dv), kv_dtype),
        ],
        compiler_params=_cparams(("parallel",)),
        name="in_proj",
    )(x2d, mod, w_bf16, qn2, kn2, cos, sin_signed)


def _stack_heads(q_ref, slabs, lo, hi):
    qs = [q_ref[:, j * LANES:(j + 1) * LANES] for j in slabs]
    return jnp.concatenate([q * lo for q in qs] + [q * hi for q in qs], axis=0)


def _tile_lanes(x, n):
    return jnp.concatenate([x] * n, axis=1)


def _pv_with_denominator(p, v, lo, hi):
    half = p.shape[0] // 2
    pv_lo = jnp.dot(p[:half], v * lo + hi, preferred_element_type=F32)
    pv_hi = jnp.dot(p[half:], v * hi + lo, preferred_element_type=F32)
    return jnp.concatenate([pv_lo, pv_hi], axis=0)


def _normalize_store(o, o_ref, slabs, tq, lane_lo):
    half = len(slabs) * tq
    for n, j in enumerate(slabs):
        o_lo = o[n * tq:(n + 1) * tq]
        o_hi = o[half + n * tq:half + (n + 1) * tq]
        o_lo = o_lo * (1.0 / pltpu.roll(o_lo, HEAD_DIM, 1))
        o_hi = o_hi * (1.0 / pltpu.roll(o_hi, HEAD_DIM, 1))
        o_ref[:, j * LANES:(j + 1) * LANES] = jnp.where(lane_lo, o_lo, o_hi).astype(o_ref.dtype)


def _sink_rows(sink_ref, heads, tq):
    return jnp.concatenate([jnp.full((tq, LANES), sink_ref[h] * LOG2E, F32) for h in heads], axis=0)


def _software_pipeline(units, stages):
    live = [dict() for _ in stages]
    for t in range(len(units) + len(stages) - 1):
        for k, stage in enumerate(stages):
            u = t - k
            if 0 <= u < len(units):
                live[k][u] = stage(units[u], live[k - 1].pop(u) if k else None)


def _softmax_pv(parts, sink, lo, hi):
    rows = parts[0][0].shape[0]
    m = jnp.full((rows, LANES), NEG_BIG, F32) if sink is None else sink
    for s, _ in parts:
        m = jnp.maximum(m, s.max(axis=1, keepdims=True))
    o = None
    for s, v in parts:
        p = jnp.exp2(s - _tile_lanes(m, s.shape[1] // LANES)).astype(BF16)
        pv = _pv_with_denominator(p, v, lo, hi)
        o = pv if o is None else o + pv
    if sink is not None:
        lo32, hi32 = _half_masks(F32)
        e = jnp.exp2(sink - m)
        half = rows // 2
        o = o + jnp.concatenate([e[:half] * hi32, e[half:] * lo32], axis=0)
    return o


def _ctx_attn_kernel(sink_ref, q_ref, k_ref, v_ref, o_ref, *, groups):
    lo, hi = _half_masks(BF16)
    lo32, hi32 = _half_masks(F32)
    lane_lo = lax.broadcasted_iota(jnp.int32, (SEQ, LANES), 1) < HEAD_DIM
    units = [(b, grp) for b in range(q_ref.shape[0] // SEQ) for grp in groups]

    def scores(unit):
        b, (slabs, ks, _) = unit
        kk = k_ref[b * SEQ:(b + 1) * SEQ, ks * LANES:(ks + 1) * LANES].astype(BF16)
        return lax.dot_general(_stack_heads(q_ref.at[b * SEQ:(b + 1) * SEQ], slabs, lo, hi), kk, _NT,
                               preferred_element_type=F32)

    def row_max(unit, s):
        sink_heads = unit[1][2]
        if sink_heads is None:
            return s, jnp.maximum(jnp.full((s.shape[0], LANES), NEG_BIG, F32), s.max(axis=1, keepdims=True)), None
        sink = _sink_rows(sink_ref, sink_heads, SEQ)
        return s, jnp.maximum(sink, s.max(axis=1, keepdims=True)), sink

    def probs(unit, state):
        s, m, sink = state
        extra = None
        if sink is not None:
            e = jnp.exp2(sink - m)
            half = s.shape[0] // 2
            extra = jnp.concatenate([e[:half] * hi32, e[half:] * lo32], axis=0)
        return jnp.exp2(s - _tile_lanes(m, s.shape[1] // LANES)).astype(BF16), extra

    def weighted_sum(unit, state):
        p, extra = state
        b, (slabs, ks, _) = unit
        vv = v_ref[b * SEQ:(b + 1) * SEQ, ks * LANES:(ks + 1) * LANES].astype(BF16)
        o = _pv_with_denominator(p, vv, lo, hi)
        return o if extra is None else o + extra

    def finish(unit, o):
        b, (slabs, _, _) = unit
        _normalize_store(o, o_ref.at[b * SEQ:(b + 1) * SEQ], slabs, SEQ, lane_lo)

    _software_pipeline(units, [lambda u, _: scores(u), row_max, probs, weighted_sum, finish])


def _ctx_attention(q, k, v, sink, *, groups, seqs):
    kw = k.shape[1]
    rows = seqs * SEQ
    return pl.pallas_call(
        functools.partial(_ctx_attn_kernel, groups=groups),
        grid=(BATCH // seqs,),
        in_specs=[
            pl.BlockSpec(memory_space=pltpu.SMEM),
            pl.BlockSpec((rows, D_MODEL), lambda b: (b, 0)),
            pl.BlockSpec((rows, kw), lambda b: (b, 0)),
            pl.BlockSpec((rows, kw), lambda b: (b, 0)),
        ],
        out_specs=pl.BlockSpec((rows, D_MODEL), lambda b: (b, 0)),
        out_shape=jax.ShapeDtypeStruct((P_TOK, D_MODEL), BF16),
        compiler_params=_cparams(("parallel",)),
        name="ctx_attn",
    )(sink, q, k, v)


def _global_attn_kernel(q_ref, k_ref, v_ref, o_ref, s0, s1, p0, p1, a0, a1, m_ref, acc_ref, *, tk):
    tq = q_ref.shape[0]
    slabs = tuple(range(q_ref.shape[1] // LANES))
    nk = k_ref.shape[0] // tk
    lo, hi = _half_masks(BF16)
    lane_lo = lax.broadcasted_iota(jnp.int32, (tq, LANES), 1) < HEAD_DIM
    qstack = _stack_heads(q_ref, slabs, lo, hi)
    m_ref[...] = jnp.full(m_ref.shape, NEG_BIG, F32)
    acc_ref[...] = jnp.zeros(acc_ref.shape, F32)
    s_bufs, p_bufs, a_bufs = (s0, s1), (p0, p1), (a0, a1)

    def stage_a(c):
        s_bufs[c % 2][...] = lax.dot_general(qstack, k_ref[c * tk:(c + 1) * tk, :], _NT, preferred_element_type=F32)

    def stage_b(c):
        s = s_bufs[c % 2][...]
        m_old = m_ref[...]
        m_new = jnp.maximum(m_old, s.max(axis=1, keepdims=True))
        a_bufs[c % 2][...] = jnp.exp2(m_old - m_new)
        p_bufs[c % 2][...] = jnp.exp2(s - _tile_lanes(m_new, tk // LANES)).astype(BF16)
        m_ref[...] = m_new

    def stage_c(c):
        pv = _pv_with_denominator(p_bufs[c % 2][...], v_ref[c * tk:(c + 1) * tk, :], lo, hi)
        acc_ref[...] = a_bufs[c % 2][...] * acc_ref[...] + pv

    for t in range(nk + 2):
        if t < nk:
            stage_a(t)
        if 1 <= t <= nk:
            stage_b(t - 1)
        if t >= 2:
            stage_c(t - 2)
    _normalize_store(acc_ref[...], o_ref, slabs, tq, lane_lo)


def _global_attention(q, kcat, vcat):
    tq, tk = 256, 512
    t = kcat.shape[1]
    per_batch = DEC_SEQ // tq
    rows = N_HEADS_A * tq
    kv_spec = pl.BlockSpec((None, t, LANES), lambda b, i: (b, 0, 0))
    return pl.pallas_call(
        functools.partial(_global_attn_kernel, tk=tk),
        grid=(DEC_BATCH, per_batch),
        in_specs=[pl.BlockSpec((tq, 4 * LANES), lambda b, i: (b * per_batch + i, 0)), kv_spec, kv_spec],
        out_specs=pl.BlockSpec((tq, 4 * LANES), lambda b, i: (b * per_batch + i, 0)),
        out_shape=jax.ShapeDtypeStruct((S_TOK, 4 * LANES), BF16),
        scratch_shapes=[pltpu.VMEM((rows, tk), F32), pltpu.VMEM((rows, tk), F32),
                        pltpu.VMEM((rows, tk), BF16), pltpu.VMEM((rows, tk), BF16),
                        pltpu.VMEM((rows, LANES), F32), pltpu.VMEM((rows, LANES), F32),
                        pltpu.VMEM((rows, LANES), F32), pltpu.VMEM((rows, LANES), F32)],
        compiler_params=_cparams(("parallel", "parallel")),
        name="global_attn",
    )(q, kcat, vcat)


def _window_attn_kernel(sink_ref, q_ref, k_ref, v_ref, o_ref):
    tq = WINDOW
    n_blocks = q_ref.shape[0] // tq
    slabs = tuple(range(q_ref.shape[1] // LANES))
    n_heads = 2 * len(slabs)
    span = tq + 2 * WINDOW
    lo, hi = _half_masks(BF16)
    lo32, hi32 = _half_masks(F32)
    lane_lo = lax.broadcasted_iota(jnp.int32, (tq, LANES), 1) < HEAD_DIM
    row = lax.broadcasted_iota(jnp.int32, (tq, span), 0)
    col = lax.broadcasted_iota(jnp.int32, (tq, span), 1)
    band = (col >= row) & (col <= row + 2 * WINDOW)
    sink = _sink_rows(sink_ref, range(n_heads), tq)

    def local_offset(u):
        i = pl.program_id(1) * n_blocks + u
        return i, pl.multiple_of(PAST_LEN - WINDOW + i * tq, LANES)

    def scores(u):
        i, off = local_offset(u)
        kpos = i * tq - WINDOW + col
        valid = band & (kpos >= 0) & (kpos < DEC_SEQ)
        qstack = _stack_heads(q_ref.at[u * tq:(u + 1) * tq], slabs, lo, hi)
        s_loc = lax.dot_general(qstack, k_ref[pl.ds(off, span), :], _NT, preferred_element_type=F32)
        s_loc = jnp.where(valid[None], s_loc.reshape(n_heads, tq, span), NEG_BIG).reshape(n_heads * tq, span)
        return [s_loc, lax.dot_general(qstack, k_ref[0:PAST_LEN, :], _NT, preferred_element_type=F32)]

    def row_max(u, s_parts):
        m = sink
        for s in s_parts:
            m = jnp.maximum(m, s.max(axis=1, keepdims=True))
        return s_parts, m

    def probs(u, state):
        s_parts, m = state
        e = jnp.exp2(sink - m)
        half = e.shape[0] // 2
        extra = jnp.concatenate([e[:half] * hi32, e[half:] * lo32], axis=0)
        return [jnp.exp2(s - _tile_lanes(m, s.shape[1] // LANES)).astype(BF16) for s in s_parts], extra

    def weighted_sum(u, state):
        p_parts, extra = state
        _, off = local_offset(u)
        return (_pv_with_denominator(p_parts[0], v_ref[pl.ds(off, span), :], lo, hi)
                + _pv_with_denominator(p_parts[1], v_ref[0:PAST_LEN, :], lo, hi) + extra)

    def finish(u, o):
        _normalize_store(o, o_ref.at[u * tq:(u + 1) * tq], slabs, tq, lane_lo)

    _software_pipeline(list(range(n_blocks)), [
        lambda u, _: scores(u),
        lambda u, s_parts: probs(u, row_max(u, s_parts)),
        lambda u, state: finish(u, weighted_sum(u, state))])


def _window_attention(q, kcat, vcat, sink):
    tq = WINDOW_BLOCKS_PER_STEP * WINDOW
    t = kcat.shape[1]
    per_batch = DEC_SEQ // tq
    return pl.pallas_call(
        _window_attn_kernel,
        grid=(DEC_BATCH, per_batch),
        in_specs=[
            pl.BlockSpec(memory_space=pltpu.SMEM),
            pl.BlockSpec((tq, 4 * LANES), lambda b, i: (b * per_batch + i, 1)),
            pl.BlockSpec((None, t, LANES), lambda b, i: (b, 0, 0)),
            pl.BlockSpec((None, t, LANES), lambda b, i: (b, 0, 0)),
        ],
        out_specs=pl.BlockSpec((tq, 4 * LANES), lambda b, i: (b * per_batch + i, 0)),
        out_shape=jax.ShapeDtypeStruct((S_TOK, 4 * LANES), BF16),
        compiler_params=_cparams(("parallel", "parallel")),
        name="window_attn",
    )(sink, q, kcat, vcat)


def _na_start_row(g):
    return jnp.clip(NA_QROWS * g - NA_ROWS // 2, 0, DEC_SEQ // GRID_W - NA_KROWS)


def _na_step_bias(t_ref, half, g, lane_lo64):
    rows = DEC_SEQ // GRID_W
    start = _na_start_row(g)
    d0 = start - NA_QROWS * g + NA_ROWS - 1
    row_blocks = []
    for qr in range(NA_QROWS):
        rs = jnp.clip(NA_QROWS * g + qr - NA_ROWS // 2, 0, rows - NA_ROWS)
        blocks = []
        for m in range(NA_KROWS // 2):
            tiles = []
            for kr in (2 * m, 2 * m + 1):
                krow = start + kr
                inside = (krow >= rs) & (krow < rs + NA_ROWS)
                dr = jnp.clip(kr - qr + d0, 0, 2 * NA_ROWS - 2)
                tiles.append(t_ref[half, dr] + jnp.where(inside, 0.0, NEG_BIG))
            blocks.append(jnp.where(lane_lo64, tiles[0], tiles[1]))
        row_blocks.append(jnp.concatenate(blocks, axis=1))
    return jnp.concatenate(row_blocks, axis=0)


def _na_attn_kernel(q_ref, k_ref, v_ref, kc_ref, vc_ref, t_ref, o_ref):
    g = pl.program_id(2)
    tq = q_ref.shape[0]
    lo, hi = _half_masks(BF16)
    lane_lo = lax.broadcasted_iota(jnp.int32, (tq, LANES), 1) < HEAD_DIM
    lane_lo64 = lax.broadcasted_iota(jnp.int32, (GRID_W, LANES), 1) < HEAD_DIM
    off = pl.multiple_of(_na_start_row(g) * GRID_W, GRID_W)
    n_slabs = q_ref.shape[1] // LANES
    cols = [slice(n * LANES, (n + 1) * LANES) for n in range(n_slabs)]

    def scores(n):
        qstack = _stack_heads(q_ref, (n,), lo, hi)
        bias = jnp.concatenate([_na_step_bias(t_ref, 2 * n + half, g, lane_lo64) for half in range(2)], axis=0)
        k_loc = k_ref[pl.ds(off, NA_KROWS * GRID_W), cols[n]]
        s_loc = lax.dot_general(qstack, k_loc, _NT, preferred_element_type=F32) + bias
        return [s_loc, lax.dot_general(qstack, kc_ref[:, cols[n]], _NT, preferred_element_type=F32)]

    def row_max(n, s_parts):
        m = jnp.full((2 * tq, LANES), NEG_BIG, F32)
        for s in s_parts:
            m = jnp.maximum(m, s.max(axis=1, keepdims=True))
        return s_parts, m

    def probs(n, state):
        s_parts, m = state
        return [jnp.exp2(s - _tile_lanes(m, s.shape[1] // LANES)).astype(BF16) for s in s_parts]

    def weighted_sum(n, p_parts):
        v_loc = v_ref[pl.ds(off, NA_KROWS * GRID_W), cols[n]]
        return (_pv_with_denominator(p_parts[0], v_loc, lo, hi)
                + _pv_with_denominator(p_parts[1], vc_ref[:, cols[n]], lo, hi))

    def finish(n, o):
        _normalize_store(o, o_ref, (n,), tq, lane_lo)

    _software_pipeline(list(range(n_slabs)), [
        lambda n, _: scores(n),
        lambda n, s_parts: probs(n, row_max(n, s_parts)),
        lambda n, p_parts: finish(n, weighted_sum(n, p_parts))])


def _na_bias_table(rpb):
    cols = np.arange(GRID_W)
    cs = np.clip(cols - NA_COLS // 2, 0, GRID_W - NA_COLS)
    dc = cols[None, :] - cols[:, None] + NA_COLS - 1
    inside = (cols[None, :] >= cs[:, None]) & (cols[None, :] < cs[:, None] + NA_COLS)
    onehot = (np.arange(2 * NA_COLS - 1)[:, None, None] == dc[None]) & inside[None]
    t = jnp.einsum("hdc,cqk->hdqk", rpb.astype(F32), jnp.asarray(onehot, F32), precision=lax.Precision.HIGHEST)
    t = jnp.where(inside[None, None], t * LOG2E, NEG_BIG)
    return jnp.concatenate([t, t], axis=-1)


def _na_attention(q, k, v, kc, vc, bias):
    tq = NA_QROWS * GRID_W
    n_groups = DEC_SEQ // tq
    w = NA_SLABS_PER_STEP * LANES
    k3 = k.reshape(DEC_BATCH, DEC_SEQ, D_MODEL)
    v3 = v.reshape(DEC_BATCH, DEC_SEQ, D_MODEL)
    tok = lambda s, b, g: (b * n_groups + g, s)
    per_batch = lambda s, b, g: (b, 0, s)
    once = pl.Buffered(1)
    return pl.pallas_call(
        _na_attn_kernel,
        grid=(D_MODEL // w, DEC_BATCH, n_groups),
        in_specs=[
            pl.BlockSpec((tq, w), tok),
            pl.BlockSpec((None, DEC_SEQ, w), per_batch, pipeline_mode=once),
            pl.BlockSpec((None, DEC_SEQ, w), per_batch, pipeline_mode=once),
            pl.BlockSpec((None, PAST_LEN, w), per_batch, pipeline_mode=once),
            pl.BlockSpec((None, PAST_LEN, w), per_batch, pipeline_mode=once),
            pl.BlockSpec((2 * NA_SLABS_PER_STEP, 2 * NA_ROWS - 1, GRID_W, LANES), lambda s, b, g: (s, 0, 0, 0),
                         pipeline_mode=once),
        ],
        out_specs=pl.BlockSpec((tq, w), tok),
        out_shape=jax.ShapeDtypeStruct((S_TOK, D_MODEL), BF16),
        compiler_params=_cparams(("parallel", "parallel", "parallel")),
        name="na_attn",
    )(q, k3, v3, kc, vc, bias)


def _layer_norm(z, g, b):
    mu = jnp.mean(z, axis=-1, keepdims=True)
    zc = z - mu
    var = jnp.mean(zc * zc, axis=-1, keepdims=True)
    return zc * lax.rsqrt(var + LN_EPS) * g + b


def _post_kernel(*refs, n_parts):
    o_refs = refs[:n_parts]
    (y_ref, mod_ref, w_ref, lng_ref, lnb_ref, wr_hi_ref, wr_lo_ref, br_ref,
     y_out_ref, h_out_ref, gates_ref) = refs[n_parts:]
    pw = D_MODEL // n_parts
    mix = None
    for p, o_ref in enumerate(o_refs):
        part = jnp.dot(o_ref[...], w_ref[p * pw:(p + 1) * pw, :], preferred_element_type=F32)
        mix = part if mix is None else mix + part
    y = _layer_norm(ALPHA * y_ref[...] + mod_ref[2] * mix, lng_ref[...], lnb_ref[...])
    y_out_ref[...] = y
    h = y * (1.0 + mod_ref[4]) + mod_ref[3]
    h_hi = h.astype(BF16)
    h_out_ref[...] = h_hi
    h_lo = (h - h_hi.astype(F32)).astype(BF16)
    logits = lax.dot_general(wr_hi_ref[...], h_hi, _NT, preferred_element_type=F32)
    logits += lax.dot_general(wr_hi_ref[...], h_lo, _NT, preferred_element_type=F32)
    logits += lax.dot_general(wr_lo_ref[...], h_hi, _NT, preferred_element_type=F32)
    scores = 1.0 / (1.0 + jnp.exp(-logits))
    work = scores + br_ref[...]
    expert = lax.broadcasted_iota(jnp.int32, work.shape, 0).astype(F32)
    chosen = jnp.zeros(work.shape, F32)
    for _ in range(TOP_K):
        mx = work.max(axis=0, keepdims=True)
        first = jnp.where(work == mx, expert, float(N_EXPERTS)).min(axis=0, keepdims=True)
        pick = expert == first
        chosen = jnp.where(pick, scores, chosen)
        work = jnp.where(pick, NEG_BIG, work)
    gates_t = chosen / chosen.sum(axis=0, keepdims=True) * ROUTED_SCALE
    padded = jnp.concatenate([gates_t, jnp.zeros((LANES - N_EXPERTS, gates_t.shape[1]), F32)], axis=0)
    gates_ref[...] = padded.T[:, :N_EXPERTS]


def _post_mixer(o_parts, y2d, mod, w_out_bf16, ln_g, ln_b, wr_hi, wr_lo, b_router, *, latent):
    n = y2d.shape[0]
    n_parts = len(o_parts)
    pw = D_MODEL // n_parts
    per_batch = DEC_SEQ // TM
    mod_map = (lambda i: (0, 1 + i // per_batch, 0, 0)) if latent else (lambda i: (0, 0, 0, 0))
    o_specs = [pl.BlockSpec((TM, pw), functools.partial(lambda i, c: (i, c), c=col)) for _, col in o_parts]
    full = lambda i: (0, 0)
    return pl.pallas_call(
        functools.partial(_post_kernel, n_parts=n_parts),
        grid=(n // TM,),
        in_specs=o_specs + [
            pl.BlockSpec((TM, D_MODEL), lambda i: (i, 0)),
            pl.BlockSpec((6, None, 1, D_MODEL), mod_map),
            pl.BlockSpec((D_MODEL, D_MODEL), full),
            pl.BlockSpec((1, D_MODEL), full),
            pl.BlockSpec((1, D_MODEL), full),
            pl.BlockSpec((N_EXPERTS, D_MODEL), full),
            pl.BlockSpec((N_EXPERTS, D_MODEL), full),
            pl.BlockSpec((N_EXPERTS, 1), full),
        ],
        out_specs=[
            pl.BlockSpec((TM, D_MODEL), lambda i: (i, 0)),
            pl.BlockSpec((TM, D_MODEL), lambda i: (i, 0)),
            pl.BlockSpec((TM, N_EXPERTS), lambda i: (i, 0)),
        ],
        out_shape=[
            jax.ShapeDtypeStruct((n, D_MODEL), F32),
            jax.ShapeDtypeStruct((n, D_MODEL), BF16),
            jax.ShapeDtypeStruct((n, N_EXPERTS), F32),
        ],
        compiler_params=_cparams(("parallel",)),
        name="post_mixer",
    )(*[a for a, _ in o_parts], y2d, mod, w_out_bf16, ln_g, ln_b, wr_hi, wr_lo, b_router)


def _silu(x):
    return x / (1.0 + jnp.exp(-x))


def _moe_kernel(x_ref, gates_ref, y_ref, mod_ref, wg_ref, wu_ref, wd_ref, sg_ref, su_ref, sd_ref,
                lng_ref, lnb_ref, o_ref, acc_ref):
    e = pl.program_id(1)
    x = x_ref[...]

    @pl.when(e == 0)
    def _():
        a = _silu(jnp.dot(x, sg_ref[...].astype(BF16), preferred_element_type=F32))
        a = a * jnp.dot(x, su_ref[...].astype(BF16), preferred_element_type=F32)
        acc_ref[...] = jnp.dot(a.astype(BF16), sd_ref[...].astype(BF16), preferred_element_type=F32)

    gates = gates_ref[...]
    lane = lax.broadcasted_iota(jnp.int32, gates.shape, 1)
    acts = []
    for j in range(EXPERTS_PER_STEP):
        w1 = jnp.concatenate([wg_ref[j].astype(BF16), wu_ref[j].astype(BF16)], axis=1)
        hcat = jnp.dot(x, w1, preferred_element_type=F32)
        ge = jnp.sum(jnp.where(lane == e * EXPERTS_PER_STEP + j, gates, 0.0), axis=1, keepdims=True)
        acts.append((_silu(hcat[:, :D_EXPERT]) * hcat[:, D_EXPERT:] * ge).astype(BF16))
    w2 = jnp.concatenate([wd_ref[j].astype(BF16) for j in range(EXPERTS_PER_STEP)], axis=0)
    acc_ref[...] += jnp.dot(jnp.concatenate(acts, axis=1), w2, preferred_element_type=F32)

    @pl.when(e == pl.num_programs(1) - 1)
    def _():
        o_ref[...] = _layer_norm(ALPHA * y_ref[...] + mod_ref[5] * acc_ref[...], lng_ref[...], lnb_ref[...])


def _moe(x_bf16, gates, y2d, mod, w_gate, w_up, w_down, ws_gate, ws_up, ws_down, ln_g, ln_b, *, layer, latent):
    n = x_bf16.shape[0]
    per_batch = DEC_SEQ // TM_MOE
    mod_map = (lambda i, e: (0, 1 + i // per_batch, 0, 0)) if latent else (lambda i, e: (0, 0, 0, 0))
    ep = EXPERTS_PER_STEP
    tok = lambda i, e: (i, 0)
    full = lambda i, e: (0, 0)
    routed = lambda i, e: (layer, e, 0, 0)
    shared = lambda i, e: (layer, 0, 0)
    return pl.pallas_call(
        _moe_kernel,
        grid=(n // TM_MOE, N_EXPERTS // ep),
        in_specs=[
            pl.BlockSpec((TM_MOE, D_MODEL), tok),
            pl.BlockSpec((TM_MOE, N_EXPERTS), tok),
            pl.BlockSpec((TM_MOE, D_MODEL), tok, pipeline_mode=pl.Buffered(1)),
            pl.BlockSpec((6, None, 1, D_MODEL), mod_map),
            pl.BlockSpec((None, ep, D_MODEL, D_EXPERT), routed),
            pl.BlockSpec((None, ep, D_MODEL, D_EXPERT), routed),
            pl.BlockSpec((None, ep, D_EXPERT, D_MODEL), routed),
            pl.BlockSpec((None, D_MODEL, D_EXPERT), shared),
            pl.BlockSpec((None, D_MODEL, D_EXPERT), shared),
            pl.BlockSpec((None, D_EXPERT, D_MODEL), shared),
            pl.BlockSpec((1, D_MODEL), full),
            pl.BlockSpec((1, D_MODEL), full),
        ],
        out_specs=pl.BlockSpec((TM_MOE, D_MODEL), tok),
        out_shape=jax.ShapeDtypeStruct((n, D_MODEL), F32),
        scratch_shapes=[pltpu.VMEM((TM_MOE, D_MODEL), F32)],
        compiler_params=_cparams(("parallel", "arbitrary")),
        name="moe",
    )(x_bf16, gates, y2d, mod, w_gate, w_up, w_down, ws_gate, ws_up, ws_down, ln_g, ln_b)


def _slab_perm():
    idx = []
    for j in range(4):
        for half in range(2):
            head = j + 4 * half
            idx.extend(range(head * HEAD_DIM, (head + 1) * HEAD_DIM))
    return np.asarray(idx, np.int32)


def _rope_tables():
    t = np.arange(DEC_SEQ)
    quarter = HEAD_DIM // 4
    inv = ROPE_THETA ** (-np.arange(quarter, dtype=np.float64) / quarter)
    ar = (t // GRID_W)[:, None] * inv
    ac = (t % GRID_W)[:, None] * inv
    ang = np.concatenate([ar, ar, ac, ac] * 2, axis=-1)
    sign = np.where((np.arange(LANES) % 32) < 16, -1.0, 1.0)
    return jnp.asarray(np.cos(ang), F32), jnp.asarray(np.sin(ang) * sign, F32)


def kernel(x_prompt, x_sample, cache_ka, cache_va, cache_kb, cache_vb, cache_kc, cache_vc, c, c_ctx, w_ada, b_ada, ln_g, ln_b, w_in_even, w_out_even, qnorm_a, knorm_a, sink_b, w_in_odd, w_out_odd, rpb_c, w_router, b_router, w_gate, w_up, w_down, ws_gate, ws_up, ws_down):
    D = D_MODEL
    y_p = x_prompt.reshape(P_TOK, D)
    y_s = x_sample.reshape(S_TOK, D)

    cond8 = jnp.concatenate([c_ctx[None, :], c, jnp.zeros((8 - 1 - DEC_BATCH, D), F32)], axis=0)
    ada = _adaln(cond8, w_ada, b_ada)
    mods = ada.reshape(DEPTH, 8, 6, 1, D).transpose(0, 2, 1, 3, 4)

    cos, sin_signed = _rope_tables()
    perm = _slab_perm()
    in_perm = np.concatenate([perm, 512 + perm, np.arange(1024, EVEN_IN)])
    out_perm = np.concatenate([perm, 512 + perm])
    ones_row = jnp.ones((1, LANES), F32)
    no_sink = jnp.zeros((N_HEADS_B,), F32)

    new = {}
    for l in range(DEPTH):
        mod = mods[l]
        if l % 2 == 0:
            e = l // 2
            w_in = w_in_even[e][:, in_perm].astype(BF16)
            w_out = w_out_even[e][out_perm, :].astype(BF16)
            qn2 = jnp.tile(qnorm_a[e], 2)[None, :]
            kn2 = jnp.tile(knorm_a[e], 2)[None, :]
            proj = functools.partial(_project, w_bf16=w_in, qn2=qn2, kn2=kn2, cos=cos, sin_signed=sin_signed,
                                     dq=1024, dk=256, dv=256, q_norm=512, k_norm=128)
            q_p, k_p, v_p = proj(y_p, mod, rope=False, latent=False)
            q_s, k_s, v_s = proj(y_s, mod, rope=True, latent=True)
            new["ka"], new["kb"] = k_p[:, :LANES], k_p[:, LANES:]
            new["va"], new["vb"] = v_p[:, :LANES], v_p[:, LANES:]
            sink = sink_b[e].astype(F32)
            groups = (tuple(((j,), 0, None) for j in range(4))
                      + tuple(((4 + j,), 1, (j, j + 4)) for j in range(4)))
            o_p = _ctx_attention(q_p, k_p, v_p, sink, groups=groups, seqs=4)

            k_s3 = k_s.reshape(DEC_BATCH, DEC_SEQ, 2 * LANES)
            v_s3 = v_s.reshape(DEC_BATCH, DEC_SEQ, 2 * LANES)
            cka = cache_ka[:, e].reshape(DEC_BATCH, PAST_LEN, LANES).astype(BF16)
            cva = cache_va[:, e].reshape(DEC_BATCH, PAST_LEN, LANES).astype(BF16)
            ckb = cache_kb[:, e].reshape(DEC_BATCH, PAST_LEN, LANES).astype(BF16)
            cvb = cache_vb[:, e].reshape(DEC_BATCH, PAST_LEN, LANES).astype(BF16)
            pad = jnp.zeros((DEC_BATCH, WINDOW, LANES), BF16)
            o_a = _global_attention(q_s, jnp.concatenate([cka, k_s3[:, :, :LANES]], axis=1),
                                    jnp.concatenate([cva, v_s3[:, :, :LANES]], axis=1))
            o_b = _window_attention(q_s, jnp.concatenate([ckb, k_s3[:, :, LANES:], pad], axis=1),
                                    jnp.concatenate([cvb, v_s3[:, :, LANES:], pad], axis=1), sink)
            parts_p = [(o_p, 0), (o_p, 1)]
            parts_s = [(o_a, 0), (o_b, 0)]
        else:
            o = l // 2
            w_in = w_in_odd[o].astype(BF16)
            w_out = w_out_odd[o].astype(BF16)
            proj = functools.partial(_project, w_bf16=w_in, qn2=ones_row, kn2=ones_row, cos=cos,
                                     sin_signed=sin_signed, dq=1024, dk=1024, dv=1024, q_norm=0, k_norm=0,
                                     rope=False)
            q_p, k_p, v_p = proj(y_p, mod, latent=False)
            q_s, k_s, v_s = proj(y_s, mod, latent=True)
            new["kc"], new["vc"] = k_p, v_p
            groups = tuple(((j,), j, None) for j in range(D // LANES))
            o_p = _ctx_attention(q_p, k_p, v_p, no_sink, groups=groups, seqs=4)
            kc = cache_kc[:, o].reshape(DEC_BATCH, PAST_LEN, D).astype(BF16)
            vc = cache_vc[:, o].reshape(DEC_BATCH, PAST_LEN, D).astype(BF16)
            o_s = _na_attention(q_s, k_s, v_s, kc, vc, _na_bias_table(rpb_c[o]))
            parts_p = [(o_p, 0)]
            parts_s = [(o_s, 0)]

        wr = w_router[l].T
        wr_hi = wr.astype(BF16)
        wr_lo = (wr - wr_hi.astype(F32)).astype(BF16)
        post = functools.partial(_post_mixer, mod=mod, w_out_bf16=w_out, ln_g=ln_g[l, 0][None, :],
                                 ln_b=ln_b[l, 0][None, :], wr_hi=wr_hi, wr_lo=wr_lo, b_router=b_router[l][:, None])
        y_p, h_p, g_p = post(parts_p, y_p, latent=False)
        y_s, h_s, g_s = post(parts_s, y_s, latent=True)
        moe = functools.partial(_moe, mod=mod, w_gate=w_gate, w_up=w_up, w_down=w_down,
                                ws_gate=ws_gate, ws_up=ws_up, ws_down=ws_down,
                                ln_g=ln_g[l, 1][None, :], ln_b=ln_b[l, 1][None, :], layer=l)
        y_p = moe(h_p, g_p, y_p, latent=False)
        y_s = moe(h_s, g_s, y_s, latent=True)

    kv_a = (BATCH, 1, SEQ, N_KV_A, HEAD_DIM)
    kv_c = (BATCH, 1, SEQ, N_HEADS_C, HEAD_DIM)
    return (y_p.reshape(BATCH, SEQ, D), y_s.reshape(DEC_BATCH, DEC_SEQ, D),
            new["ka"].reshape(kv_a), new["va"].reshape(kv_a), new["kb"].reshape(kv_a), new["vb"].reshape(kv_a),
            new["kc"].reshape(kv_c), new["vc"].reshape(kv_c))
```

```python
import functools
import math

import numpy as np
import jax
import jax.numpy as jnp
from jax import lax
from jax.experimental import pallas as pl
from jax.experimental.pallas import tpu as pltpu

F32 = jnp.float32
BF16 = jnp.bfloat16

D_MODEL = 1024
BATCH = 16
SEQ = 256
DEPTH = 2
DEC_BATCH = 2
DEC_SEQ = 4096
PAST_LEN = 512
GRID_W = 64
HEAD_DIM = 64
N_HEADS_A = 8
N_KV_A = 2
N_HEADS_B = 8
N_KV_B = 2
N_HEADS_C = 16
EVEN_IN = 1536
ODD_IN = 3072
WINDOW = 128
NA_ROWS = 8
NA_COLS = 16
ROPE_THETA = 10000.0
N_EXPERTS = 64
TOP_K = 8
D_EXPERT = 128
ROUTED_SCALE = 2.5
ALPHA = (2 * DEPTH) ** 0.25
LN_EPS = 1e-6
RMS_EPS = 1e-6
NEG_BIG = -1e30
LOG2E = math.log2(math.e)

LANES = 128
P_TOK = BATCH * SEQ
S_TOK = DEC_BATCH * DEC_SEQ
TM = 512
TM_MOE = 1024
EXPERTS_PER_STEP = 8
NA_QROWS = 4
NA_KROWS = 12
NA_SLABS_PER_STEP = 8
WINDOW_BLOCKS_PER_STEP = 4
VMEM_LIMIT = 56 * 1024 * 1024

_NT = (((1,), (1,)), ((), ()))


def _cparams(sem):
    return pltpu.CompilerParams(dimension_semantics=sem, vmem_limit_bytes=VMEM_LIMIT)


def _half_masks(dtype):
    lane = lax.broadcasted_iota(jnp.int32, (1, LANES), 1)
    lo = jnp.where(lane < HEAD_DIM, 1.0, 0.0).astype(dtype)
    hi = jnp.where(lane < HEAD_DIM, 0.0, 1.0).astype(dtype)
    return lo, hi


def _ada_kernel(c_ref, w_ref, b_ref, o_ref):
    c = c_ref[...]
    a = c / (1.0 + jnp.exp(-c))
    w = w_ref[0]
    a_hi = a.astype(BF16)
    a_lo = (a - a_hi.astype(F32)).astype(BF16)
    w_hi = w.astype(BF16)
    w_lo = (w - w_hi.astype(F32)).astype(BF16)
    acc = jnp.dot(a_hi, w_hi, preferred_element_type=F32)
    acc += jnp.dot(a_lo, w_hi, preferred_element_type=F32)
    acc += jnp.dot(a_hi, w_lo, preferred_element_type=F32)
    o_ref[0] = acc + b_ref[0]


def _adaln(cond8, w_ada, b_ada):
    tn = 1536
    return pl.pallas_call(
        _ada_kernel,
        grid=(DEPTH, 6 * D_MODEL // tn),
        in_specs=[
            pl.BlockSpec((8, D_MODEL), lambda l, j: (0, 0)),
            pl.BlockSpec((1, D_MODEL, tn), lambda l, j: (l, 0, j)),
            pl.BlockSpec((1, 1, tn), lambda l, j: (l, 0, j)),
        ],
        out_specs=pl.BlockSpec((1, 8, tn), lambda l, j: (l, 0, j)),
        out_shape=jax.ShapeDtypeStruct((DEPTH, 8, 6 * D_MODEL), F32),
        compiler_params=_cparams(("parallel", "parallel")),
        name="adaln",
    )(cond8, w_ada, b_ada.reshape(DEPTH, 1, 6 * D_MODEL))


def _group_sum_matrix():
    r = (lax.broadcasted_iota(jnp.int32, (2 * LANES, LANES), 0) % LANES) // HEAD_DIM
    c = lax.broadcasted_iota(jnp.int32, (2 * LANES, LANES), 1) // HEAD_DIM
    return jnp.where(r == c, 1.0, 0.0).astype(BF16)


def _rms_slab(t, g, gmat):
    sq = t * t
    hi = sq.astype(BF16)
    lo = (sq - hi.astype(F32)).astype(BF16)
    ss = jnp.dot(jnp.concatenate([hi, lo], axis=1), gmat, preferred_element_type=F32)
    return t * lax.rsqrt(ss * (1.0 / HEAD_DIM) + RMS_EPS) * g


def _rope_slab(t, cos, sin_signed, first):
    r = jnp.where(first, pltpu.roll(t, LANES - 16, 1), pltpu.roll(t, 16, 1))
    return t * cos + r * sin_signed


def _proj_kernel(x_ref, mod_ref, w_ref, qn_ref, kn_ref, cos_ref, sin_ref, q_ref, k_ref, v_ref,
                 *, dq, dk, q_norm, k_norm, rope):
    x = x_ref[...]
    h = (x * (1.0 + mod_ref[1]) + mod_ref[0]).astype(BF16)
    y = jnp.dot(h, w_ref[...], preferred_element_type=F32)
    tm = x.shape[0]
    gmat = _group_sum_matrix() if (q_norm or k_norm) else None
    if rope:
        cos = cos_ref[...]
        sin_signed = sin_ref[...]
        first = (lax.broadcasted_iota(jnp.int32, (tm, LANES), 1) % 32) < 16
    for s in range(dq // LANES):
        t = y[:, s * LANES:(s + 1) * LANES]
        if s * LANES < q_norm:
            t = _rms_slab(t, qn_ref[...], gmat)
        if rope:
            t = _rope_slab(t, cos, sin_signed, first)
        q_ref[:, s * LANES:(s + 1) * LANES] = (t * (HEAD_DIM ** -0.5 * LOG2E)).astype(q_ref.dtype)
    for s in range(dk // LANES):
        t = y[:, dq + s * LANES:dq + (s + 1) * LANES]
        if s * LANES < k_norm:
            t = _rms_slab(t, kn_ref[...], gmat)
        if rope:
            t = _rope_slab(t, cos, sin_signed, first)
        k_ref[:, s * LANES:(s + 1) * LANES] = t.astype(k_ref.dtype)
    v_ref[...] = y[:, dq + dk:].astype(v_ref.dtype)


def _project(x2d, mod, w_bf16, qn2, kn2, cos, sin_signed, *, dq, dk, dv, q_norm, k_norm, rope, latent):
    n = x2d.shape[0]
    per_batch = DEC_SEQ // TM
    if latent:
        mod_map = lambda i: (0, 1 + i // per_batch, 0, 0)
        pos_map = lambda i: (i % per_batch, 0)
        kv_dtype = BF16
    else:
        mod_map = lambda i: (0, 0, 0, 0)
        pos_map = lambda i: (0, 0)
        kv_dtype = F32
    kern = functools.partial(_proj_kernel, dq=dq, dk=dk, q_norm=q_norm, k_norm=k_norm, rope=rope)
    return pl.pallas_call(
        kern,
        grid=(n // TM,),
        in_specs=[
            pl.BlockSpec((TM, D_MODEL), lambda i: (i, 0)),
            pl.BlockSpec((6, None, 1, D_MODEL), mod_map),
            pl.BlockSpec((D_MODEL, dq + dk + dv), lambda i: (0, 0)),
            pl.BlockSpec((1, LANES), lambda i: (0, 0)),
            pl.BlockSpec((1, LANES), lambda i: (0, 0)),
            pl.BlockSpec((TM, LANES), pos_map),
            pl.BlockSpec((TM, LANES), pos_map),
        ],
        out_specs=[
            pl.BlockSpec((TM, dq), lambda i: (i, 0)),
            pl.BlockSpec((TM, dk), lambda i: (i, 0)),
            pl.BlockSpec((TM, dv), lambda i: (i, 0)),
        ],
        out_shape=[
            jax.ShapeDtypeStruct((n, dq), BF16),
            jax.ShapeDtypeStruct((n, dk), kv_dtype),
            jax.ShapeDtypeStruct((n, dv), kv_dtype),
        ],
        compiler_params=_cparams(("parallel",)),
        name="in_proj",
    )(x2d, mod, w_bf16, qn2, kn2, cos, sin_signed)


def _stack_heads(q_ref, slabs, lo, hi):
    qs = [q_ref[:, j * LANES:(j + 1) * LANES] for j in slabs]
    return jnp.concatenate([q * lo for q in qs] + [q * hi for q in qs], axis=0)


def _tile_lanes(x, n):
    return jnp.concatenate([x] * n, axis=1)


def _pv_with_denominator(p, v, lo, hi):
    half = p.shape[0] // 2
    pv_lo = jnp.dot(p[:half], v * lo + hi, preferred_element_type=F32)
    pv_hi = jnp.dot(p[half:], v * hi + lo, preferred_element_type=F32)
    return jnp.concatenate([pv_lo, pv_hi], axis=0)


def _normalize_store(o, o_ref, slabs, tq, lane_lo):
    half = len(slabs) * tq
    for n, j in enumerate(slabs):
        o_lo = o[n * tq:(n + 1) * tq]
        o_hi = o[half + n * tq:half + (n + 1) * tq]
        o_lo = o_lo * (1.0 / pltpu.roll(o_lo, HEAD_DIM, 1))
        o_hi = o_hi * (1.0 / pltpu.roll(o_hi, HEAD_DIM, 1))
        o_ref[:, j * LANES:(j + 1) * LANES] = jnp.where(lane_lo, o_lo, o_hi).astype(o_ref.dtype)


def _sink_rows(sink_ref, heads, tq):
    return jnp.concatenate([jnp.full((tq, LANES), sink_ref[h] * LOG2E, F32) for h in heads], axis=0)


def _software_pipeline(units, stages):
    live = [dict() for _ in stages]
    for t in range(len(units) + len(stages) - 1):
        for k, stage in enumerate(stages):
            u = t - k
            if 0 <= u < len(units):
                live[k][u] = stage(units[u], live[k - 1].pop(u) if k else None)


def _softmax_pv(parts, sink, lo, hi):
    rows = parts[0][0].shape[0]
    m = jnp.full((rows, LANES), NEG_BIG, F32) if sink is None else sink
    for s, _ in parts:
        m = jnp.maximum(m, s.max(axis=1, keepdims=True))
    o = None
    for s, v in parts:
        p = jnp.exp2(s - _tile_lanes(m, s.shape[1] // LANES)).astype(BF16)
        pv = _pv_with_denominator(p, v, lo, hi)
        o = pv if o is None else o + pv
    if sink is not None:
        lo32, hi32 = _half_masks(F32)
        e = jnp.exp2(sink - m)
        half = rows // 2
        o = o + jnp.concatenate([e[:half] * hi32, e[half:] * lo32], axis=0)
    return o


def _ctx_attn_kernel(sink_ref, q_ref, k_ref, v_ref, o_ref, *, groups):
    lo, hi = _half_masks(BF16)
    lo32, hi32 = _half_masks(F32)
    lane_lo = lax.broadcasted_iota(jnp.int32, (SEQ, LANES), 1) < HEAD_DIM
    units = [(b, grp) for b in range(q_ref.shape[0] // SEQ) for grp in groups]

    def scores(unit):
        b, (slabs, ks, _) = unit
        kk = k_ref[b * SEQ:(b + 1) * SEQ, ks * LANES:(ks + 1) * LANES].astype(BF16)
        return lax.dot_general(_stack_heads(q_ref.at[b * SEQ:(b + 1) * SEQ], slabs, lo, hi), kk, _NT,
                               preferred_element_type=F32)

    def row_max(unit, s):
        sink_heads = unit[1][2]
        if sink_heads is None:
            return s, jnp.maximum(jnp.full((s.shape[0], LANES), NEG_BIG, F32), s.max(axis=1, keepdims=True)), None
        sink = _sink_rows(sink_ref, sink_heads, SEQ)
        return s, jnp.maximum(sink, s.max(axis=1, keepdims=True)), sink

    def probs(unit, state):
        s, m, sink = state
        extra = None
        if sink is not None:
            e = jnp.exp2(sink - m)
            half = s.shape[0] // 2
            extra = jnp.concatenate([e[:half] * hi32, e[half:] * lo32], axis=0)
        return jnp.exp2(s - _tile_lanes(m, s.shape[1] // LANES)).astype(BF16), extra

    def weighted_sum(unit, state):
        p, extra = state
        b, (slabs, ks, _) = unit
        vv = v_ref[b * SEQ:(b + 1) * SEQ, ks * LANES:(ks + 1) * LANES].astype(BF16)
        o = _pv_with_denominator(p, vv, lo, hi)
        return o if extra is None else o + extra

    def finish(unit, o):
        b, (slabs, _, _) = unit
        _normalize_store(o, o_ref.at[b * SEQ:(b + 1) * SEQ], slabs, SEQ, lane_lo)

    _software_pipeline(units, [lambda u, _: scores(u), row_max, probs, weighted_sum, finish])


def _ctx_attention(q, k, v, sink, *, groups, seqs):
    kw = k.shape[1]
    rows = seqs * SEQ
    return pl.pallas_call(
        functools.partial(_ctx_attn_kernel, groups=groups),
        grid=(BATCH // seqs,),
        in_specs=[
            pl.BlockSpec(memory_space=pltpu.SMEM),
            pl.BlockSpec((rows, D_MODEL), lambda b: (b, 0)),
            pl.BlockSpec((rows, kw), lambda b: (b, 0)),
            pl.BlockSpec((rows, kw), lambda b: (b, 0)),
        ],
        out_specs=pl.BlockSpec((rows, D_MODEL), lambda b: (b, 0)),
        out_shape=jax.ShapeDtypeStruct((P_TOK, D_MODEL), BF16),
        compiler_params=_cparams(("parallel",)),
        name="ctx_attn",
    )(sink, q, k, v)


def _global_attn_kernel(q_ref, kc_ref, vc_ref, k_ref, v_ref, o_ref, s0, s1, p0, p1, a0, a1, m_ref, acc_ref, *, tk):
    tq = q_ref.shape[0]
    slabs = tuple(range(q_ref.shape[1] // LANES))
    nk = 1 + k_ref.shape[0] // tk

    def keys(c):
        return kc_ref[...].astype(BF16) if c == 0 else k_ref[(c - 1) * tk:c * tk, :]

    def values(c):
        return vc_ref[...].astype(BF16) if c == 0 else v_ref[(c - 1) * tk:c * tk, :]

    lo, hi = _half_masks(BF16)
    lane_lo = lax.broadcasted_iota(jnp.int32, (tq, LANES), 1) < HEAD_DIM
    qstack = _stack_heads(q_ref, slabs, lo, hi)
    m_ref[...] = jnp.full(m_ref.shape, NEG_BIG, F32)
    acc_ref[...] = jnp.zeros(acc_ref.shape, F32)
    s_bufs, p_bufs, a_bufs = (s0, s1), (p0, p1), (a0, a1)

    def stage_a(c):
        s_bufs[c % 2][...] = lax.dot_general(qstack, keys(c), _NT, preferred_element_type=F32)

    def stage_b(c):
        s = s_bufs[c % 2][...]
        m_old = m_ref[...]
        m_new = jnp.maximum(m_old, s.max(axis=1, keepdims=True))
        a_bufs[c % 2][...] = jnp.exp2(m_old - m_new)
        p_bufs[c % 2][...] = jnp.exp2(s - _tile_lanes(m_new, tk // LANES)).astype(BF16)
        m_ref[...] = m_new

    def stage_c(c):
        pv = _pv_with_denominator(p_bufs[c % 2][...], values(c), lo, hi)
        acc_ref[...] = a_bufs[c % 2][...] * acc_ref[...] + pv

    for t in range(nk + 2):
        if t < nk:
            stage_a(t)
        if 1 <= t <= nk:
            stage_b(t - 1)
        if t >= 2:
            stage_c(t - 2)
    _normalize_store(acc_ref[...], o_ref, slabs, tq, lane_lo)


def _global_attention(q, kc, vc, k, v):
    tq, tk = 256, PAST_LEN
    per_batch = DEC_SEQ // tq
    rows = N_HEADS_A * tq
    ctx_spec = pl.BlockSpec((None, PAST_LEN, LANES), lambda b, i: (b, 0, 0))
    new_spec = pl.BlockSpec((None, DEC_SEQ, LANES), lambda b, i: (b, 0, 0))
    return pl.pallas_call(
        functools.partial(_global_attn_kernel, tk=tk),
        grid=(DEC_BATCH, per_batch),
        in_specs=[pl.BlockSpec((tq, 4 * LANES), lambda b, i: (b * per_batch + i, 0)),
                  ctx_spec, ctx_spec, new_spec, new_spec],
        out_specs=pl.BlockSpec((tq, 4 * LANES), lambda b, i: (b * per_batch + i, 0)),
        out_shape=jax.ShapeDtypeStruct((S_TOK, 4 * LANES), BF16),
        scratch_shapes=[pltpu.VMEM((rows, tk), F32), pltpu.VMEM((rows, tk), F32),
                        pltpu.VMEM((rows, tk), BF16), pltpu.VMEM((rows, tk), BF16),
                        pltpu.VMEM((rows, LANES), F32), pltpu.VMEM((rows, LANES), F32),
                        pltpu.VMEM((rows, LANES), F32), pltpu.VMEM((rows, LANES), F32)],
        compiler_params=_cparams(("parallel", "parallel")),
        name="global_attn",
    )(q, kc, vc, k, v)


def _window_attn_kernel(sink_ref, q_ref, k_ref, v_ref, o_ref):
    tq = WINDOW
    n_blocks = q_ref.shape[0] // tq
    slabs = tuple(range(q_ref.shape[1] // LANES))
    n_heads = 2 * len(slabs)
    span = tq + 2 * WINDOW
    lo, hi = _half_masks(BF16)
    lo32, hi32 = _half_masks(F32)
    lane_lo = lax.broadcasted_iota(jnp.int32, (tq, LANES), 1) < HEAD_DIM
    row = lax.broadcasted_iota(jnp.int32, (tq, span), 0)
    col = lax.broadcasted_iota(jnp.int32, (tq, span), 1)
    band = (col >= row) & (col <= row + 2 * WINDOW)
    sink = _sink_rows(sink_ref, range(n_heads), tq)

    def local_offset(u):
        i = pl.program_id(1) * n_blocks + u
        return i, pl.multiple_of(PAST_LEN - WINDOW + i * tq, LANES)

    def scores(u):
        i, off = local_offset(u)
        kpos = i * tq - WINDOW + col
        valid = band & (kpos >= 0) & (kpos < DEC_SEQ)
        qstack = _stack_heads(q_ref.at[u * tq:(u + 1) * tq], slabs, lo, hi)
        s_loc = lax.dot_general(qstack, k_ref[pl.ds(off, span), :], _NT, preferred_element_type=F32)
        s_loc = jnp.where(valid[None], s_loc.reshape(n_heads, tq, span), NEG_BIG).reshape(n_heads * tq, span)
        return [s_loc, lax.dot_general(qstack, k_ref[0:PAST_LEN, :], _NT, preferred_element_type=F32)]

    def row_max(u, s_parts):
        m = sink
        for s in s_parts:
            m = jnp.maximum(m, s.max(axis=1, keepdims=True))
        return s_parts, m

    def probs(u, state):
        s_parts, m = state
        e = jnp.exp2(sink - m)
        half = e.shape[0] // 2
        extra = jnp.concatenate([e[:half] * hi32, e[half:] * lo32], axis=0)
        return [jnp.exp2(s - _tile_lanes(m, s.shape[1] // LANES)).astype(BF16) for s in s_parts], extra

    def weighted_sum(u, state):
        p_parts, extra = state
        _, off = local_offset(u)
        return (_pv_with_denominator(p_parts[0], v_ref[pl.ds(off, span), :], lo, hi)
                + _pv_with_denominator(p_parts[1], v_ref[0:PAST_LEN, :], lo, hi) + extra)

    def finish(u, o):
        _normalize_store(o, o_ref.at[u * tq:(u + 1) * tq], slabs, tq, lane_lo)

    _software_pipeline(list(range(n_blocks)), [
        lambda u, _: scores(u),
        lambda u, s_parts: probs(u, row_max(u, s_parts)),
        lambda u, state: finish(u, weighted_sum(u, state))])


def _window_attention(q, kcat, vcat, sink):
    tq = WINDOW_BLOCKS_PER_STEP * WINDOW
    t = kcat.shape[1]
    per_batch = DEC_SEQ // tq
    return pl.pallas_call(
        _window_attn_kernel,
        grid=(DEC_BATCH, per_batch),
        in_specs=[
            pl.BlockSpec(memory_space=pltpu.SMEM),
            pl.BlockSpec((tq, 4 * LANES), lambda b, i: (b * per_batch + i, 1)),
            pl.BlockSpec((None, t, LANES), lambda b, i: (b, 0, 0)),
            pl.BlockSpec((None, t, LANES), lambda b, i: (b, 0, 0)),
        ],
        out_specs=pl.BlockSpec((tq, 4 * LANES), lambda b, i: (b * per_batch + i, 0)),
        out_shape=jax.ShapeDtypeStruct((S_TOK, 4 * LANES), BF16),
        compiler_params=_cparams(("parallel", "parallel")),
        name="window_attn",
    )(sink, q, kcat, vcat)


def _na_start_row(g):
    return jnp.clip(NA_QROWS * g - NA_ROWS // 2, 0, DEC_SEQ // GRID_W - NA_KROWS)


def _na_step_bias(t_ref, half, g, lane_lo64):
    rows = DEC_SEQ // GRID_W
    start = _na_start_row(g)
    d0 = start - NA_QROWS * g + NA_ROWS - 1
    row_blocks = []
    for qr in range(NA_QROWS):
        rs = jnp.clip(NA_QROWS * g + qr - NA_ROWS // 2, 0, rows - NA_ROWS)
        blocks = []
        for m in range(NA_KROWS // 2):
            tiles = []
            for kr in (2 * m, 2 * m + 1):
                krow = start + kr
                inside = (krow >= rs) & (krow < rs + NA_ROWS)
                dr = jnp.clip(kr - qr + d0, 0, 2 * NA_ROWS - 2)
                tiles.append(t_ref[half, dr] + jnp.where(inside, 0.0, NEG_BIG))
            blocks.append(jnp.where(lane_lo64, tiles[0], tiles[1]))
        row_blocks.append(jnp.concatenate(blocks, axis=1))
    return jnp.concatenate(row_blocks, axis=0)


def _na_attn_kernel(q_ref, k_ref, v_ref, kc_ref, vc_ref, t_ref, o_ref):
    g = pl.program_id(2)
    tq = q_ref.shape[0]
    lo, hi = _half_masks(BF16)
    lane_lo = lax.broadcasted_iota(jnp.int32, (tq, LANES), 1) < HEAD_DIM
    lane_lo64 = lax.broadcasted_iota(jnp.int32, (GRID_W, LANES), 1) < HEAD_DIM
    off = pl.multiple_of(_na_start_row(g) * GRID_W, GRID_W)
    n_slabs = q_ref.shape[1] // LANES
    cols = [slice(n * LANES, (n + 1) * LANES) for n in range(n_slabs)]

    def scores(n):
        qstack = _stack_heads(q_ref, (n,), lo, hi)
        bias = jnp.concatenate([_na_step_bias(t_ref, 2 * n + half, g, lane_lo64) for half in range(2)], axis=0)
        k_loc = k_ref[pl.ds(off, NA_KROWS * GRID_W), cols[n]]
        s_loc = lax.dot_general(qstack, k_loc, _NT, preferred_element_type=F32) + bias
        return [s_loc, lax.dot_general(qstack, kc_ref[:, cols[n]], _NT, preferred_element_type=F32)]

    def row_max(n, s_parts):
        m = jnp.full((2 * tq, LANES), NEG_BIG, F32)
        for s in s_parts:
            m = jnp.maximum(m, s.max(axis=1, keepdims=True))
        return s_parts, m

    def probs(n, state):
        s_parts, m = state
        return [jnp.exp2(s - _tile_lanes(m, s.shape[1] // LANES)).astype(BF16) for s in s_parts]

    def weighted_sum(n, p_parts):
        v_loc = v_ref[pl.ds(off, NA_KROWS * GRID_W), cols[n]]
        return (_pv_with_denominator(p_parts[0], v_loc, lo, hi)
                + _pv_with_denominator(p_parts[1], vc_ref[:, cols[n]], lo, hi))

    def finish(n, o):
        _normalize_store(o, o_ref, (n,), tq, lane_lo)

    _software_pipeline(list(range(n_slabs)), [
        lambda n, _: scores(n),
        lambda n, s_parts: probs(n, row_max(n, s_parts)),
        lambda n, p_parts: finish(n, weighted_sum(n, p_parts))])


def _na_bias_table(rpb):
    cols = np.arange(GRID_W)
    cs = np.clip(cols - NA_COLS // 2, 0, GRID_W - NA_COLS)
    dc = cols[None, :] - cols[:, None] + NA_COLS - 1
    inside = (cols[None, :] >= cs[:, None]) & (cols[None, :] < cs[:, None] + NA_COLS)
    onehot = (np.arange(2 * NA_COLS - 1)[:, None, None] == dc[None]) & inside[None]
    t = jnp.einsum("hdc,cqk->hdqk", rpb.astype(F32), jnp.asarray(onehot, F32), precision=lax.Precision.HIGHEST)
    t = jnp.where(inside[None, None], t * LOG2E, NEG_BIG)
    return jnp.concatenate([t, t], axis=-1)


def _na_attention(q, k, v, kc, vc, bias):
    tq = NA_QROWS * GRID_W
    n_groups = DEC_SEQ // tq
    w = NA_SLABS_PER_STEP * LANES
    k3 = k.reshape(DEC_BATCH, DEC_SEQ, D_MODEL)
    v3 = v.reshape(DEC_BATCH, DEC_SEQ, D_MODEL)
    tok = lambda s, b, g: (b * n_groups + g, s)
    per_batch = lambda s, b, g: (b, 0, s)
    once = pl.Buffered(1)
    return pl.pallas_call(
        _na_attn_kernel,
        grid=(D_MODEL // w, DEC_BATCH, n_groups),
        in_specs=[
            pl.BlockSpec((tq, w), tok),
            pl.BlockSpec((None, DEC_SEQ, w), per_batch, pipeline_mode=once),
            pl.BlockSpec((None, DEC_SEQ, w), per_batch, pipeline_mode=once),
            pl.BlockSpec((None, PAST_LEN, w), per_batch, pipeline_mode=once),
            pl.BlockSpec((None, PAST_LEN, w), per_batch, pipeline_mode=once),
            pl.BlockSpec((2 * NA_SLABS_PER_STEP, 2 * NA_ROWS - 1, GRID_W, LANES), lambda s, b, g: (s, 0, 0, 0),
                         pipeline_mode=once),
        ],
        out_specs=pl.BlockSpec((tq, w), tok),
        out_shape=jax.ShapeDtypeStruct((S_TOK, D_MODEL), BF16),
        compiler_params=_cparams(("parallel", "parallel", "parallel")),
        name="na_attn",
    )(q, k3, v3, kc, vc, bias)


def _layer_norm(z, g, b):
    mu = jnp.mean(z, axis=-1, keepdims=True)
    zc = z - mu
    var = jnp.mean(zc * zc, axis=-1, keepdims=True)
    return zc * lax.rsqrt(var + LN_EPS) * g + b


def _post_kernel(*refs, n_parts):
    o_refs = refs[:n_parts]
    (y_ref, mod_ref, w_ref, lng_ref, lnb_ref, wr_hi_ref, wr_lo_ref, br_ref,
     y_out_ref, h_out_ref, gates_ref) = refs[n_parts:]
    pw = D_MODEL // n_parts
    mix = None
    for p, o_ref in enumerate(o_refs):
        part = jnp.dot(o_ref[...], w_ref[p * pw:(p + 1) * pw, :], preferred_element_type=F32)
        mix = part if mix is None else mix + part
    y = _layer_norm(ALPHA * y_ref[...] + mod_ref[2] * mix, lng_ref[...], lnb_ref[...])
    y_out_ref[...] = y
    h = y * (1.0 + mod_ref[4]) + mod_ref[3]
    h_hi = h.astype(BF16)
    h_out_ref[...] = h_hi
    h_lo = (h - h_hi.astype(F32)).astype(BF16)
    logits = lax.dot_general(wr_hi_ref[...], h_hi, _NT, preferred_element_type=F32)
    logits += lax.dot_general(wr_hi_ref[...], h_lo, _NT, preferred_element_type=F32)
    logits += lax.dot_general(wr_lo_ref[...], h_hi, _NT, preferred_element_type=F32)
    scores = 1.0 / (1.0 + jnp.exp(-logits))
    work = scores + br_ref[...]
    expert = lax.broadcasted_iota(jnp.int32, work.shape, 0).astype(F32)
    chosen = jnp.zeros(work.shape, F32)
    for _ in range(TOP_K):
        mx = work.max(axis=0, keepdims=True)
        first = jnp.where(work == mx, expert, float(N_EXPERTS)).min(axis=0, keepdims=True)
        pick = expert == first
        chosen = jnp.where(pick, scores, chosen)
        work = jnp.where(pick, NEG_BIG, work)
    gates_t = chosen / chosen.sum(axis=0, keepdims=True) * ROUTED_SCALE
    padded = jnp.concatenate([gates_t, jnp.zeros((LANES - N_EXPERTS, gates_t.shape[1]), F32)], axis=0)
    gates_ref[...] = padded.T[:, :N_EXPERTS]


def _post_mixer(o_parts, y2d, mod, w_out_bf16, ln_g, ln_b, wr_hi, wr_lo, b_router, *, latent):
    n = y2d.shape[0]
    n_parts = len(o_parts)
    pw = D_MODEL // n_parts
    per_batch = DEC_SEQ // TM
    mod_map = (lambda i: (0, 1 + i // per_batch, 0, 0)) if latent else (lambda i: (0, 0, 0, 0))
    o_specs = [pl.BlockSpec((TM, pw), functools.partial(lambda i, c: (i, c), c=col)) for _, col in o_parts]
    full = lambda i: (0, 0)
    return pl.pallas_call(
        functools.partial(_post_kernel, n_parts=n_parts),
        grid=(n // TM,),
        in_specs=o_specs + [
            pl.BlockSpec((TM, D_MODEL), lambda i: (i, 0)),
            pl.BlockSpec((6, None, 1, D_MODEL), mod_map),
            pl.BlockSpec((D_MODEL, D_MODEL), full),
            pl.BlockSpec((1, D_MODEL), full),
            pl.BlockSpec((1, D_MODEL), full),
            pl.BlockSpec((N_EXPERTS, D_MODEL), full),
            pl.BlockSpec((N_EXPERTS, D_MODEL), full),
            pl.BlockSpec((N_EXPERTS, 1), full),
        ],
        out_specs=[
            pl.BlockSpec((TM, D_MODEL), lambda i: (i, 0)),
            pl.BlockSpec((TM, D_MODEL), lambda i: (i, 0)),
            pl.BlockSpec((TM, N_EXPERTS), lambda i: (i, 0)),
        ],
        out_shape=[
            jax.ShapeDtypeStruct((n, D_MODEL), F32),
            jax.ShapeDtypeStruct((n, D_MODEL), BF16),
            jax.ShapeDtypeStruct((n, N_EXPERTS), F32),
        ],
        compiler_params=_cparams(("parallel",)),
        name="post_mixer",
    )(*[a for a, _ in o_parts], y2d, mod, w_out_bf16, ln_g, ln_b, wr_hi, wr_lo, b_router)


def _silu(x):
    return x / (1.0 + jnp.exp(-x))


def _moe_kernel(x_ref, gates_ref, y_ref, mod_ref, wg_ref, wu_ref, wd_ref, sg_ref, su_ref, sd_ref,
                lng_ref, lnb_ref, o_ref, acc_ref):
    e = pl.program_id(1)
    x = x_ref[...]

    @pl.when(e == 0)
    def _():
        a = _silu(jnp.dot(x, sg_ref[...].astype(BF16), preferred_element_type=F32))
        a = a * jnp.dot(x, su_ref[...].astype(BF16), preferred_element_type=F32)
        acc_ref[...] = jnp.dot(a.astype(BF16), sd_ref[...].astype(BF16), preferred_element_type=F32)

    gates = gates_ref[...]
    lane = lax.broadcasted_iota(jnp.int32, gates.shape, 1)
    acts = []
    for j in range(EXPERTS_PER_STEP):
        w1 = jnp.concatenate([wg_ref[j].astype(BF16), wu_ref[j].astype(BF16)], axis=1)
        hcat = jnp.dot(x, w1, preferred_element_type=F32)
        ge = jnp.sum(jnp.where(lane == e * EXPERTS_PER_STEP + j, gates, 0.0), axis=1, keepdims=True)
        acts.append((_silu(hcat[:, :D_EXPERT]) * hcat[:, D_EXPERT:] * ge).astype(BF16))
    w2 = jnp.concatenate([wd_ref[j].astype(BF16) for j in range(EXPERTS_PER_STEP)], axis=0)
    acc_ref[...] += jnp.dot(jnp.concatenate(acts, axis=1), w2, preferred_element_type=F32)

    @pl.when(e == pl.num_programs(1) - 1)
    def _():
        o_ref[...] = _layer_norm(ALPHA * y_ref[...] + mod_ref[5] * acc_ref[...], lng_ref[...], lnb_ref[...])


def _moe(x_bf16, gates, y2d, mod, w_gate, w_up, w_down, ws_gate, ws_up, ws_down, ln_g, ln_b, *, layer, latent):
    n = x_bf16.shape[0]
    per_batch = DEC_SEQ // TM_MOE
    mod_map = (lambda i, e: (0, 1 + i // per_batch, 0, 0)) if latent else (lambda i, e: (0, 0, 0, 0))
    ep = EXPERTS_PER_STEP
    tok = lambda i, e: (i, 0)
    full = lambda i, e: (0, 0)
    routed = lambda i, e: (layer, e, 0, 0)
    shared = lambda i, e: (layer, 0, 0)
    return pl.pallas_call(
        _moe_kernel,
        grid=(n // TM_MOE, N_EXPERTS // ep),
        in_specs=[
            pl.BlockSpec((TM_MOE, D_MODEL), tok),
            pl.BlockSpec((TM_MOE, N_EXPERTS), tok),
            pl.BlockSpec((TM_MOE, D_MODEL), tok, pipeline_mode=pl.Buffered(1)),
            pl.BlockSpec((6, None, 1, D_MODEL), mod_map),
            pl.BlockSpec((None, ep, D_MODEL, D_EXPERT), routed),
            pl.BlockSpec((None, ep, D_MODEL, D_EXPERT), routed),
            pl.BlockSpec((None, ep, D_EXPERT, D_MODEL), routed),
            pl.BlockSpec((None, D_MODEL, D_EXPERT), shared),
            pl.BlockSpec((None, D_MODEL, D_EXPERT), shared),
            pl.BlockSpec((None, D_EXPERT, D_MODEL), shared),
            pl.BlockSpec((1, D_MODEL), full),
            pl.BlockSpec((1, D_MODEL), full),
        ],
        out_specs=pl.BlockSpec((TM_MOE, D_MODEL), tok),
        out_shape=jax.ShapeDtypeStruct((n, D_MODEL), F32),
        scratch_shapes=[pltpu.VMEM((TM_MOE, D_MODEL), F32)],
        compiler_params=_cparams(("parallel", "arbitrary")),
        name="moe",
    )(x_bf16, gates, y2d, mod, w_gate, w_up, w_down, ws_gate, ws_up, ws_down, ln_g, ln_b)


def _slab_perm():
    idx = []
    for j in range(4):
        for half in range(2):
            head = j + 4 * half
            idx.extend(range(head * HEAD_DIM, (head + 1) * HEAD_DIM))
    return np.asarray(idx, np.int32)


def _rope_tables():
    t = np.arange(DEC_SEQ)
    quarter = HEAD_DIM // 4
    inv = ROPE_THETA ** (-np.arange(quarter, dtype=np.float64) / quarter)
    ar = (t // GRID_W)[:, None] * inv
    ac = (t % GRID_W)[:, None] * inv
    ang = np.concatenate([ar, ar, ac, ac] * 2, axis=-1)
    sign = np.where((np.arange(LANES) % 32) < 16, -1.0, 1.0)
    return jnp.asarray(np.cos(ang), F32), jnp.asarray(np.sin(ang) * sign, F32)


def kernel(x_prompt, x_sample, cache_ka, cache_va, cache_kb, cache_vb, cache_kc, cache_vc, c, c_ctx, w_ada, b_ada, ln_g, ln_b, w_in_even, w_out_even, qnorm_a, knorm_a, sink_b, w_in_odd, w_out_odd, rpb_c, w_router, b_router, w_gate, w_up, w_down, ws_gate, ws_up, ws_down):
    D = D_MODEL
    y_p = x_prompt.reshape(P_TOK, D)
    y_s = x_sample.reshape(S_TOK, D)

    cond8 = jnp.concatenate([c_ctx[None, :], c, jnp.zeros((8 - 1 - DEC_BATCH, D), F32)], axis=0)
    ada = _adaln(cond8, w_ada, b_ada)
    mods = ada.reshape(DEPTH, 8, 6, 1, D).transpose(0, 2, 1, 3, 4)

    cos, sin_signed = _rope_tables()
    perm = _slab_perm()
    in_perm = np.concatenate([perm, 512 + perm, np.arange(1024, EVEN_IN)])
    out_perm = np.concatenate([perm, 512 + perm])
    ones_row = jnp.ones((1, LANES), F32)
    no_sink = jnp.zeros((N_HEADS_B,), F32)

    new = {}
    for l in range(DEPTH):
        mod = mods[l]
        if l % 2 == 0:
            e = l // 2
            w_in = w_in_even[e][:, in_perm].astype(BF16)
            w_out = w_out_even[e][out_perm, :].astype(BF16)
            qn2 = jnp.tile(qnorm_a[e], 2)[None, :]
            kn2 = jnp.tile(knorm_a[e], 2)[None, :]
            proj = functools.partial(_project, w_bf16=w_in, qn2=qn2, kn2=kn2, cos=cos, sin_signed=sin_signed,
                                     dq=1024, dk=256, dv=256, q_norm=512, k_norm=128)
            q_p, k_p, v_p = proj(y_p, mod, rope=False, latent=False)
            q_s, k_s, v_s = proj(y_s, mod, rope=True, latent=True)
            new["ka"], new["kb"] = k_p[:, :LANES], k_p[:, LANES:]
            new["va"], new["vb"] = v_p[:, :LANES], v_p[:, LANES:]
            sink = sink_b[e].astype(F32)
            groups = (tuple(((j,), 0, None) for j in range(4))
                      + tuple(((4 + j,), 1, (j, j + 4)) for j in range(4)))
            o_p = _ctx_attention(q_p, k_p, v_p, sink, groups=groups, seqs=4)

            k_s3 = k_s.reshape(DEC_BATCH, DEC_SEQ, 2 * LANES)
            v_s3 = v_s.reshape(DEC_BATCH, DEC_SEQ, 2 * LANES)
            ckb = cache_kb[:, e].reshape(DEC_BATCH, PAST_LEN, LANES).astype(BF16)
            cvb = cache_vb[:, e].reshape(DEC_BATCH, PAST_LEN, LANES).astype(BF16)
            pad = jnp.zeros((DEC_BATCH, WINDOW, LANES), BF16)
            o_a = _global_attention(q_s, cache_ka[:, e].reshape(DEC_BATCH, PAST_LEN, LANES),
                                    cache_va[:, e].reshape(DEC_BATCH, PAST_LEN, LANES), k_s3, v_s3)
            o_b = _window_attention(q_s, jnp.concatenate([ckb, k_s3[:, :, LANES:], pad], axis=1),
                                    jnp.concatenate([cvb, v_s3[:, :, LANES:], pad], axis=1), sink)
            parts_p = [(o_p, 0), (o_p, 1)]
            parts_s = [(o_a, 0), (o_b, 0)]
        else:
            o = l // 2
            w_in = w_in_odd[o].astype(BF16)
            w_out = w_out_odd[o].astype(BF16)
            proj = functools.partial(_project, w_bf16=w_in, qn2=ones_row, kn2=ones_row, cos=cos,
                                     sin_signed=sin_signed, dq=1024, dk=1024, dv=1024, q_norm=0, k_norm=0,
                                     rope=False)
            q_p, k_p, v_p = proj(y_p, mod, latent=False)
            q_s, k_s, v_s = proj(y_s, mod, latent=True)
            new["kc"], new["vc"] = k_p, v_p
            groups = tuple(((j,), j, None) for j in range(D // LANES))
            o_p = _ctx_attention(q_p, k_p, v_p, no_sink, groups=groups, seqs=4)
            kc = cache_kc[:, o].reshape(DEC_BATCH, PAST_LEN, D).astype(BF16)
            vc = cache_vc[:, o].reshape(DEC_BATCH, PAST_LEN, D).astype(BF16)
            o_s = _na_attention(q_s, k_s, v_s, kc, vc, _na_bias_table(rpb_c[o]))
            parts_p = [(o_p, 0)]
            parts_s = [(o_s, 0)]

        wr = w_router[l].T
        wr_hi = wr.astype(BF16)
        wr_lo = (wr - wr_hi.astype(F32)).astype(BF16)
        post = functools.partial(_post_mixer, mod=mod, w_out_bf16=w_out, ln_g=ln_g[l, 0][None, :],
                                 ln_b=ln_b[l, 0][None, :], wr_hi=wr_hi, wr_lo=wr_lo, b_router=b_router[l][:, None])
        y_p, h_p, g_p = post(parts_p, y_p, latent=False)
        y_s, h_s, g_s = post(parts_s, y_s, latent=True)
        moe = functools.partial(_moe, mod=mod, w_gate=w_gate, w_up=w_up, w_down=w_down,
                                ws_gate=ws_gate, ws_up=ws_up, ws_down=ws_down,
                                ln_g=ln_g[l, 1][None, :], ln_b=ln_b[l, 1][None, :], layer=l)
        y_p = moe(h_p, g_p, y_p, latent=False)
        y_s = moe(h_s, g_s, y_s, latent=True)

    kv_a = (BATCH, 1, SEQ, N_KV_A, HEAD_DIM)
    kv_c = (BATCH, 1, SEQ, N_HEADS_C, HEAD_DIM)
    return (y_p.reshape(BATCH, SEQ, D), y_s.reshape(DEC_BATCH, DEC_SEQ, D),
            new["ka"].reshape(kv_a), new["va"].reshape(kv_a), new["kb"].reshape(kv_a), new["vb"].reshape(kv_a),
            new["kc"].reshape(kv_c), new["vc"].reshape(kv_c))
```

```python
import functools
import math

import numpy as np
import jax
import jax.numpy as jnp
from jax import lax
from jax.experimental import pallas as pl
from jax.experimental.pallas import tpu as pltpu

F32 = jnp.float32
BF16 = jnp.bfloat16

D_MODEL = 1024
BATCH = 16
SEQ = 256
DEPTH = 2
DEC_BATCH = 2
DEC_SEQ = 4096
PAST_LEN = 512
GRID_W = 64
HEAD_DIM = 64
N_HEADS_A = 8
N_KV_A = 2
N_HEADS_B = 8
N_KV_B = 2
N_HEADS_C = 16
EVEN_IN = 1536
ODD_IN = 3072
WINDOW = 128
NA_ROWS = 8
NA_COLS = 16
ROPE_THETA = 10000.0
N_EXPERTS = 64
TOP_K = 8
D_EXPERT = 128
ROUTED_SCALE = 2.5
ALPHA = (2 * DEPTH) ** 0.25
LN_EPS = 1e-6
RMS_EPS = 1e-6
NEG_BIG = -1e30
LOG2E = math.log2(math.e)

LANES = 128
P_TOK = BATCH * SEQ
S_TOK = DEC_BATCH * DEC_SEQ
TM = 512
TM_POST = 1024
TM_MOE = 1024
EXPERTS_PER_STEP = 8
NA_QROWS = 4
NA_KROWS = 12
NA_SLABS_PER_STEP = 8
WINDOW_BLOCKS_PER_STEP = 4
VMEM_LIMIT = 56 * 1024 * 1024

_NT = (((1,), (1,)), ((), ()))


def _cparams(sem):
    return pltpu.CompilerParams(dimension_semantics=sem, vmem_limit_bytes=VMEM_LIMIT)


def _half_masks(dtype):
    lane = lax.broadcasted_iota(jnp.int32, (1, LANES), 1)
    lo = jnp.where(lane < HEAD_DIM, 1.0, 0.0).astype(dtype)
    hi = jnp.where(lane < HEAD_DIM, 0.0, 1.0).astype(dtype)
    return lo, hi


def _ada_kernel(c_ref, w_ref, b_ref, o_ref):
    c = c_ref[...]
    a = c / (1.0 + jnp.exp(-c))
    w = w_ref[0]
    a_hi = a.astype(BF16)
    a_lo = (a - a_hi.astype(F32)).astype(BF16)
    w_hi = w.astype(BF16)
    w_lo = (w - w_hi.astype(F32)).astype(BF16)
    acc = jnp.dot(a_hi, w_hi, preferred_element_type=F32)
    acc += jnp.dot(a_lo, w_hi, preferred_element_type=F32)
    acc += jnp.dot(a_hi, w_lo, preferred_element_type=F32)
    o_ref[0] = acc + b_ref[0]


def _adaln(cond8, w_ada, b_ada):
    tn = 1536
    return pl.pallas_call(
        _ada_kernel,
        grid=(DEPTH, 6 * D_MODEL // tn),
        in_specs=[
            pl.BlockSpec((8, D_MODEL), lambda l, j: (0, 0)),
            pl.BlockSpec((1, D_MODEL, tn), lambda l, j: (l, 0, j)),
            pl.BlockSpec((1, 1, tn), lambda l, j: (l, 0, j)),
        ],
        out_specs=pl.BlockSpec((1, 8, tn), lambda l, j: (l, 0, j)),
        out_shape=jax.ShapeDtypeStruct((DEPTH, 8, 6 * D_MODEL), F32),
        compiler_params=_cparams(("parallel", "parallel")),
        name="adaln",
    )(cond8, w_ada, b_ada.reshape(DEPTH, 1, 6 * D_MODEL))


def _group_sum_matrix():
    r = (lax.broadcasted_iota(jnp.int32, (2 * LANES, LANES), 0) % LANES) // HEAD_DIM
    c = lax.broadcasted_iota(jnp.int32, (2 * LANES, LANES), 1) // HEAD_DIM
    return jnp.where(r == c, 1.0, 0.0).astype(BF16)


def _rms_slab(t, g, gmat):
    sq = t * t
    hi = sq.astype(BF16)
    lo = (sq - hi.astype(F32)).astype(BF16)
    ss = jnp.dot(jnp.concatenate([hi, lo], axis=1), gmat, preferred_element_type=F32)
    return t * lax.rsqrt(ss * (1.0 / HEAD_DIM) + RMS_EPS) * g


def _rope_slab(t, cos, sin_signed, first):
    r = jnp.where(first, pltpu.roll(t, LANES - 16, 1), pltpu.roll(t, 16, 1))
    return t * cos + r * sin_signed


def _proj_kernel(x_ref, mod_ref, w_ref, qn_ref, kn_ref, cos_ref, sin_ref, q_ref, k_ref, v_ref,
                 *, dq, dk, q_norm, k_norm, rope):
    x = x_ref[...]
    h = (x * (1.0 + mod_ref[1]) + mod_ref[0]).astype(BF16)
    y = jnp.dot(h, w_ref[...], preferred_element_type=F32)
    tm = x.shape[0]
    gmat = _group_sum_matrix() if (q_norm or k_norm) else None
    if rope:
        cos = cos_ref[...]
        sin_signed = sin_ref[...]
        first = (lax.broadcasted_iota(jnp.int32, (tm, LANES), 1) % 32) < 16
    for s in range(dq // LANES):
        t = y[:, s * LANES:(s + 1) * LANES]
        if s * LANES < q_norm:
            t = _rms_slab(t, qn_ref[...], gmat)
        if rope:
            t = _rope_slab(t, cos, sin_signed, first)
        q_ref[:, s * LANES:(s + 1) * LANES] = (t * (HEAD_DIM ** -0.5 * LOG2E)).astype(q_ref.dtype)
    for s in range(dk // LANES):
        t = y[:, dq + s * LANES:dq + (s + 1) * LANES]
        if s * LANES < k_norm:
            t = _rms_slab(t, kn_ref[...], gmat)
        if rope:
            t = _rope_slab(t, cos, sin_signed, first)
        k_ref[:, s * LANES:(s + 1) * LANES] = t.astype(k_ref.dtype)
    v_ref[...] = y[:, dq + dk:].astype(v_ref.dtype)


def _project(x2d, mod, w_bf16, qn2, kn2, cos, sin_signed, *, dq, dk, dv, q_norm, k_norm, rope, latent):
    n = x2d.shape[0]
    per_batch = DEC_SEQ // TM
    if latent:
        mod_map = lambda i: (0, 1 + i // per_batch, 0, 0)
        pos_map = lambda i: (i % per_batch, 0)
        kv_dtype = BF16
    else:
        mod_map = lambda i: (0, 0, 0, 0)
        pos_map = lambda i: (0, 0)
        kv_dtype = F32
    kern = functools.partial(_proj_kernel, dq=dq, dk=dk, q_norm=q_norm, k_norm=k_norm, rope=rope)
    return pl.pallas_call(
        kern,
        grid=(n // TM,),
        in_specs=[
            pl.BlockSpec((TM, D_MODEL), lambda i: (i, 0)),
            pl.BlockSpec((6, None, 1, D_MODEL), mod_map),
            pl.BlockSpec((D_MODEL, dq + dk + dv), lambda i: (0, 0)),
            pl.BlockSpec((1, LANES), lambda i: (0, 0)),
            pl.BlockSpec((1, LANES), lambda i: (0, 0)),
            pl.BlockSpec((TM, LANES), pos_map),
            pl.BlockSpec((TM, LANES), pos_map),
        ],
        out_specs=[
            pl.BlockSpec((TM, dq), lambda i: (i, 0)),
            pl.BlockSpec((TM, dk), lambda i: (i, 0)),
            pl.BlockSpec((TM, dv), lambda i: (i, 0)),
        ],
        out_shape=[
            jax.ShapeDtypeStruct((n, dq), BF16),
            jax.ShapeDtypeStruct((n, dk), kv_dtype),
            jax.ShapeDtypeStruct((n, dv), kv_dtype),
        ],
        compiler_params=_cparams(("parallel",)),
        name="in_proj",
    )(x2d, mod, w_bf16, qn2, kn2, cos, sin_signed)


def _stack_heads(q_ref, slabs, lo, hi):
    qs = [q_ref[:, j * LANES:(j + 1) * LANES] for j in slabs]
    return jnp.concatenate([q * lo for q in qs] + [q * hi for q in qs], axis=0)


def _tile_lanes(x, n):
    return jnp.concatenate([x] * n, axis=1)


def _pv_with_denominator(p, v, lo, hi):
    half = p.shape[0] // 2
    pv_lo = jnp.dot(p[:half], v * lo + hi, preferred_element_type=F32)
    pv_hi = jnp.dot(p[half:], v * hi + lo, preferred_element_type=F32)
    return jnp.concatenate([pv_lo, pv_hi], axis=0)


def _normalize_store(o, o_ref, slabs, tq, lane_lo):
    half = len(slabs) * tq
    for n, j in enumerate(slabs):
        o_lo = o[n * tq:(n + 1) * tq]
        o_hi = o[half + n * tq:half + (n + 1) * tq]
        o_lo = o_lo * (1.0 / pltpu.roll(o_lo, HEAD_DIM, 1))
        o_hi = o_hi * (1.0 / pltpu.roll(o_hi, HEAD_DIM, 1))
        o_ref[:, j * LANES:(j + 1) * LANES] = jnp.where(lane_lo, o_lo, o_hi).astype(o_ref.dtype)


def _sink_rows(sink_ref, heads, tq):
    return jnp.concatenate([jnp.full((tq, LANES), sink_ref[h] * LOG2E, F32) for h in heads], axis=0)


def _software_pipeline(units, stages):
    live = [dict() for _ in stages]
    for t in range(len(units) + len(stages) - 1):
        for k, stage in enumerate(stages):
            u = t - k
            if 0 <= u < len(units):
                live[k][u] = stage(units[u], live[k - 1].pop(u) if k else None)


def _softmax_pv(parts, sink, lo, hi):
    rows = parts[0][0].shape[0]
    m = jnp.full((rows, LANES), NEG_BIG, F32) if sink is None else sink
    for s, _ in parts:
        m = jnp.maximum(m, s.max(axis=1, keepdims=True))
    o = None
    for s, v in parts:
        p = jnp.exp2(s - _tile_lanes(m, s.shape[1] // LANES)).astype(BF16)
        pv = _pv_with_denominator(p, v, lo, hi)
        o = pv if o is None else o + pv
    if sink is not None:
        lo32, hi32 = _half_masks(F32)
        e = jnp.exp2(sink - m)
        half = rows // 2
        o = o + jnp.concatenate([e[:half] * hi32, e[half:] * lo32], axis=0)
    return o


def _ctx_attn_kernel(sink_ref, q_ref, k_ref, v_ref, o_ref, *, groups):
    lo, hi = _half_masks(BF16)
    lo32, hi32 = _half_masks(F32)
    lane_lo = lax.broadcasted_iota(jnp.int32, (SEQ, LANES), 1) < HEAD_DIM
    units = [(b, grp) for b in range(q_ref.shape[0] // SEQ) for grp in groups]

    def scores(unit):
        b, (slabs, ks, _) = unit
        kk = k_ref[b * SEQ:(b + 1) * SEQ, ks * LANES:(ks + 1) * LANES].astype(BF16)
        return lax.dot_general(_stack_heads(q_ref.at[b * SEQ:(b + 1) * SEQ], slabs, lo, hi), kk, _NT,
                               preferred_element_type=F32)

    def row_max(unit, s):
        sink_heads = unit[1][2]
        if sink_heads is None:
            return s, jnp.maximum(jnp.full((s.shape[0], LANES), NEG_BIG, F32), s.max(axis=1, keepdims=True)), None
        sink = _sink_rows(sink_ref, sink_heads, SEQ)
        return s, jnp.maximum(sink, s.max(axis=1, keepdims=True)), sink

    def probs(unit, state):
        s, m, sink = state
        extra = None
        if sink is not None:
            e = jnp.exp2(sink - m)
            half = s.shape[0] // 2
            extra = jnp.concatenate([e[:half] * hi32, e[half:] * lo32], axis=0)
        return jnp.exp2(s - _tile_lanes(m, s.shape[1] // LANES)).astype(BF16), extra

    def weighted_sum(unit, state):
        p, extra = state
        b, (slabs, ks, _) = unit
        vv = v_ref[b * SEQ:(b + 1) * SEQ, ks * LANES:(ks + 1) * LANES].astype(BF16)
        o = _pv_with_denominator(p, vv, lo, hi)
        return o if extra is None else o + extra

    def finish(unit, o):
        b, (slabs, _, _) = unit
        _normalize_store(o, o_ref.at[b * SEQ:(b + 1) * SEQ], slabs, SEQ, lane_lo)

    _software_pipeline(units, [lambda u, _: scores(u), row_max, probs, weighted_sum, finish])


def _ctx_attention(q, k, v, sink, *, groups, seqs):
    kw = k.shape[1]
    rows = seqs * SEQ
    return pl.pallas_call(
        functools.partial(_ctx_attn_kernel, groups=groups),
        grid=(BATCH // seqs,),
        in_specs=[
            pl.BlockSpec(memory_space=pltpu.SMEM),
            pl.BlockSpec((rows, D_MODEL), lambda b: (b, 0)),
            pl.BlockSpec((rows, kw), lambda b: (b, 0)),
            pl.BlockSpec((rows, kw), lambda b: (b, 0)),
        ],
        out_specs=pl.BlockSpec((rows, D_MODEL), lambda b: (b, 0)),
        out_shape=jax.ShapeDtypeStruct((P_TOK, D_MODEL), BF16),
        compiler_params=_cparams(("parallel",)),
        name="ctx_attn",
    )(sink, q, k, v)


def _global_attn_kernel(q_ref, k_ref, v_ref, o_ref, s0, s1, p0, p1, a0, a1, m_ref, acc_ref, *, tk):
    tq = q_ref.shape[0]
    slabs = tuple(range(q_ref.shape[1] // LANES))
    nk = k_ref.shape[0] // tk
    lo, hi = _half_masks(BF16)
    lane_lo = lax.broadcasted_iota(jnp.int32, (tq, LANES), 1) < HEAD_DIM
    qstack = _stack_heads(q_ref, slabs, lo, hi)
    m_ref[...] = jnp.full(m_ref.shape, NEG_BIG, F32)
    acc_ref[...] = jnp.zeros(acc_ref.shape, F32)
    s_bufs, p_bufs, a_bufs = (s0, s1), (p0, p1), (a0, a1)

    def stage_a(c):
        s_bufs[c % 2][...] = lax.dot_general(qstack, k_ref[c * tk:(c + 1) * tk, :], _NT, preferred_element_type=F32)

    def stage_b(c):
        s = s_bufs[c % 2][...]
        m_old = m_ref[...]
        m_new = jnp.maximum(m_old, s.max(axis=1, keepdims=True))
        a_bufs[c % 2][...] = jnp.exp2(m_old - m_new)
        p_bufs[c % 2][...] = jnp.exp2(s - _tile_lanes(m_new, tk // LANES)).astype(BF16)
        m_ref[...] = m_new

    def stage_c(c):
        pv = _pv_with_denominator(p_bufs[c % 2][...], v_ref[c * tk:(c + 1) * tk, :], lo, hi)
        acc_ref[...] = a_bufs[c % 2][...] * acc_ref[...] + pv

    for t in range(nk + 2):
        if t < nk:
            stage_a(t)
        if 1 <= t <= nk:
            stage_b(t - 1)
        if t >= 2:
            stage_c(t - 2)
    _normalize_store(acc_ref[...], o_ref, slabs, tq, lane_lo)


def _global_attention(q, kcat, vcat):
    tq, tk = 256, 512
    t = kcat.shape[1]
    per_batch = DEC_SEQ // tq
    rows = N_HEADS_A * tq
    kv_spec = pl.BlockSpec((None, t, LANES), lambda b, i: (b, 0, 0))
    return pl.pallas_call(
        functools.partial(_global_attn_kernel, tk=tk),
        grid=(DEC_BATCH, per_batch),
        in_specs=[pl.BlockSpec((tq, 4 * LANES), lambda b, i: (b * per_batch + i, 0)), kv_spec, kv_spec],
        out_specs=pl.BlockSpec((tq, 4 * LANES), lambda b, i: (b * per_batch + i, 0)),
        out_shape=jax.ShapeDtypeStruct((S_TOK, 4 * LANES), BF16),
        scratch_shapes=[pltpu.VMEM((rows, tk), F32), pltpu.VMEM((rows, tk), F32),
                        pltpu.VMEM((rows, tk), BF16), pltpu.VMEM((rows, tk), BF16),
                        pltpu.VMEM((rows, LANES), F32), pltpu.VMEM((rows, LANES), F32),
                        pltpu.VMEM((rows, LANES), F32), pltpu.VMEM((rows, LANES), F32)],
        compiler_params=_cparams(("parallel", "parallel")),
        name="global_attn",
    )(q, kcat, vcat)


def _window_attn_kernel(sink_ref, q_ref, k_ref, v_ref, o_ref):
    tq = WINDOW
    n_blocks = q_ref.shape[0] // tq
    slabs = tuple(range(q_ref.shape[1] // LANES))
    n_heads = 2 * len(slabs)
    span = tq + 2 * WINDOW
    lo, hi = _half_masks(BF16)
    lo32, hi32 = _half_masks(F32)
    lane_lo = lax.broadcasted_iota(jnp.int32, (tq, LANES), 1) < HEAD_DIM
    row = lax.broadcasted_iota(jnp.int32, (tq, span), 0)
    col = lax.broadcasted_iota(jnp.int32, (tq, span), 1)
    band = (col >= row) & (col <= row + 2 * WINDOW)
    sink = _sink_rows(sink_ref, range(n_heads), tq)

    def local_offset(u):
        i = pl.program_id(1) * n_blocks + u
        return i, pl.multiple_of(PAST_LEN - WINDOW + i * tq, LANES)

    def scores(u):
        i, off = local_offset(u)
        kpos = i * tq - WINDOW + col
        valid = band & (kpos >= 0) & (kpos < DEC_SEQ)
        qstack = _stack_heads(q_ref.at[u * tq:(u + 1) * tq], slabs, lo, hi)
        s_loc = lax.dot_general(qstack, k_ref[pl.ds(off, span), :], _NT, preferred_element_type=F32)
        s_loc = jnp.where(valid[None], s_loc.reshape(n_heads, tq, span), NEG_BIG).reshape(n_heads * tq, span)
        return [s_loc, lax.dot_general(qstack, k_ref[0:PAST_LEN, :], _NT, preferred_element_type=F32)]

    def row_max(u, s_parts):
        m = sink
        for s in s_parts:
            m = jnp.maximum(m, s.max(axis=1, keepdims=True))
        return s_parts, m

    def probs(u, state):
        s_parts, m = state
        e = jnp.exp2(sink - m)
        half = e.shape[0] // 2
        extra = jnp.concatenate([e[:half] * hi32, e[half:] * lo32], axis=0)
        return [jnp.exp2(s - _tile_lanes(m, s.shape[1] // LANES)).astype(BF16) for s in s_parts], extra

    def weighted_sum(u, state):
        p_parts, extra = state
        _, off = local_offset(u)
        return (_pv_with_denominator(p_parts[0], v_ref[pl.ds(off, span), :], lo, hi)
                + _pv_with_denominator(p_parts[1], v_ref[0:PAST_LEN, :], lo, hi) + extra)

    def finish(u, o):
        _normalize_store(o, o_ref.at[u * tq:(u + 1) * tq], slabs, tq, lane_lo)

    _software_pipeline(list(range(n_blocks)), [
        lambda u, _: scores(u),
        lambda u, s_parts: probs(u, row_max(u, s_parts)),
        lambda u, state: finish(u, weighted_sum(u, state))])


def _window_attention(q, kcat, vcat, sink):
    tq = WINDOW_BLOCKS_PER_STEP * WINDOW
    t = kcat.shape[1]
    per_batch = DEC_SEQ // tq
    return pl.pallas_call(
        _window_attn_kernel,
        grid=(DEC_BATCH, per_batch),
        in_specs=[
            pl.BlockSpec(memory_space=pltpu.SMEM),
            pl.BlockSpec((tq, 4 * LANES), lambda b, i: (b * per_batch + i, 1)),
            pl.BlockSpec((None, t, LANES), lambda b, i: (b, 0, 0)),
            pl.BlockSpec((None, t, LANES), lambda b, i: (b, 0, 0)),
        ],
        out_specs=pl.BlockSpec((tq, 4 * LANES), lambda b, i: (b * per_batch + i, 0)),
        out_shape=jax.ShapeDtypeStruct((S_TOK, 4 * LANES), BF16),
        compiler_params=_cparams(("parallel", "parallel")),
        name="window_attn",
    )(sink, q, kcat, vcat)


def _na_start_row(g):
    return jnp.clip(NA_QROWS * g - NA_ROWS // 2, 0, DEC_SEQ // GRID_W - NA_KROWS)


def _na_step_bias(t_ref, half, g, lane_lo64):
    rows = DEC_SEQ // GRID_W
    start = _na_start_row(g)
    d0 = start - NA_QROWS * g + NA_ROWS - 1
    row_blocks = []
    for qr in range(NA_QROWS):
        rs = jnp.clip(NA_QROWS * g + qr - NA_ROWS // 2, 0, rows - NA_ROWS)
        blocks = []
        for m in range(NA_KROWS // 2):
            tiles = []
            for kr in (2 * m, 2 * m + 1):
                krow = start + kr
                inside = (krow >= rs) & (krow < rs + NA_ROWS)
                dr = jnp.clip(kr - qr + d0, 0, 2 * NA_ROWS - 2)
                tiles.append(t_ref[half, dr] + jnp.where(inside, 0.0, NEG_BIG))
            blocks.append(jnp.where(lane_lo64, tiles[0], tiles[1]))
        row_blocks.append(jnp.concatenate(blocks, axis=1))
    return jnp.concatenate(row_blocks, axis=0)


def _na_attn_kernel(q_ref, k_ref, v_ref, kc_ref, vc_ref, t_ref, o_ref):
    g = pl.program_id(2)
    tq = q_ref.shape[0]
    lo, hi = _half_masks(BF16)
    lane_lo = lax.broadcasted_iota(jnp.int32, (tq, LANES), 1) < HEAD_DIM
    lane_lo64 = lax.broadcasted_iota(jnp.int32, (GRID_W, LANES), 1) < HEAD_DIM
    off = pl.multiple_of(_na_start_row(g) * GRID_W, GRID_W)
    n_slabs = q_ref.shape[1] // LANES
    cols = [slice(n * LANES, (n + 1) * LANES) for n in range(n_slabs)]

    def scores(n):
        qstack = _stack_heads(q_ref, (n,), lo, hi)
        bias = jnp.concatenate([_na_step_bias(t_ref, 2 * n + half, g, lane_lo64) for half in range(2)], axis=0)
        k_loc = k_ref[pl.ds(off, NA_KROWS * GRID_W), cols[n]]
        s_loc = lax.dot_general(qstack, k_loc, _NT, preferred_element_type=F32) + bias
        return [s_loc, lax.dot_general(qstack, kc_ref[:, cols[n]], _NT, preferred_element_type=F32)]

    def row_max(n, s_parts):
        m = jnp.full((2 * tq, LANES), NEG_BIG, F32)
        for s in s_parts:
            m = jnp.maximum(m, s.max(axis=1, keepdims=True))
        return s_parts, m

    def probs(n, state):
        s_parts, m = state
        return [jnp.exp2(s - _tile_lanes(m, s.shape[1] // LANES)).astype(BF16) for s in s_parts]

    def weighted_sum(n, p_parts):
        v_loc = v_ref[pl.ds(off, NA_KROWS * GRID_W), cols[n]]
        return (_pv_with_denominator(p_parts[0], v_loc, lo, hi)
                + _pv_with_denominator(p_parts[1], vc_ref[:, cols[n]], lo, hi))

    def finish(n, o):
        _normalize_store(o, o_ref, (n,), tq, lane_lo)

    _software_pipeline(list(range(n_slabs)), [
        lambda n, _: scores(n),
        lambda n, s_parts: probs(n, row_max(n, s_parts)),
        lambda n, p_parts: finish(n, weighted_sum(n, p_parts))])


def _na_bias_table(rpb):
    cols = np.arange(GRID_W)
    cs = np.clip(cols - NA_COLS // 2, 0, GRID_W - NA_COLS)
    dc = cols[None, :] - cols[:, None] + NA_COLS - 1
    inside = (cols[None, :] >= cs[:, None]) & (cols[None, :] < cs[:, None] + NA_COLS)
    onehot = (np.arange(2 * NA_COLS - 1)[:, None, None] == dc[None]) & inside[None]
    t = jnp.einsum("hdc,cqk->hdqk", rpb.astype(F32), jnp.asarray(onehot, F32), precision=lax.Precision.HIGHEST)
    t = jnp.where(inside[None, None], t * LOG2E, NEG_BIG)
    return jnp.concatenate([t, t], axis=-1)


def _na_attention(q, k, v, kc, vc, bias):
    tq = NA_QROWS * GRID_W
    n_groups = DEC_SEQ // tq
    w = NA_SLABS_PER_STEP * LANES
    k3 = k.reshape(DEC_BATCH, DEC_SEQ, D_MODEL)
    v3 = v.reshape(DEC_BATCH, DEC_SEQ, D_MODEL)
    tok = lambda s, b, g: (b * n_groups + g, s)
    per_batch = lambda s, b, g: (b, 0, s)
    once = pl.Buffered(1)
    return pl.pallas_call(
        _na_attn_kernel,
        grid=(D_MODEL // w, DEC_BATCH, n_groups),
        in_specs=[
            pl.BlockSpec((tq, w), tok),
            pl.BlockSpec((None, DEC_SEQ, w), per_batch, pipeline_mode=once),
            pl.BlockSpec((None, DEC_SEQ, w), per_batch, pipeline_mode=once),
            pl.BlockSpec((None, PAST_LEN, w), per_batch, pipeline_mode=once),
            pl.BlockSpec((None, PAST_LEN, w), per_batch, pipeline_mode=once),
            pl.BlockSpec((2 * NA_SLABS_PER_STEP, 2 * NA_ROWS - 1, GRID_W, LANES), lambda s, b, g: (s, 0, 0, 0),
                         pipeline_mode=once),
        ],
        out_specs=pl.BlockSpec((tq, w), tok),
        out_shape=jax.ShapeDtypeStruct((S_TOK, D_MODEL), BF16),
        compiler_params=_cparams(("parallel", "parallel", "parallel")),
        name="na_attn",
    )(q, k3, v3, kc, vc, bias)


def _layer_norm(z, g, b):
    mu = jnp.mean(z, axis=-1, keepdims=True)
    zc = z - mu
    var = jnp.mean(zc * zc, axis=-1, keepdims=True)
    return zc * lax.rsqrt(var + LN_EPS) * g + b


def _post_kernel(*refs, n_parts):
    o_refs = refs[:n_parts]
    (y_ref, mod_ref, w_ref, lng_ref, lnb_ref, wr_hi_ref, wr_lo_ref, br_ref,
     y_out_ref, h_out_ref, gates_ref) = refs[n_parts:]
    pw = D_MODEL // n_parts
    mix = None
    for p, o_ref in enumerate(o_refs):
        part = jnp.dot(o_ref[...], w_ref[p * pw:(p + 1) * pw, :], preferred_element_type=F32)
        mix = part if mix is None else mix + part
    y = _layer_norm(ALPHA * y_ref[...] + mod_ref[2] * mix, lng_ref[...], lnb_ref[...])
    y_out_ref[...] = y
    h = y * (1.0 + mod_ref[4]) + mod_ref[3]
    h_hi = h.astype(BF16)
    h_out_ref[...] = h_hi
    h_lo = (h - h_hi.astype(F32)).astype(BF16)
    logits = lax.dot_general(wr_hi_ref[...], h_hi, _NT, preferred_element_type=F32)
    logits += lax.dot_general(wr_hi_ref[...], h_lo, _NT, preferred_element_type=F32)
    logits += lax.dot_general(wr_lo_ref[...], h_hi, _NT, preferred_element_type=F32)
    scores = 1.0 / (1.0 + jnp.exp(-logits))
    work = scores + br_ref[...]
    expert = lax.broadcasted_iota(jnp.int32, work.shape, 0).astype(F32)
    chosen = jnp.zeros(work.shape, F32)
    for _ in range(TOP_K):
        mx = work.max(axis=0, keepdims=True)
        first = jnp.where(work == mx, expert, float(N_EXPERTS)).min(axis=0, keepdims=True)
        pick = expert == first
        chosen = jnp.where(pick, scores, chosen)
        work = jnp.where(pick, NEG_BIG, work)
    gates_t = chosen / chosen.sum(axis=0, keepdims=True) * ROUTED_SCALE
    padded = jnp.concatenate([gates_t, jnp.zeros((LANES - N_EXPERTS, gates_t.shape[1]), F32)], axis=0)
    gates_ref[...] = padded.T[:, :N_EXPERTS]


def _post_mixer(o_parts, y2d, mod, w_out_bf16, ln_g, ln_b, wr_hi, wr_lo, b_router, *, latent):
    n = y2d.shape[0]
    n_parts = len(o_parts)
    pw = D_MODEL // n_parts
    TM = TM_POST
    per_batch = DEC_SEQ // TM
    mod_map = (lambda i: (0, 1 + i // per_batch, 0, 0)) if latent else (lambda i: (0, 0, 0, 0))
    o_specs = [pl.BlockSpec((TM, pw), functools.partial(lambda i, c: (i, c), c=col)) for _, col in o_parts]
    full = lambda i: (0, 0)
    return pl.pallas_call(
        functools.partial(_post_kernel, n_parts=n_parts),
        grid=(n // TM,),
        in_specs=o_specs + [
            pl.BlockSpec((TM, D_MODEL), lambda i: (i, 0)),
            pl.BlockSpec((6, None, 1, D_MODEL), mod_map),
            pl.BlockSpec((D_MODEL, D_MODEL), full, pipeline_mode=pl.Buffered(1)),
            pl.BlockSpec((1, D_MODEL), full),
            pl.BlockSpec((1, D_MODEL), full),
            pl.BlockSpec((N_EXPERTS, D_MODEL), full),
            pl.BlockSpec((N_EXPERTS, D_MODEL), full),
            pl.BlockSpec((N_EXPERTS, 1), full),
        ],
        out_specs=[
            pl.BlockSpec((TM, D_MODEL), lambda i: (i, 0)),
            pl.BlockSpec((TM, D_MODEL), lambda i: (i, 0)),
            pl.BlockSpec((TM, N_EXPERTS), lambda i: (i, 0)),
        ],
        out_shape=[
            jax.ShapeDtypeStruct((n, D_MODEL), F32),
            jax.ShapeDtypeStruct((n, D_MODEL), BF16),
            jax.ShapeDtypeStruct((n, N_EXPERTS), F32),
        ],
        compiler_params=_cparams(("parallel",)),
        name="post_mixer",
    )(*[a for a, _ in o_parts], y2d, mod, w_out_bf16, ln_g, ln_b, wr_hi, wr_lo, b_router)


def _silu(x):
    return x / (1.0 + jnp.exp(-x))


def _moe_kernel(x_ref, gates_ref, y_ref, mod_ref, wg_ref, wu_ref, wd_ref, sg_ref, su_ref, sd_ref,
                lng_ref, lnb_ref, o_ref, acc_ref):
    e = pl.program_id(1)
    x = x_ref[...]

    @pl.when(e == 0)
    def _():
        a = _silu(jnp.dot(x, sg_ref[...].astype(BF16), preferred_element_type=F32))
        a = a * jnp.dot(x, su_ref[...].astype(BF16), preferred_element_type=F32)
        acc_ref[...] = jnp.dot(a.astype(BF16), sd_ref[...].astype(BF16), preferred_element_type=F32)

    gates = gates_ref[...]
    lane = lax.broadcasted_iota(jnp.int32, gates.shape, 1)
    acts = []
    for j in range(EXPERTS_PER_STEP):
        w1 = jnp.concatenate([wg_ref[j].astype(BF16), wu_ref[j].astype(BF16)], axis=1)
        hcat = jnp.dot(x, w1, preferred_element_type=F32)
        ge = jnp.sum(jnp.where(lane == e * EXPERTS_PER_STEP + j, gates, 0.0), axis=1, keepdims=True)
        acts.append((_silu(hcat[:, :D_EXPERT]) * hcat[:, D_EXPERT:] * ge).astype(BF16))
    w2 = jnp.concatenate([wd_ref[j].astype(BF16) for j in range(EXPERTS_PER_STEP)], axis=0)
    acc_ref[...] += jnp.dot(jnp.concatenate(acts, axis=1), w2, preferred_element_type=F32)

    @pl.when(e == pl.num_programs(1) - 1)
    def _():
        o_ref[...] = _layer_norm(ALPHA * y_ref[...] + mod_ref[5] * acc_ref[...], lng_ref[...], lnb_ref[...])


def _moe(x_bf16, gates, y2d, mod, w_gate, w_up, w_down, ws_gate, ws_up, ws_down, ln_g, ln_b, *, layer, latent):
    n = x_bf16.shape[0]
    per_batch = DEC_SEQ // TM_MOE
    mod_map = (lambda i, e: (0, 1 + i // per_batch, 0, 0)) if latent else (lambda i, e: (0, 0, 0, 0))
    ep = EXPERTS_PER_STEP
    tok = lambda i, e: (i, 0)
    full = lambda i, e: (0, 0)
    routed = lambda i, e: (layer, e, 0, 0)
    shared = lambda i, e: (layer, 0, 0)
    return pl.pallas_call(
        _moe_kernel,
        grid=(n // TM_MOE, N_EXPERTS // ep),
        in_specs=[
            pl.BlockSpec((TM_MOE, D_MODEL), tok),
            pl.BlockSpec((TM_MOE, N_EXPERTS), tok),
            pl.BlockSpec((TM_MOE, D_MODEL), tok, pipeline_mode=pl.Buffered(1)),
            pl.BlockSpec((6, None, 1, D_MODEL), mod_map),
            pl.BlockSpec((None, ep, D_MODEL, D_EXPERT), routed),
            pl.BlockSpec((None, ep, D_MODEL, D_EXPERT), routed),
            pl.BlockSpec((None, ep, D_EXPERT, D_MODEL), routed),
            pl.BlockSpec((None, D_MODEL, D_EXPERT), shared),
            pl.BlockSpec((None, D_MODEL, D_EXPERT), shared),
            pl.BlockSpec((None, D_EXPERT, D_MODEL), shared),
            pl.BlockSpec((1, D_MODEL), full),
            pl.BlockSpec((1, D_MODEL), full),
        ],
        out_specs=pl.BlockSpec((TM_MOE, D_MODEL), tok),
        out_shape=jax.ShapeDtypeStruct((n, D_MODEL), F32),
        scratch_shapes=[pltpu.VMEM((TM_MOE, D_MODEL), F32)],
        compiler_params=_cparams(("parallel", "arbitrary")),
        name="moe",
    )(x_bf16, gates, y2d, mod, w_gate, w_up, w_down, ws_gate, ws_up, ws_down, ln_g, ln_b)


def _slab_perm():
    idx = []
    for j in range(4):
        for half in range(2):
            head = j + 4 * half
            idx.extend(range(head * HEAD_DIM, (head + 1) * HEAD_DIM))
    return np.asarray(idx, np.int32)


def _rope_tables():
    t = np.arange(DEC_SEQ)
    quarter = HEAD_DIM // 4
    inv = ROPE_THETA ** (-np.arange(quarter, dtype=np.float64) / quarter)
    ar = (t // GRID_W)[:, None] * inv
    ac = (t % GRID_W)[:, None] * inv
    ang = np.concatenate([ar, ar, ac, ac] * 2, axis=-1)
    sign = np.where((np.arange(LANES) % 32) < 16, -1.0, 1.0)
    return jnp.asarray(np.cos(ang), F32), jnp.asarray(np.sin(ang) * sign, F32)


def kernel(x_prompt, x_sample, cache_ka, cache_va, cache_kb, cache_vb, cache_kc, cache_vc, c, c_ctx, w_ada, b_ada, ln_g, ln_b, w_in_even, w_out_even, qnorm_a, knorm_a, sink_b, w_in_odd, w_out_odd, rpb_c, w_router, b_router, w_gate, w_up, w_down, ws_gate, ws_up, ws_down):
    D = D_MODEL
    y_p = x_prompt.reshape(P_TOK, D)
    y_s = x_sample.reshape(S_TOK, D)

    cond8 = jnp.concatenate([c_ctx[None, :], c, jnp.zeros((8 - 1 - DEC_BATCH, D), F32)], axis=0)
    ada = _adaln(cond8, w_ada, b_ada)
    mods = ada.reshape(DEPTH, 8, 6, 1, D).transpose(0, 2, 1, 3, 4)

    cos, sin_signed = _rope_tables()
    perm = _slab_perm()
    in_perm = np.concatenate([perm, 512 + perm, np.arange(1024, EVEN_IN)])
    out_perm = np.concatenate([perm, 512 + perm])
    ones_row = jnp.ones((1, LANES), F32)
    no_sink = jnp.zeros((N_HEADS_B,), F32)

    new = {}
    for l in range(DEPTH):
        mod = mods[l]
        if l % 2 == 0:
            e = l // 2
            w_in = w_in_even[e][:, in_perm].astype(BF16)
            w_out = w_out_even[e][out_perm, :].astype(BF16)
            qn2 = jnp.tile(qnorm_a[e], 2)[None, :]
            kn2 = jnp.tile(knorm_a[e], 2)[None, :]
            proj = functools.partial(_project, w_bf16=w_in, qn2=qn2, kn2=kn2, cos=cos, sin_signed=sin_signed,
                                     dq=1024, dk=256, dv=256, q_norm=512, k_norm=128)
            q_p, k_p, v_p = proj(y_p, mod, rope=False, latent=False)
            q_s, k_s, v_s = proj(y_s, mod, rope=True, latent=True)
            new["ka"], new["kb"] = k_p[:, :LANES], k_p[:, LANES:]
            new["va"], new["vb"] = v_p[:, :LANES], v_p[:, LANES:]
            sink = sink_b[e].astype(F32)
            groups = (tuple(((j,), 0, None) for j in range(4))
                      + tuple(((4 + j,), 1, (j, j + 4)) for j in range(4)))
            o_p = _ctx_attention(q_p, k_p, v_p, sink, groups=groups, seqs=4)

            k_s3 = k_s.reshape(DEC_BATCH, DEC_SEQ, 2 * LANES)
            v_s3 = v_s.reshape(DEC_BATCH, DEC_SEQ, 2 * LANES)
            cka = cache_ka[:, e].reshape(DEC_BATCH, PAST_LEN, LANES).astype(BF16)
            cva = cache_va[:, e].reshape(DEC_BATCH, PAST_LEN, LANES).astype(BF16)
            ckb = cache_kb[:, e].reshape(DEC_BATCH, PAST_LEN, LANES).astype(BF16)
            cvb = cache_vb[:, e].reshape(DEC_BATCH, PAST_LEN, LANES).astype(BF16)
            pad = jnp.zeros((DEC_BATCH, WINDOW, LANES), BF16)
            o_a = _global_attention(q_s, jnp.concatenate([cka, k_s3[:, :, :LANES]], axis=1),
                                    jnp.concatenate([cva, v_s3[:, :, :LANES]], axis=1))
            o_b = _window_attention(q_s, jnp.concatenate([ckb, k_s3[:, :, LANES:], pad], axis=1),
                                    jnp.concatenate([cvb, v_s3[:, :, LANES:], pad], axis=1), sink)
            parts_p = [(o_p, 0), (o_p, 1)]
            parts_s = [(o_a, 0), (o_b, 0)]
        else:
            o = l // 2
            w_in = w_in_odd[o].astype(BF16)
            w_out = w_out_odd[o].astype(BF16)
            proj = functools.partial(_project, w_bf16=w_in, qn2=ones_row, kn2=ones_row, cos=cos,
                                     sin_signed=sin_signed, dq=1024, dk=1024, dv=1024, q_norm=0, k_norm=0,
                                     rope=False)
            q_p, k_p, v_p = proj(y_p, mod, latent=False)
            q_s, k_s, v_s = proj(y_s, mod, latent=True)
            new["kc"], new["vc"] = k_p, v_p
            groups = tuple(((j,), j, None) for j in range(D // LANES))
            o_p = _ctx_attention(q_p, k_p, v_p, no_sink, groups=groups, seqs=4)
            kc = cache_kc[:, o].reshape(DEC_BATCH, PAST_LEN, D).astype(BF16)
            vc = cache_vc[:, o].reshape(DEC_BATCH, PAST_LEN, D).astype(BF16)
            o_s = _na_attention(q_s, k_s, v_s, kc, vc, _na_bias_table(rpb_c[o]))
            parts_p = [(o_p, 0)]
            parts_s = [(o_s, 0)]

        wr = w_router[l].T
        wr_hi = wr.astype(BF16)
        wr_lo = (wr - wr_hi.astype(F32)).astype(BF16)
        post = functools.partial(_post_mixer, mod=mod, w_out_bf16=w_out, ln_g=ln_g[l, 0][None, :],
                                 ln_b=ln_b[l, 0][None, :], wr_hi=wr_hi, wr_lo=wr_lo, b_router=b_router[l][:, None])
        y_p, h_p, g_p = post(parts_p, y_p, latent=False)
        y_s, h_s, g_s = post(parts_s, y_s, latent=True)
        moe = functools.partial(_moe, mod=mod, w_gate=w_gate, w_up=w_up, w_down=w_down,
                                ws_gate=ws_gate, ws_up=ws_up, ws_down=ws_down,
                                ln_g=ln_g[l, 1][None, :], ln_b=ln_b[l, 1][None, :], layer=l)
        y_p = moe(h_p, g_p, y_p, latent=False)
        y_s = moe(h_s, g_s, y_s, latent=True)

    kv_a = (BATCH, 1, SEQ, N_KV_A, HEAD_DIM)
    kv_c = (BATCH, 1, SEQ, N_HEADS_C, HEAD_DIM)
    return (y_p.reshape(BATCH, SEQ, D), y_s.reshape(DEC_BATCH, DEC_SEQ, D),
            new["ka"].reshape(kv_a), new["va"].reshape(kv_a), new["kb"].reshape(kv_a), new["vb"].reshape(kv_a),
            new["kc"].reshape(kv_c), new["vc"].reshape(kv_c))
```
